```python
import math
import jax, jax.numpy as jnp
from jax import lax
import numpy as np

D_MODEL = 1024
BATCH = 1
SEQ = 16384
DEPTH = 2

CHUNK = 64
BRANCH_WIDTH = D_MODEL // 2
MIX_WIDTH = 3 * BRANCH_WIDTH
S5_GROUP = 16
S5_GROUPS = BRANCH_WIDTH // S5_GROUP
S5_STATE = 64
RWKV_HEAD_DIM = 64
RWKV_HEADS = BRANCH_WIDTH // RWKV_HEAD_DIM
RWKV_LORA_W = 64
RWKV_LORA_A = 64
RWKV_LORA_V = 32
RWKV_GN_EPS = 64e-5
M2_HEAD_DIM = 64
M2_HEADS = BRANCH_WIDTH // M2_HEAD_DIM
M2_GROUPS = 2
M2_STATE = 128
M2_CONV = 4
M2_XBC = BRANCH_WIDTH + 2 * M2_GROUPS * M2_STATE
NORM_EPS = 1e-5

S5_COLS = (BRANCH_WIDTH, BRANCH_WIDTH)
RWKV_COLS = (BRANCH_WIDTH, BRANCH_WIDTH, BRANCH_WIDTH,
             RWKV_LORA_W, RWKV_LORA_A, BRANCH_WIDTH)
M2_COLS = (M2_XBC, M2_HEADS, BRANCH_WIDTH)
S5_PROJ = 2 * BRANCH_WIDTH
RWKV_PROJ = 4 * BRANCH_WIDTH + RWKV_LORA_W + RWKV_LORA_A
M2_PROJ = M2_XBC + M2_HEADS + BRANCH_WIDTH
PROJ_WIDTH = S5_PROJ + RWKV_PROJ + M2_PROJ

kernel_name = 'hybrid_s5_rwkv7_mamba2_parallel_heads'


def rmsnorm(x, g):
    xf = x.astype(jnp.float32)
    y = xf * lax.rsqrt(jnp.mean(xf * xf, axis=-1, keepdims=True) + NORM_EPS)
    return y * g.astype(jnp.float32)


def split_cols(t, sizes):
    offsets = np.cumsum(sizes)[:-1].tolist()
    return jnp.split(t, offsets, axis=-1)


def token_shift_lerp(t, mu):
    prev = jnp.pad(t, ((0, 0), (1, 0), (0, 0)))[:, :-1]
    return t + (prev - t) * mu.astype(jnp.float32)


def _complex_linear_combine(e1, e2):
    a1r, a1i, b1r, b1i = e1
    a2r, a2i, b2r, b2i = e2
    return (a1r * a2r - a1i * a2i,
            a1r * a2i + a1i * a2r,
            a2r * b1r - a2i * b1i + b2r,
            a2r * b1i + a2i * b1r + b2i)


def s5_mixer(u, log_dt, a_re, a_im, b_re, b_im, c_re, c_im, d_skip, w_glu, b_glu):
    f32 = jnp.float32
    bsz, seq, _ = u.shape
    uf = u.reshape(bsz, seq, S5_GROUPS, S5_GROUP)
    ar, ai = a_re.astype(f32), a_im.astype(f32)
    dt = jnp.exp(log_dt.astype(f32))[:, None]
    mag = jnp.exp(dt * ar)
    abar_re, abar_im = mag * jnp.cos(dt * ai), mag * jnp.sin(dt * ai)
    den = ar * ar + ai * ai
    f_re = ((abar_re - 1.0) * ar + abar_im * ai) / den
    f_im = (abar_im * ar - (abar_re - 1.0) * ai) / den
    br, bi = b_re.astype(f32), b_im.astype(f32)
    bb_re = f_re[..., None] * br - f_im[..., None] * bi
    bb_im = f_re[..., None] * bi + f_im[..., None] * br
    bu_re = jnp.einsum('blgc,gpc->blgp', uf, bb_re)
    bu_im = jnp.einsum('blgc,gpc->blgp', uf, bb_im)
    a_seq_re = jnp.broadcast_to(abar_re, bu_re.shape)
    a_seq_im = jnp.broadcast_to(abar_im, bu_im.shape)
    _, _, s_re, s_im = lax.associative_scan(
        _complex_linear_combine, (a_seq_re, a_seq_im, bu_re, bu_im), axis=1)
    y = (jnp.einsum('blgp,gcp->blgc', s_re, c_re.astype(f32))
         - jnp.einsum('blgp,gcp->blgc', s_im, c_im.astype(f32))
         + d_skip.astype(f32) * uf)
    y = jax.nn.gelu(y.reshape(bsz, seq, BRANCH_WIDTH))
    return y * jax.nn.sigmoid(y @ w_glu.astype(f32) + b_glu.astype(f32))


def rwkv7_recurrence(r, w, k, v, a_vec, b_vec):
    bsz, _, n_heads, n = r.shape

    def step(state, inp):
        r_t, w_t, k_t, v_t, a_t, b_t = inp
        sa = jnp.einsum('bhvk,bhk->bhv', state, a_t)
        state = (state * w_t[:, :, None, :]
                 + sa[..., None] * b_t[:, :, None, :]
                 + v_t[..., None] * k_t[:, :, None, :])
        return state, jnp.einsum('bhvk,bhk->bhv', state, r_t)

    s0 = jnp.zeros((bsz, n_heads, n, n), jnp.float32)
    xs = tuple(jnp.moveaxis(t, 1, 0) for t in (r, w, k, v, a_vec, b_vec))
    _, ys = lax.scan(step, s0, xs)
    return jnp.moveaxis(ys, 0, 1)


def rwkv7_mixer(cols, mu, w_up, w0, a_up, a0, k_k, k_a, r_k, ln_g, ln_b,
                v_first, vd_cols, vmix_mu, v_up, v0):
    f32 = jnp.float32
    bsz, seq, _ = cols.shape
    xs = token_shift_lerp(cols, mu)
    r, k, v, wd, ad, g = split_cols(xs, RWKV_COLS)
    w_log = -jax.nn.softplus(-(w0.astype(f32) + jnp.tanh(wd) @ w_up.astype(f32))) - 0.5
    decay = jnp.exp(-jnp.exp(w_log))
    a = jax.nn.sigmoid(a0.astype(f32) + ad @ a_up.astype(f32))
    if v_first is None:
        v_first = v
    else:
        vd = token_shift_lerp(vd_cols, vmix_mu)
        v = v + (v_first - v) * jax.nn.sigmoid(v0.astype(f32) + vd @ v_up.astype(f32))
    heads = lambda t: t.reshape(bsz, seq, RWKV_HEADS, RWKV_HEAD_DIM)
    kk = heads(k * k_k.astype(f32))
    kk = kk / jnp.maximum(jnp.sqrt(jnp.sum(kk * kk, axis=-1, keepdims=True)), 1e-12)
    k = k * (1.0 + (a - 1.0) * k_a.astype(f32))
    rh, kh, vh, ah = heads(r), heads(k), heads(v), heads(a)
    y = rwkv7_recurrence(rh, heads(decay), kh, vh, -kk, kk * ah)
    mean = jnp.mean(y, axis=-1, keepdims=True)
    var = jnp.mean(jnp.square(y - mean), axis=-1, keepdims=True)
    y = ((y - mean) * lax.rsqrt(var + RWKV_GN_EPS)).reshape(bsz, seq, BRANCH_WIDTH)
    y = y * ln_g.astype(f32) + ln_b.astype(f32)
    bonus = jnp.sum(rh * kh * r_k.astype(f32), axis=-1, keepdims=True) * vh
    y = y + bonus.reshape(bsz, seq, BRANCH_WIDTH)
    return y * jax.nn.silu(g), v_first


def causal_depthwise_conv(t, w, b):
    k_width, ch = w.shape
    y = lax.conv_general_dilated(
        t, w.astype(jnp.float32)[:, None, :], window_strides=(1,),
        padding=[(k_width - 1, 0)], dimension_numbers=('NWC', 'WIO', 'NWC'),
        feature_group_count=ch)
    return y + b.astype(jnp.float32)


def ssd_chunked(xdt, da, b, c):
    bsz, seq, n_heads, p = xdt.shape
    n_chunks = seq // CHUNK
    xdt = xdt.reshape(bsz, n_chunks, CHUNK, n_heads, p)
    b = b.reshape(bsz, n_chunks, CHUNK, n_heads, -1)
    c = c.reshape(bsz, n_chunks, CHUNK, n_heads, -1)
    cs = jnp.cumsum(da.reshape(bsz, n_chunks, CHUNK, n_heads), axis=2)
    seg = cs[:, :, :, None, :] - cs[:, :, None, :, :]
    causal = jnp.tril(jnp.ones((CHUNK, CHUNK), bool))[:, :, None]
    decay_in = jnp.exp(jnp.where(causal, seg, -jnp.inf))
    scores = jnp.einsum('bclhn,bcshn->bclsh', c, b) * decay_in
    y_diag = jnp.einsum('bclsh,bcshp->bclhp', scores, xdt)
    decay_to_end = jnp.exp(cs[:, :, -1:, :] - cs)
    chunk_states = jnp.einsum('bclhn,bclh,bclhp->bchpn', b, decay_to_end, xdt)
    chunk_decay = jnp.exp(cs[:, :, -1, :])

    def step(state, inp):
        s_c, g_c = inp
        return state * g_c[:, :, None, None] + s_c, state

    h0 = jnp.zeros((bsz, n_heads, p, b.shape[-1]), jnp.float32)
    _, prev = lax.scan(step, h0, (jnp.moveaxis(chunk_states, 1, 0), jnp.moveaxis(chunk_decay, 1, 0)))
    prev = jnp.moveaxis(prev, 0, 1)
    y_off = jnp.einsum('bclhn,bchpn,bclh->bclhp', c, prev, jnp.exp(cs))
    return (y_diag + y_off).reshape(bsz, seq, n_heads, p)


def mamba2_mixer(xbc, dt_raw, z, conv_w, conv_b, dt_bias, a_log, d_skip, norm_g):
    f32 = jnp.float32
    bsz, seq, _ = xbc.shape
    xbc = jax.nn.silu(causal_depthwise_conv(xbc, conv_w, conv_b))
    xm, bm, cm = split_cols(xbc, (BRANCH_WIDTH, M2_GROUPS * M2_STATE, M2_GROUPS * M2_STATE))
    heads_per_group = M2_HEADS // M2_GROUPS
    x = xm.reshape(bsz, seq, M2_HEADS, M2_HEAD_DIM)
    bm = jnp.repeat(bm.reshape(bsz, seq, M2_GROUPS, M2_STATE), heads_per_group, axis=2)
    cm = jnp.repeat(cm.reshape(bsz, seq, M2_GROUPS, M2_STATE), heads_per_group, axis=2)
    dt = jax.nn.softplus(dt_raw + dt_bias.astype(f32))
    a = -jnp.exp(a_log.astype(f32))
    y = ssd_chunked(x * dt[..., None], dt * a, bm, cm) + x * d_skip.astype(f32)[:, None]
    y = y.reshape(bsz, seq, BRANCH_WIDTH)
    return rmsnorm(y * jax.nn.silu(z), norm_g)


def setup_inputs(seed: int = 0) -> dict:
    key = jax.random.key(seed)
    ks = list(jax.random.split(key, 40))
    f32 = jnp.float32
    nrm = lambda shape, scale: scale * jax.random.normal(ks.pop(), shape, f32)
    L = DEPTH
    Lv = DEPTH - 1
    BW = BRANCH_WIDTH
    n_state = jnp.arange(S5_STATE, dtype=f32)
    lin = jnp.arange(BW, dtype=f32) / (BW - 1)
    m2_dt = jnp.exp(jax.random.uniform(ks.pop(), (L, M2_HEADS), f32, math.log(1e-3), math.log(1e-1)))
    return {
        'x': nrm((BATCH, SEQ, D_MODEL), 1.0),
        'norm_g': 1.0 + nrm((L, D_MODEL), 0.02),
        'w_in': nrm((L, D_MODEL, PROJ_WIDTH), D_MODEL ** -0.5),
        'w_in_vmix': nrm((Lv, D_MODEL, RWKV_LORA_V), D_MODEL ** -0.5),
        's5_log_dt': jax.random.uniform(ks.pop(), (L, S5_GROUPS), f32, math.log(1e-3), math.log(1e-1)),
        's5_a_re': -0.5 + nrm((L, S5_GROUPS, S5_STATE), 0.01),
        's5_a_im': math.pi * n_state + nrm((L, S5_GROUPS, S5_STATE), 0.01),
        's5_b_re': nrm((L, S5_GROUPS, S5_STATE, S5_GROUP), S5_GROUP ** -0.5),
        's5_b_im': nrm((L, S5_GROUPS, S5_STATE, S5_GROUP), S5_GROUP ** -0.5),
        's5_c_re': nrm((L, S5_GROUPS, S5_GROUP, S5_STATE), S5_STATE ** -0.5),
        's5_c_im': nrm((L, S5_GROUPS, S5_GROUP, S5_STATE), S5_STATE ** -0.5),
        's5_d': nrm((L, S5_GROUPS, S5_GROUP), 1.0),
        's5_w_glu': nrm((L, BW, BW), BW ** -0.5),
        's5_b_glu': nrm((L, BW), 0.01),
        'rwkv_mu': jax.random.uniform(ks.pop(), (L, RWKV_PROJ), f32),
        'rwkv_w_up': nrm((L, RWKV_LORA_W, BW), 0.1 * RWKV_LORA_W ** -0.5),
        'rwkv_w0': (-6.0 + 5.0 * lin ** 0.85 + 0.5) + nrm((L, BW), 0.1),
        'rwkv_a_up': nrm((L, RWKV_LORA_A, BW), 0.1 * RWKV_LORA_A ** -0.5),
        'rwkv_a0': nrm((L, BW), 0.1),
        'rwkv_k_k': 0.85 + nrm((L, BW), 0.02),
        'rwkv_k_a': 1.0 + nrm((L, BW), 0.02),
        'rwkv_r_k': -0.04 + nrm((L, RWKV_HEADS, RWKV_HEAD_DIM), 0.01),
        'rwkv_ln_g': 1.0 + nrm((L, BW), 0.02),
        'rwkv_ln_b': nrm((L, BW), 0.01),
        'rwkv_vmix_mu': jax.random.uniform(ks.pop(), (Lv, RWKV_LORA_V), f32),
        'rwkv_v_up': nrm((Lv, RWKV_LORA_V, BW), 0.1 * RWKV_LORA_V ** -0.5),
        'rwkv_v0': 1.0 + nrm((Lv, BW), 0.1),
        'm2_conv_w': nrm((L, M2_CONV, M2_XBC), 0.5),
        'm2_conv_b': nrm((L, M2_XBC), 0.01),
        'm2_dt_bias': m2_dt + jnp.log(-jnp.expm1(-m2_dt)),
        'm2_a_log': jnp.log(jax.random.uniform(ks.pop(), (L, M2_HEADS), f32, 1.0, 16.0)),
        'm2_d': 1.0 + nrm((L, M2_HEADS), 0.1),
        'm2_norm_g': 1.0 + nrm((L, BW), 0.02),
        'w_out': nrm((L, MIX_WIDTH, D_MODEL), MIX_WIDTH ** -0.5),
        'final_norm_g': 1.0 + nrm((D_MODEL,), 0.02),
    }


def reference(x, norm_g, w_in, w_in_vmix, s5_log_dt, s5_a_re, s5_a_im, s5_b_re, s5_b_im,
              s5_c_re, s5_c_im, s5_d, s5_w_glu, s5_b_glu, rwkv_mu, rwkv_w_up, rwkv_w0,
              rwkv_a_up, rwkv_a0, rwkv_k_k, rwkv_k_a, rwkv_r_k, rwkv_ln_g, rwkv_ln_b,
              rwkv_vmix_mu, rwkv_v_up, rwkv_v0, m2_conv_w, m2_conv_b, m2_dt_bias, m2_a_log,
              m2_d, m2_norm_g, w_out, final_norm_g):
    f32 = jnp.float32
    h = x
    v_first = None
    for i in range(DEPTH):
        hn = rmsnorm(h, norm_g[i])
        if i == 0:
            w_proj = w_in[i]
        else:
            w_proj = jnp.concatenate([w_in[i], w_in_vmix[i - 1]], axis=1)
        proj = hn @ w_proj.astype(f32)
        s5_cols = proj[..., :S5_PROJ]
        rw_cols = proj[..., S5_PROJ:S5_PROJ + RWKV_PROJ]
        m2_cols = proj[..., S5_PROJ + RWKV_PROJ:PROJ_WIDTH]

        u, g_s5 = split_cols(s5_cols, S5_COLS)
        y_s5 = s5_mixer(u, s5_log_dt[i], s5_a_re[i], s5_a_im[i], s5_b_re[i], s5_b_im[i],
                        s5_c_re[i], s5_c_im[i], s5_d[i], s5_w_glu[i], s5_b_glu[i]) * jax.nn.silu(g_s5)

        if i == 0:
            y_rw, v_first = rwkv7_mixer(rw_cols, rwkv_mu[i], rwkv_w_up[i], rwkv_w0[i], rwkv_a_up[i],
                                        rwkv_a0[i], rwkv_k_k[i], rwkv_k_a[i], rwkv_r_k[i],
                                        rwkv_ln_g[i], rwkv_ln_b[i], None, None, None, None, None)
        else:
            y_rw, v_first = rwkv7_mixer(rw_cols, rwkv_mu[i], rwkv_w_up[i], rwkv_w0[i], rwkv_a_up[i],
                                        rwkv_a0[i], rwkv_k_k[i], rwkv_k_a[i], rwkv_r_k[i],
                                        rwkv_ln_g[i], rwkv_ln_b[i], v_first, proj[..., PROJ_WIDTH:],
                                        rwkv_vmix_mu[i - 1], rwkv_v_up[i - 1], rwkv_v0[i - 1])

        xbc, dt_raw, z = split_cols(m2_cols, M2_COLS)
        y_m2 = mamba2_mixer(xbc, dt_raw, z, m2_conv_w[i], m2_conv_b[i], m2_dt_bias[i],
                            m2_a_log[i], m2_d[i], m2_norm_g[i])

        mixed = jnp.concatenate([y_s5, y_rw, y_m2], axis=-1)
        h = h + (mixed @ w_out[i].astype(f32)).astype(h.dtype)
    return rmsnorm(h, final_norm_g).astype(x.dtype)
```

```python
import functools
import math

import jax
import jax.numpy as jnp
import numpy as np
from jax import lax
from jax.experimental import pallas as pl
from jax.experimental.pallas import tpu as pltpu

F32 = jnp.float32
BF16 = jnp.bfloat16
HI = lax.Precision.HIGHEST

D_MODEL = 1024
BW = 512
S5_GROUP = 16
S5_GROUPS = 32
S5_STATE = 64
HEAD = 64
N_HEADS = 8
N_PAIRS = N_HEADS // 2
LORA_W = 64
LORA_A = 64
LORA_V = 32
GN_EPS = 64e-5
M2_GROUPS = 2
M2_STATE = 128
M2_CONV = 4
M2_XBC = 1024
NORM_EPS = 1e-5

LANES = 128
SUBLANES = 8

COL_RW = 0
COL_U = 2048
COL_GS5 = 2560
COL_XBC = 3072
COL_Z = 4096
COL_WA = 4608
COL_MISC = 4736
PROJ_PAD = 4864

PROJ_TM = 256
PROJ_TN = 512
S5_T = 32
RW_T = 64
M2_T = 128
OUT_TM = 256
VMEM_LIMIT = 48 * 1024 * 1024


def _dot(a, b, prec=None):
    return jnp.dot(a, b, precision=prec, preferred_element_type=F32)


def _dot_nt(a, b, prec=None):
    return lax.dot_general(a, b, (((1,), (1,)), ((), ())), precision=prec, preferred_element_type=F32)


def _dot_tn(a, b, prec=None):
    return lax.dot_general(a, b, (((0,), (0,)), ((), ())), precision=prec, preferred_element_type=F32)


def _sigmoid(x):
    return 1.0 / (1.0 + jnp.exp(-x))


def _silu(x):
    return x * _sigmoid(x)


def _softplus(x):
    return jnp.maximum(x, 0.0) + jnp.log1p(jnp.exp(-jnp.abs(x)))


def _gelu_tanh(x):
    c = math.sqrt(2.0 / math.pi)
    return 0.5 * x * (1.0 + jnp.tanh(c * (x + 0.044715 * (x * x * x))))


def _prev_rows(x, halo, is_first):
    rolled = pltpu.roll(x, 1, axis=0)
    last = jnp.where(is_first, 0.0, halo[SUBLANES - 1:SUBLANES, :])
    row = lax.broadcasted_iota(jnp.int32, x.shape, 0)
    return jnp.where(row == 0, last, rolled)


def _proj_kernel(h_ref, g_ref, w_ref, o_ref):
    x = h_ref[...]
    ms = jnp.mean(x * x, axis=-1, keepdims=True)
    hn = (x * lax.rsqrt(ms + NORM_EPS) * g_ref[...]).astype(BF16)
    for j in range(0, PROJ_PAD, PROJ_TN):
        w = min(PROJ_TN, PROJ_PAD - j)
        o_ref[:, j:j + w] = _dot(hn, w_ref[:, j:j + w])


def _proj(h, g, w):
    L = h.shape[0]
    return pl.pallas_call(
        _proj_kernel,
        grid=(L // PROJ_TM,),
        in_specs=[
            pl.BlockSpec((PROJ_TM, D_MODEL), lambda i: (i, 0)),
            pl.BlockSpec((1, D_MODEL), lambda i: (0, 0)),
            pl.BlockSpec((D_MODEL, PROJ_PAD), lambda i: (0, 0)),
        ],
        out_specs=pl.BlockSpec((PROJ_TM, PROJ_PAD), lambda i: (i, 0)),
        out_shape=jax.ShapeDtypeStruct((L, PROJ_PAD), F32),
        compiler_params=pltpu.CompilerParams(
            dimension_semantics=("parallel",), vmem_limit_bytes=VMEM_LIMIT),
    )(h, g, w)


def _s5_kernel(u_ref, k_ref, e_ref, cm_ref, a1_ref, a2_ref, y_ref, *, nsteps):
    ub = u_ref[0].astype(BF16)
    x = _dot(ub, e_ref[0])
    row = lax.broadcasted_iota(jnp.int32, x.shape, 0)
    for s in range(nsteps):
        d = 1 << s
        sh = jnp.where(row >= d, pltpu.roll(x, d, axis=0), 0.0)
        x = x + a1_ref[0, s:s + 1, :] * sh + a2_ref[0, s:s + 1, :] * pltpu.roll(sh, S5_STATE, axis=1)
    sprev = jnp.where(row >= 1, pltpu.roll(x, 1, axis=0), 0.0)
    y_ref[0] = _dot(ub, k_ref[0]) + _dot(sprev.astype(BF16), cm_ref[0])


def _s5_tables(log_dt, a_re, a_im, b_re, b_im, c_re, c_im, d_skip, n_chunks):
    T = S5_T
    G, P = a_re.shape
    dt = jnp.exp(log_dt.astype(F32))[:, None]
    ar, ai = a_re.astype(F32), a_im.astype(F32)
    mag = jnp.exp(dt * ar)
    abar_re, abar_im = mag * jnp.cos(dt * ai), mag * jnp.sin(dt * ai)
    den = ar * ar + ai * ai
    f_re = ((abar_re - 1.0) * ar + abar_im * ai) / den
    f_im = (abar_im * ar - (abar_re - 1.0) * ai) / den
    br, bi = b_re.astype(F32), b_im.astype(F32)
    bb_re = f_re[..., None] * br - f_im[..., None] * bi
    bb_im = f_re[..., None] * bi + f_im[..., None] * br

    def powers(tau):
        tau = tau.astype(F32)[None, :, None]
        m = jnp.exp(tau * (dt * ar)[:, None, :])
        ang = tau * (dt * ai)[:, None, :]
        return m * jnp.cos(ang), m * jnp.sin(ang)

    pw_re, pw_im = powers(jnp.arange(T + 1))
    cr, ci = c_re.astype(F32), c_im.astype(F32)
    cp_re = cr[:, None] * pw_re[:, :, None, :] - ci[:, None] * pw_im[:, :, None, :]
    cp_im = cr[:, None] * pw_im[:, :, None, :] + ci[:, None] * pw_re[:, :, None, :]
    k_lag = (jnp.einsum('gtop,gpi->gtio', cp_re[:, :T], bb_re, precision=HI)
             - jnp.einsum('gtop,gpi->gtio', cp_im[:, :T], bb_im, precision=HI))
    eye = jnp.eye(S5_GROUP, dtype=F32)
    k_lag = k_lag.at[:, 0].add(d_skip.astype(F32)[:, :, None] * eye)
    idx = jnp.arange(T)[None, :] - jnp.arange(T)[:, None]
    toe = jnp.where((idx >= 0)[None, :, :, None, None], k_lag[:, jnp.clip(idx, 0, T - 1)], 0.0)
    toe = toe.transpose(0, 1, 3, 2, 4).reshape(G, T * S5_GROUP, T * S5_GROUP)
    rev_re, rev_im = pw_re[:, T - 1::-1][:, :T], pw_im[:, T - 1::-1][:, :T]
    bbt_re, bbt_im = bb_re.transpose(0, 2, 1), bb_im.transpose(0, 2, 1)
    e_re = rev_re[:, :, None, :] * bbt_re[:, None] - rev_im[:, :, None, :] * bbt_im[:, None]
    e_im = rev_re[:, :, None, :] * bbt_im[:, None] + rev_im[:, :, None, :] * bbt_re[:, None]
    e_map = jnp.concatenate([e_re, e_im], axis=-1).reshape(G, T * S5_GROUP, 2 * P)
    cm_re = cp_re[:, 1:].transpose(0, 3, 1, 2)
    cm_im = -cp_im[:, 1:].transpose(0, 3, 1, 2)
    c_map = jnp.concatenate([cm_re, cm_im], axis=1).reshape(G, 2 * P, T * S5_GROUP)
    nsteps = max(1, int(math.ceil(math.log2(n_chunks))))
    sp_re, sp_im = powers(T * (2 ** jnp.arange(nsteps)))
    a1 = jnp.concatenate([sp_re, sp_re], axis=-1)
    a2 = jnp.concatenate([-sp_im, sp_im], axis=-1)
    pad = (-nsteps) % SUBLANES
    a1 = jnp.pad(a1, ((0, 0), (0, pad), (0, 0)))
    a2 = jnp.pad(a2, ((0, 0), (0, pad), (0, 0)))
    return toe.astype(BF16), e_map.astype(BF16), c_map.astype(BF16), a1, a2, nsteps


def _s5(u, tables):
    toe, e_map, c_map, a1, a2, nsteps = tables
    L = u.shape[0]
    T, G, W = S5_T, S5_GROUPS, S5_T * S5_GROUP
    C = L // T
    ug = u.reshape(C, T, G, S5_GROUP).transpose(2, 0, 1, 3).reshape(G, C, W)
    yg = pl.pallas_call(
        functools.partial(_s5_kernel, nsteps=nsteps),
        grid=(G,),
        in_specs=[
            pl.BlockSpec((1, C, W), lambda g: (g, 0, 0)),
            pl.BlockSpec((1, W, W), lambda g: (g, 0, 0)),
            pl.BlockSpec((1, W, 2 * S5_STATE), lambda g: (g, 0, 0)),
            pl.BlockSpec((1, 2 * S5_STATE, W), lambda g: (g, 0, 0)),
            pl.BlockSpec((1, a1.shape[1], 2 * S5_STATE), lambda g: (g, 0, 0)),
            pl.BlockSpec((1, a2.shape[1], 2 * S5_STATE), lambda g: (g, 0, 0)),
        ],
        out_specs=pl.BlockSpec((1, C, W), lambda g: (g, 0, 0)),
        out_shape=jax.ShapeDtypeStruct((G, C, W), F32),
        compiler_params=pltpu.CompilerParams(
            dimension_semantics=("parallel",), vmem_limit_bytes=VMEM_LIMIT),
    )(ug, toe, e_map, c_map, a1, a2)
    return yg.reshape(G, C, T, S5_GROUP).transpose(1, 2, 0, 3).reshape(L, BW)


def _rwkv_kernel(*refs, has_vmix):
    if has_vmix:
        (rw_ref, rwp_ref, wa_ref, wap_ref, misc_ref, miscp_ref, vfirst_ref,
         mu_rw_ref, mu_wa_ref, mu_misc_ref, wup_ref, w0_ref, aup_ref, a0_ref, kk_ref, ka_ref, rk_ref,
         lng_ref, lnb_ref, vup_ref, v0_ref, tri_ref, y_ref, s_ref) = refs
    else:
        (rw_ref, rwp_ref, wa_ref, wap_ref,
         mu_rw_ref, mu_wa_ref, wup_ref, w0_ref, aup_ref, a0_ref, kk_ref, ka_ref, rk_ref,
         lng_ref, lnb_ref, tri_ref, y_ref, vout_ref, s_ref) = refs
    T = RW_T
    first = pl.program_id(0) == 0

    @pl.when(first)
    def _():
        s_ref[...] = jnp.zeros_like(s_ref)

    def lerp(ref, halo_ref, mu_ref):
        x = ref[...]
        return x + (_prev_rows(x, halo_ref[...], first) - x) * mu_ref[...]

    xs = lerp(rw_ref, rwp_ref, mu_rw_ref)
    r, k, v, g = (xs[:, i * BW:(i + 1) * BW] for i in range(4))
    wa = lerp(wa_ref, wap_ref, mu_wa_ref)
    w_log = -_softplus(-(w0_ref[...] + _dot(jnp.tanh(wa), wup_ref[...], HI))) - 0.5
    lw = -jnp.exp(w_log)
    a = _sigmoid(a0_ref[...] + _dot(wa, aup_ref[...], HI))
    if has_vmix:
        vd = lerp(misc_ref, miscp_ref, mu_misc_ref)
        mix = _sigmoid(v0_ref[...] + _dot(vd, vup_ref[...], HI))
        v = v + (vfirst_ref[...] - v) * mix
    else:
        vout_ref[...] = v
    kk = k * kk_ref[...]
    kmod = k * (1.0 + (a - 1.0) * ka_ref[...])
    cl = _dot(tri_ref[...], lw, HI)
    g_incl = jnp.exp(cl)
    g_inv = jnp.exp(-cl)
    pa = kk * jnp.exp(cl - lw)
    pb = kk * a * g_inv
    pk = kmod * g_inv
    pr = r * g_incl
    rkr = r * kmod * rk_ref[...]

    lane = lax.broadcasted_iota(jnp.int32, (T, LANES), 1)
    head0 = lane < HEAD
    rr = lax.broadcasted_iota(jnp.int32, (2 * T, 2 * T), 0)
    cc = lax.broadcasted_iota(jnp.int32, (2 * T, 2 * T), 1)
    strict = cc < rr
    incl = cc <= rr
    eye = (cc == rr).astype(F32)
    srow = lax.broadcasted_iota(jnp.int32, (2 * T, LANES), 0)
    slane = lax.broadcasted_iota(jnp.int32, (2 * T, LANES), 1)
    own = (srow < T) == (slane < HEAD)

    def stack(x):
        return jnp.concatenate([jnp.where(head0, x, 0.0), jnp.where(head0, 0.0, x)], axis=0)

    ys = []
    for q in range(N_PAIRS):
        sl = slice(q * LANES, (q + 1) * LANES)
        kk_h = stack(kk[:, sl])
        nrm = jnp.sqrt(jnp.sum(kk_h * kk_h, axis=-1, keepdims=True))
        inv = 1.0 / jnp.maximum(nrm, 1e-12)
        a_t = -stack(pa[:, sl]) * inv
        b_t = stack(pb[:, sl]) * inv
        k_t = stack(pk[:, sl])
        r_t = stack(pr[:, sl])
        v_h = stack(v[:, sl])
        aa = _dot_nt(jnp.concatenate([a_t, r_t], axis=0), jnp.concatenate([b_t, k_t], axis=0), HI)
        a_ab = jnp.where(strict, aa[:2 * T, :2 * T], 0.0)
        a_ak = jnp.where(strict, aa[:2 * T, 2 * T:], 0.0)
        a_rb = jnp.where(incl, aa[2 * T:, :2 * T], 0.0)
        a_rk = jnp.where(incl, aa[2 * T:, 2 * T:], 0.0)
        p = a_ab
        minv = eye + p
        for _ in range(int(math.log2(T)) - 1):
            p = _dot(p, p, HI)
            minv = minv + _dot(p, minv, HI)
        s = s_ref[q]
        u = _dot(minv, _dot_nt(a_t, s, HI) + _dot(a_ak, v_h, HI), HI)
        yh = _dot_nt(r_t, s, HI) + _dot(a_rb, u, HI) + _dot(a_rk, v_h, HI)
        s_ref[q] = (s + _dot_tn(u, b_t, HI) + _dot_tn(v_h, k_t, HI)) * g_incl[T - 1:T, sl]
        mean = jnp.sum(yh, axis=-1, keepdims=True) * (1.0 / HEAD)
        cen = jnp.where(own, yh - mean, 0.0)
        var = jnp.sum(cen * cen, axis=-1, keepdims=True) * (1.0 / HEAD)
        yn = cen * lax.rsqrt(var + GN_EPS)
        bonus = jnp.sum(stack(rkr[:, sl]), axis=-1, keepdims=True) * v_h
        ys.append((yn[:T] + yn[T:], bonus[:T] + bonus[T:]))
    yn = jnp.concatenate([t[0] for t in ys], axis=1)
    bonus = jnp.concatenate([t[1] for t in ys], axis=1)
    y_ref[...] = (yn * lng_ref[...] + lnb_ref[...] + bonus) * _silu(g)


def _rwkv(proj, p, v_first):
    L = proj.shape[0]
    T = RW_T
    has_vmix = v_first is not None
    hb = T // SUBLANES

    def cur(width, col):
        return pl.BlockSpec((T, width), lambda i: (i, col // width))

    def halo(width, col):
        return pl.BlockSpec((SUBLANES, width), lambda i: (jnp.maximum(i * hb - 1, 0), col // width))

    def row(width):
        return pl.BlockSpec((1, width), lambda i: (0, 0))

    def full(a):
        return pl.BlockSpec(a.shape, lambda i: (0, 0))

    tri = jnp.tril(jnp.ones((T, T), F32))
    args = [proj, proj, proj, proj]
    specs = [cur(4 * BW, COL_RW), halo(4 * BW, COL_RW), cur(LANES, COL_WA), halo(LANES, COL_WA)]
    if has_vmix:
        args += [proj, proj, v_first]
        specs += [cur(LANES, COL_MISC), halo(LANES, COL_MISC), pl.BlockSpec((T, BW), lambda i: (i, 0))]
        names = ['mu_rw', 'mu_wa', 'mu_misc', 'w_up', 'w0', 'a_up', 'a0', 'k_k', 'k_a', 'r_k',
                 'ln_g', 'ln_b', 'v_up', 'v0']
    else:
        names = ['mu_rw', 'mu_wa', 'w_up', 'w0', 'a_up', 'a0', 'k_k', 'k_a', 'r_k', 'ln_g', 'ln_b']
    for n in names:
        args.append(p[n])
        specs.append(full(p[n]))
    args.append(tri)
    specs.append(full(tri))
    tok = pl.BlockSpec((T, BW), lambda i: (i, 0))
    out_specs = tok if has_vmix else [tok, tok]
    tok_shape = jax.ShapeDtypeStruct((L, BW), F32)
    out_shape = tok_shape if has_vmix else [tok_shape, tok_shape]
    res = pl.pallas_call(
        functools.partial(_rwkv_kernel, has_vmix=has_vmix),
        grid=(L // T,),
        in_specs=specs,
        out_specs=out_specs,
        out_shape=out_shape,
        scratch_shapes=[pltpu.VMEM((N_PAIRS, LANES, LANES), F32)],
        compiler_params=pltpu.CompilerParams(
            dimension_semantics=("arbitrary",), vmem_limit_bytes=VMEM_LIMIT),
    )(*args)
    if has_vmix:
        return res, v_first
    return res[0], res[1]


def _mamba_kernel(xbc_ref, xbcp_ref, z_ref, misc_ref, cw_ref, cb_ref, dtb_ref, alog_ref, dfull_ref,
                  ng_ref, expand_ref, tri_ref, y_ref, st_ref):
    T = M2_T
    first = pl.program_id(0) == 0

    @pl.when(first)
    def _():
        st_ref[...] = jnp.zeros_like(st_ref)

    x = xbc_ref[...]
    halo = jnp.where(first, 0.0, xbcp_ref[...])
    row8 = lax.broadcasted_iota(jnp.int32, halo.shape, 0)
    acc = x * cw_ref[M2_CONV - 1:M2_CONV, :] + cb_ref[...]
    for j in range(1, M2_CONV):
        xr = pltpu.roll(x, j, axis=0)
        head = jnp.where(row8 < j, pltpu.roll(halo, j, axis=0), xr[:SUBLANES])
        xj = jnp.concatenate([head, xr[SUBLANES:]], axis=0)
        acc = acc + xj * cw_ref[M2_CONV - 1 - j:M2_CONV - j, :]
    xbc = _silu(acc)
    xm = xbc[:, :BW]
    bm = xbc[:, BW:BW + M2_GROUPS * M2_STATE]
    cm = xbc[:, BW + M2_GROUPS * M2_STATE:]

    dt = _softplus(misc_ref[...] + dtb_ref[...])
    da = dt * (-jnp.exp(alog_ref[...]))
    cs = _dot(tri_ref[...], da, HI)
    cs_t = cs.T
    expand = expand_ref[...]
    cs_full = _dot(cs, expand, HI)
    dt_full = _dot(dt, expand, HI)
    cs_last = cs_full[T - 1:T, :]
    xdt = xm * dt_full
    ecs = jnp.exp(cs_full)
    xdte = xdt * jnp.exp(cs_last - cs_full)
    chunk_decay = jnp.exp(cs_last)

    rr = lax.broadcasted_iota(jnp.int32, (T, T), 0)
    cc = lax.broadcasted_iota(jnp.int32, (T, T), 1)
    causal = cc <= rr
    lane = lax.broadcasted_iota(jnp.int32, (T, LANES), 1)
    head0 = lane < HEAD

    scores = []
    for gi in range(M2_GROUPS):
        gs = slice(gi * M2_STATE, (gi + 1) * M2_STATE)
        scores.append(_dot_nt(cm[:, gs], bm[:, gs], HI))
    ys = []
    for q in range(N_PAIRS):
        gi = (2 * q) // (N_HEADS // M2_GROUPS)
        gs = slice(gi * M2_STATE, (gi + 1) * M2_STATE)
        sl = slice(q * LANES, (q + 1) * LANES)
        yd = []
        for h in (2 * q, 2 * q + 1):
            seg = cs[:, h:h + 1] - cs_t[h:h + 1, :]
            dec = jnp.where(causal, jnp.exp(jnp.minimum(seg, 0.0)), 0.0)
            yd.append(_dot(scores[gi] * dec, xdt[:, sl], HI))
        st = st_ref[q]
        y_off = _dot(cm[:, gs], st, HI) * ecs[:, sl]
        st_ref[q] = st * chunk_decay[:, sl] + _dot_tn(bm[:, gs], xdte[:, sl], HI)
        ys.append(jnp.where(head0, yd[0], yd[1]) + y_off)
    y = jnp.concatenate(ys, axis=1) + xm * dfull_ref[...]
    y = y * _silu(z_ref[...])
    ms = jnp.mean(y * y, axis=-1, keepdims=True)
    y_ref[...] = y * lax.rsqrt(ms + NORM_EPS) * ng_ref[...]


def _mamba(proj, p):
    L = proj.shape[0]
    T = M2_T
    hb = T // SUBLANES
    tri = jnp.tril(jnp.ones((T, T), F32))
    consts = [p['conv_w'], p['conv_b'], p['dt_bias'], p['a_log'], p['d_full'], p['norm_g'], p['expand'], tri]
    specs = [
        pl.BlockSpec((T, M2_XBC), lambda i: (i, COL_XBC // M2_XBC)),
        pl.BlockSpec((SUBLANES, M2_XBC), lambda i: (jnp.maximum(i * hb - 1, 0), COL_XBC // M2_XBC)),
        pl.BlockSpec((T, BW), lambda i: (i, COL_Z // BW)),
        pl.BlockSpec((T, LANES), lambda i: (i, COL_MISC // LANES)),
    ] + [pl.BlockSpec(c.shape, lambda i: (0, 0)) for c in consts]
    return pl.pallas_call(
        _mamba_kernel,
        grid=(L // T,),
        in_specs=specs,
        out_specs=pl.BlockSpec((T, BW), lambda i: (i, 0)),
        out_shape=jax.ShapeDtypeStruct((L, BW), F32),
        scratch_shapes=[pltpu.VMEM((N_PAIRS, M2_STATE, LANES), F32)],
        compiler_params=pltpu.CompilerParams(
            dimension_semantics=("arbitrary",), vmem_limit_bytes=VMEM_LIMIT),
    )(proj, proj, proj, proj, *consts)


def _out_kernel(h_ref, ys5_ref, gs5_ref, yrw_ref, ym2_ref, wglu_ref, bglu_ref, wo_ref, fg_ref, o_ref, *, final):
    ys = _gelu_tanh(ys5_ref[...])
    ys = ys * _sigmoid(_dot(ys.astype(BF16), wglu_ref[...]) + bglu_ref[...])
    ys = ys * _silu(gs5_ref[...])
    acc = _dot(ys.astype(BF16), wo_ref[0:BW, :])
    acc = acc + _dot(yrw_ref[...].astype(BF16), wo_ref[BW:2 * BW, :])
    acc = acc + _dot(ym2_ref[...].astype(BF16), wo_ref[2 * BW:3 * BW, :])
    hn = h_ref[...] + acc
    if final:
        ms = jnp.mean(hn * hn, axis=-1, keepdims=True)
        hn = hn * lax.rsqrt(ms + NORM_EPS) * fg_ref[...]
    o_ref[...] = hn


def _out(h, ys5, proj, yrw, ym2, w_glu, b_glu, w_out, final_g, final):
    L = h.shape[0]
    TM = OUT_TM
    tok = pl.BlockSpec((TM, BW), lambda i: (i, 0))
    return pl.pallas_call(
        functools.partial(_out_kernel, final=final),
        grid=(L // TM,),
        in_specs=[
            pl.BlockSpec((TM, D_MODEL), lambda i: (i, 0)),
            tok,
            pl.BlockSpec((TM, BW), lambda i: (i, COL_GS5 // BW)),
            tok,
            tok,
            pl.BlockSpec((BW, BW), lambda i: (0, 0)),
            pl.BlockSpec((1, BW), lambda i: (0, 0)),
            pl.BlockSpec((3 * BW, D_MODEL), lambda i: (0, 0)),
            pl.BlockSpec((1, D_MODEL), lambda i: (0, 0)),
        ],
        out_specs=pl.BlockSpec((TM, D_MODEL), lambda i: (i, 0)),
        out_shape=jax.ShapeDtypeStruct((L, D_MODEL), F32),
        compiler_params=pltpu.CompilerParams(
            dimension_semantics=("parallel",), vmem_limit_bytes=VMEM_LIMIT),
    )(h, ys5, proj, yrw, ym2, w_glu, b_glu, w_out, final_g)


def _reorder_w_in(w, w_vmix):
    o = np.cumsum([0, BW, BW, BW, BW, BW, LORA_W, LORA_A, BW, M2_XBC, N_HEADS, BW])
    u, gs5, r, k, v, wd, ad, g, xbc, dt, z = (w[:, o[i]:o[i + 1]] for i in range(11))
    vm = w_vmix if w_vmix is not None else jnp.zeros((w.shape[0], LORA_V), w.dtype)
    pad = jnp.zeros((w.shape[0], LANES - N_HEADS - LORA_V), w.dtype)
    return jnp.concatenate([r, k, v, g, u, gs5, xbc, z, wd, ad, dt, vm, pad], axis=1).astype(BF16)


def _pad_rows(w, start, total=LANES):
    return jnp.pad(w.astype(F32), ((start, total - start - w.shape[0]), (0, 0)))


def _row(a):
    return a.astype(F32).reshape(1, -1)


def kernel(x, norm_g, w_in, w_in_vmix, s5_log_dt, s5_a_re, s5_a_im, s5_b_re, s5_b_im, s5_c_re, s5_c_im, s5_d, s5_w_glu, s5_b_glu, rwkv_mu, rwkv_w_up, rwkv_w0, rwkv_a_up, rwkv_a0, rwkv_k_k, rwkv_k_a, rwkv_r_k, rwkv_ln_g, rwkv_ln_b, rwkv_vmix_mu, rwkv_v_up, rwkv_v0, m2_conv_w, m2_conv_b, m2_dt_bias, m2_a_log, m2_d, m2_norm_g, w_out, final_norm_g):
    bsz, L, d = x.shape
    assert bsz == 1 and d == D_MODEL
    assert L % max(PROJ_TM, OUT_TM, RW_T, M2_T, S5_T * SUBLANES) == 0
    depth = w_in.shape[0]
    h = x.reshape(L, d).astype(F32)
    expand = jnp.pad(jnp.repeat(jnp.eye(N_HEADS, dtype=F32), HEAD, axis=1), ((0, LANES - N_HEADS), (0, 0)))
    v_first = None
    for i in range(depth):
        w_cat = _reorder_w_in(w_in[i], w_in_vmix[i - 1] if i > 0 else None)
        proj = _proj(h, _row(norm_g[i]), w_cat)

        tables = _s5_tables(s5_log_dt[i], s5_a_re[i], s5_a_im[i], s5_b_re[i], s5_b_im[i],
                            s5_c_re[i], s5_c_im[i], s5_d[i], L // S5_T)
        y_s5 = _s5(proj[:, COL_U:COL_U + BW], tables)

        mu = rwkv_mu[i].astype(F32)
        mo = np.cumsum([0, BW, BW, BW, LORA_W, LORA_A, BW])
        mr, mk, mv, mwd, mad, mg = (mu[mo[j]:mo[j + 1]] for j in range(6))
        rp = {
            'mu_rw': jnp.concatenate([mr, mk, mv, mg]).reshape(1, -1),
            'mu_wa': jnp.concatenate([mwd, mad]).reshape(1, -1),
            'w_up': _pad_rows(rwkv_w_up[i], 0), 'w0': _row(rwkv_w0[i]),
            'a_up': _pad_rows(rwkv_a_up[i], LORA_W), 'a0': _row(rwkv_a0[i]),
            'k_k': _row(rwkv_k_k[i]), 'k_a': _row(rwkv_k_a[i]), 'r_k': _row(rwkv_r_k[i]),
            'ln_g': _row(rwkv_ln_g[i]), 'ln_b': _row(rwkv_ln_b[i]),
        }
        if i > 0:
            rp['mu_misc'] = jnp.pad(rwkv_vmix_mu[i - 1].astype(F32),
                                    (N_HEADS, LANES - N_HEADS - LORA_V)).reshape(1, -1)
            rp['v_up'] = _pad_rows(rwkv_v_up[i - 1], N_HEADS)
            rp['v0'] = _row(rwkv_v0[i - 1])
        y_rw, v_first = _rwkv(proj, rp, v_first)

        mp = {
            'conv_w': m2_conv_w[i].astype(F32), 'conv_b': _row(m2_conv_b[i]),
            'dt_bias': jnp.pad(m2_dt_bias[i].astype(F32), (0, LANES - N_HEADS)).reshape(1, -1),
            'a_log': jnp.pad(m2_a_log[i].astype(F32), (0, LANES - N_HEADS)).reshape(1, -1),
            'd_full': jnp.repeat(m2_d[i].astype(F32), HEAD).reshape(1, -1),
            'norm_g': _row(m2_norm_g[i]), 'expand': expand,
        }
        y_m2 = _mamba(proj, mp)

        h = _out(h, y_s5, proj, y_rw, y_m2, s5_w_glu[i].astype(BF16), _row(s5_b_glu[i]),
                 w_out[i].astype(BF16), _row(final_norm_g), final=(i == depth - 1))
    return h.reshape(bsz, L, d).astype(x.dtype)
```

```python
import functools
import math

import jax
import jax.numpy as jnp
import numpy as np
from jax import lax
from jax.experimental import pallas as pl
from jax.experimental.pallas import tpu as pltpu

F32 = jnp.float32
BF16 = jnp.bfloat16
HI = lax.Precision.HIGHEST

D_MODEL = 1024
BW = 512
S5_GROUP = 16
S5_GROUPS = 32
S5_STATE = 64
HEAD = 64
N_HEADS = 8
N_PAIRS = N_HEADS // 2
LORA_W = 64
LORA_A = 64
LORA_V = 32
GN_EPS = 64e-5
M2_GROUPS = 2
M2_STATE = 128
M2_CONV = 4
M2_XBC = 1024
NORM_EPS = 1e-5

LANES = 128
SUBLANES = 8

COL_RW = 0
COL_U = 2048
COL_GS5 = 2560
COL_XBC = 3072
COL_Z = 4096
COL_WA = 4608
COL_MISC = 4736
PROJ_PAD = 4864

PROJ_TM = 256
PROJ_TN = 512
S5_T = 32
RW_T = 64
RW_NB = 4
RW_GC = 2
M2_T = 128
OUT_TM = 256
VMEM_LIMIT = 48 * 1024 * 1024


def _dot(a, b, prec=None):
    return jnp.dot(a, b, precision=prec, preferred_element_type=F32)


def _dot_nt(a, b, prec=None):
    return lax.dot_general(a, b, (((1,), (1,)), ((), ())), precision=prec, preferred_element_type=F32)


def _dot_tn(a, b, prec=None):
    return lax.dot_general(a, b, (((0,), (0,)), ((), ())), precision=prec, preferred_element_type=F32)


def _split3(x):
    hi = x.astype(BF16)
    r1 = x - hi.astype(F32)
    mid = r1.astype(BF16)
    lo = (r1 - mid.astype(F32)).astype(BF16)
    return hi, mid, lo


def _dot_split_rhs(m, x):
    hi, mid, lo = _split3(x)
    return _dot(m, hi) + _dot(m, mid) + _dot(m, lo)


def _dot_split_lhs(x, m):
    hi, mid, lo = _split3(x)
    return _dot(hi, m) + _dot(mid, m) + _dot(lo, m)


def _sigmoid(x):
    return 1.0 / (1.0 + jnp.exp(-x))


def _silu(x):
    return x * _sigmoid(x)


def _softplus(x):
    return jnp.maximum(x, 0.0) + jnp.log1p(jnp.exp(-jnp.abs(x)))


def _gelu_tanh(x):
    c = math.sqrt(2.0 / math.pi)
    return 0.5 * x * (1.0 + jnp.tanh(c * (x + 0.044715 * (x * x * x))))


def _prev_rows(x, halo, is_first):
    rolled = pltpu.roll(x, 1, axis=0)
    last = jnp.where(is_first, 0.0, halo[SUBLANES - 1:SUBLANES, :])
    row = lax.broadcasted_iota(jnp.int32, x.shape, 0)
    return jnp.where(row == 0, last, rolled)


def _proj_kernel(h_ref, g_ref, w_ref, o_ref):
    x = h_ref[...]
    ms = jnp.mean(x * x, axis=-1, keepdims=True)
    hn = (x * lax.rsqrt(ms + NORM_EPS) * g_ref[...]).astype(BF16)
    for j in range(0, PROJ_PAD, PROJ_TN):
        w = min(PROJ_TN, PROJ_PAD - j)
        o_ref[:, j:j + w] = _dot(hn, w_ref[:, j:j + w])


def _proj(h, g, w):
    L = h.shape[0]
    return pl.pallas_call(
        _proj_kernel,
        grid=(L // PROJ_TM,),
        in_specs=[
            pl.BlockSpec((PROJ_TM, D_MODEL), lambda i: (i, 0)),
            pl.BlockSpec((1, D_MODEL), lambda i: (0, 0)),
            pl.BlockSpec((D_MODEL, PROJ_PAD), lambda i: (0, 0)),
        ],
        out_specs=pl.BlockSpec((PROJ_TM, PROJ_PAD), lambda i: (i, 0)),
        out_shape=jax.ShapeDtypeStruct((L, PROJ_PAD), F32),
        compiler_params=pltpu.CompilerParams(
            dimension_semantics=("parallel",), vmem_limit_bytes=VMEM_LIMIT),
    )(h, g, w)


def _s5_kernel(u_ref, k_ref, e_ref, cm_ref, a1_ref, a2_ref, y_ref, *, nsteps):
    ub = u_ref[0].astype(BF16)
    x = _dot(ub, e_ref[0])
    row = lax.broadcasted_iota(jnp.int32, x.shape, 0)
    for s in range(nsteps):
        d = 1 << s
        sh = jnp.where(row >= d, pltpu.roll(x, d, axis=0), 0.0)
        x = x + a1_ref[0, s:s + 1, :] * sh + a2_ref[0, s:s + 1, :] * pltpu.roll(sh, S5_STATE, axis=1)
    sprev = jnp.where(row >= 1, pltpu.roll(x, 1, axis=0), 0.0)
    y_ref[0] = _dot(ub, k_ref[0]) + _dot(sprev.astype(BF16), cm_ref[0])


def _s5_tables(log_dt, a_re, a_im, b_re, b_im, c_re, c_im, d_skip, n_chunks):
    T = S5_T
    G, P = a_re.shape
    dt = jnp.exp(log_dt.astype(F32))[:, None]
    ar, ai = a_re.astype(F32), a_im.astype(F32)
    mag = jnp.exp(dt * ar)
    abar_re, abar_im = mag * jnp.cos(dt * ai), mag * jnp.sin(dt * ai)
    den = ar * ar + ai * ai
    f_re = ((abar_re - 1.0) * ar + abar_im * ai) / den
    f_im = (abar_im * ar - (abar_re - 1.0) * ai) / den
    br, bi = b_re.astype(F32), b_im.astype(F32)
    bb_re = f_re[..., None] * br - f_im[..., None] * bi
    bb_im = f_re[..., None] * bi + f_im[..., None] * br

    def powers(tau):
        tau = tau.astype(F32)[None, :, None]
        m = jnp.exp(tau * (dt * ar)[:, None, :])
        ang = tau * (dt * ai)[:, None, :]
        return m * jnp.cos(ang), m * jnp.sin(ang)

    pw_re, pw_im = powers(jnp.arange(T + 1))
    cr, ci = c_re.astype(F32), c_im.astype(F32)
    cp_re = cr[:, None] * pw_re[:, :, None, :] - ci[:, None] * pw_im[:, :, None, :]
    cp_im = cr[:, None] * pw_im[:, :, None, :] + ci[:, None] * pw_re[:, :, None, :]
    k_lag = (jnp.einsum('gtop,gpi->gtio', cp_re[:, :T], bb_re, precision=HI)
             - jnp.einsum('gtop,gpi->gtio', cp_im[:, :T], bb_im, precision=HI))
    eye = jnp.eye(S5_GROUP, dtype=F32)
    k_lag = k_lag.at[:, 0].add(d_skip.astype(F32)[:, :, None] * eye)
    idx = jnp.arange(T)[None, :] - jnp.arange(T)[:, None]
    toe = jnp.where((idx >= 0)[None, :, :, None, None], k_lag[:, jnp.clip(idx, 0, T - 1)], 0.0)
    toe = toe.transpose(0, 1, 3, 2, 4).reshape(G, T * S5_GROUP, T * S5_GROUP)
    rev_re, rev_im = pw_re[:, T - 1::-1][:, :T], pw_im[:, T - 1::-1][:, :T]
    bbt_re, bbt_im = bb_re.transpose(0, 2, 1), bb_im.transpose(0, 2, 1)
    e_re = rev_re[:, :, None, :] * bbt_re[:, None] - rev_im[:, :, None, :] * bbt_im[:, None]
    e_im = rev_re[:, :, None, :] * bbt_im[:, None] + rev_im[:, :, None, :] * bbt_re[:, None]
    e_map = jnp.concatenate([e_re, e_im], axis=-1).reshape(G, T * S5_GROUP, 2 * P)
    cm_re = cp_re[:, 1:].transpose(0, 3, 1, 2)
    cm_im = -cp_im[:, 1:].transpose(0, 3, 1, 2)
    c_map = jnp.concatenate([cm_re, cm_im], axis=1).reshape(G, 2 * P, T * S5_GROUP)
    nsteps = max(1, int(math.ceil(math.log2(n_chunks))))
    sp_re, sp_im = powers(T * (2 ** jnp.arange(nsteps)))
    a1 = jnp.concatenate([sp_re, sp_re], axis=-1)
    a2 = jnp.concatenate([-sp_im, sp_im], axis=-1)
    pad = (-nsteps) % SUBLANES
    a1 = jnp.pad(a1, ((0, 0), (0, pad), (0, 0)))
    a2 = jnp.pad(a2, ((0, 0), (0, pad), (0, 0)))
    return toe.astype(BF16), e_map.astype(BF16), c_map.astype(BF16), a1, a2, nsteps


def _s5(u, tables):
    toe, e_map, c_map, a1, a2, nsteps = tables
    L = u.shape[0]
    T, G, W = S5_T, S5_GROUPS, S5_T * S5_GROUP
    C = L // T
    ug = u.reshape(C, T, G, S5_GROUP).transpose(2, 0, 1, 3).reshape(G, C, W)
    yg = pl.pallas_call(
        functools.partial(_s5_kernel, nsteps=nsteps),
        grid=(G,),
        in_specs=[
            pl.BlockSpec((1, C, W), lambda g: (g, 0, 0)),
            pl.BlockSpec((1, W, W), lambda g: (g, 0, 0)),
            pl.BlockSpec((1, W, 2 * S5_STATE), lambda g: (g, 0, 0)),
            pl.BlockSpec((1, 2 * S5_STATE, W), lambda g: (g, 0, 0)),
            pl.BlockSpec((1, a1.shape[1], 2 * S5_STATE), lambda g: (g, 0, 0)),
            pl.BlockSpec((1, a2.shape[1], 2 * S5_STATE), lambda g: (g, 0, 0)),
        ],
        out_specs=pl.BlockSpec((1, C, W), lambda g: (g, 0, 0)),
        out_shape=jax.ShapeDtypeStruct((G, C, W), F32),
        compiler_params=pltpu.CompilerParams(
            dimension_semantics=("parallel",), vmem_limit_bytes=VMEM_LIMIT),
    )(ug, toe, e_map, c_map, a1, a2)
    return yg.reshape(G, C, T, S5_GROUP).transpose(1, 2, 0, 3).reshape(L, BW)


def _rwkv_kernel(*refs, has_vmix):
    if has_vmix:
        (rw_ref, rwp_ref, wa_ref, wap_ref, misc_ref, miscp_ref, vfirst_ref,
         mu_rw_ref, mu_wa_ref, mu_misc_ref, wup_ref, w0_ref, aup_ref, a0_ref, kk_ref, ka_ref, rk_ref,
         lng_ref, lnb_ref, vup_ref, v0_ref, tri_ref, y_ref, s_ref) = refs
    else:
        (rw_ref, rwp_ref, wa_ref, wap_ref,
         mu_rw_ref, mu_wa_ref, wup_ref, w0_ref, aup_ref, a0_ref, kk_ref, ka_ref, rk_ref,
         lng_ref, lnb_ref, tri_ref, y_ref, vout_ref, s_ref) = refs
    T = RW_T
    TB = RW_NB * T
    first = pl.program_id(0) == 0

    @pl.when(first)
    def _():
        s_ref[...] = jnp.zeros_like(s_ref)

    def lerp(ref, halo_ref, mu_ref):
        x = ref[...]
        return x + (_prev_rows(x, halo_ref[...], first) - x) * mu_ref[...]

    xs = lerp(rw_ref, rwp_ref, mu_rw_ref)
    r, k, v, g = (xs[:, i * BW:(i + 1) * BW] for i in range(4))
    wa = lerp(wa_ref, wap_ref, mu_wa_ref)
    w_log = -_softplus(-(w0_ref[...] + _dot(jnp.tanh(wa).astype(BF16), wup_ref[...]))) - 0.5
    lw = -jnp.exp(w_log)
    a = _sigmoid(a0_ref[...] + _dot(wa.astype(BF16), aup_ref[...]))
    if has_vmix:
        vd = lerp(misc_ref, miscp_ref, mu_misc_ref)
        mix = _sigmoid(v0_ref[...] + _dot(vd.astype(BF16), vup_ref[...]))
        v = v + (vfirst_ref[...] - v) * mix
    else:
        vout_ref[...] = v
    kk = k * kk_ref[...]
    kmod = k * (1.0 + (a - 1.0) * ka_ref[...])
    cums = _dot_split_rhs(tri_ref[...], lw)
    cl, rv = cums[:TB], cums[TB:]
    g_incl = jnp.exp(cl)
    g_inv = jnp.exp(-cl)
    g_end = jnp.exp(rv)
    kka = kk * a
    pa = kk * jnp.exp(cl - lw)
    pb = kka * g_inv
    pbe = kka * g_end
    pk = kmod * g_inv
    pke = kmod * g_end
    pr = r * g_incl
    rkr = r * kmod * rk_ref[...]
    gated = _silu(g)

    lane = lax.broadcasted_iota(jnp.int32, (T, LANES), 1)
    head0 = lane < HEAD
    rr = lax.broadcasted_iota(jnp.int32, (2 * T, 2 * T), 0)
    cc = lax.broadcasted_iota(jnp.int32, (2 * T, 2 * T), 1)
    strict = cc < rr
    incl = cc <= rr
    eye = (cc == rr).astype(F32)
    srow = lax.broadcasted_iota(jnp.int32, (2 * T, LANES), 0)
    slane = lax.broadcasted_iota(jnp.int32, (2 * T, LANES), 1)
    own = (srow < T) == (slane < HEAD)

    def stack(x):
        return jnp.concatenate([jnp.where(head0, x, 0.0), jnp.where(head0, 0.0, x)], axis=0)

    def b16(x):
        return x.astype(BF16)

    def rows(c):
        return slice(c * T, (c + 1) * T)

    def lanes(q):
        return slice(q * LANES, (q + 1) * LANES)

    state = [s_ref[q] for q in range(N_PAIRS)]
    for c0 in range(0, RW_NB, RW_GC):
        units = [(c, q) for c in range(c0, c0 + RW_GC) for q in range(N_PAIRS)]
        inv = []
        for c, q in units:
            kk_h = stack(kk[rows(c), lanes(q)])
            nrm = jnp.sqrt(jnp.sum(kk_h * kk_h, axis=-1, keepdims=True))
            inv.append(1.0 / jnp.maximum(nrm, 1e-12))
        a_t = [b16(-stack(pa[rows(c), lanes(q)]) * i) for (c, q), i in zip(units, inv)]
        b_t = [b16(stack(pb[rows(c), lanes(q)]) * i) for (c, q), i in zip(units, inv)]
        be_t = [b16(stack(pbe[rows(c), lanes(q)]) * i) for (c, q), i in zip(units, inv)]
        k_t = [b16(stack(pk[rows(c), lanes(q)])) for c, q in units]
        ke_t = [b16(stack(pke[rows(c), lanes(q)])) for c, q in units]
        r_s = [stack(pr[rows(c), lanes(q)]) for c, q in units]
        v_s = [stack(v[rows(c), lanes(q)]) for c, q in units]
        v_t = [b16(x) for x in v_s]
        aa = [_dot_nt(jnp.concatenate([a, b16(r)], axis=0), jnp.concatenate([b, k_], axis=0))
              for a, r, b, k_ in zip(a_t, r_s, b_t, k_t)]
        a_ak = [b16(jnp.where(strict, x[:2 * T, 2 * T:], 0.0)) for x in aa]
        a_rb = [b16(jnp.where(incl, x[2 * T:, :2 * T], 0.0)) for x in aa]
        a_rk = [b16(jnp.where(incl, x[2 * T:, 2 * T:], 0.0)) for x in aa]
        p = [jnp.where(strict, x[:2 * T, :2 * T], 0.0) for x in aa]
        minv = [eye + x for x in p]
        for _ in range(int(math.log2(T)) - 1):
            p_b = [b16(x) for x in p]
            p = [_dot(x, x) for x in p_b]
            minv = [m + _dot(b16(x), b16(m)) for x, m in zip(p, minv)]
        akv = [_dot(x, y) for x, y in zip(a_ak, v_t)]
        wu_b = [b16(_dot(b16(m), jnp.concatenate([a, b16(x)], axis=1)))
                for m, a, x in zip(minv, a_t, akv)]
        p_mat = [b16(_dot_tn(w[:, :LANES], be)) for w, be in zip(wu_b, be_t)]
        z_mat = [_dot_tn(jnp.concatenate([w[:, LANES:], vt], axis=0), jnp.concatenate([be, ke], axis=0))
                 for w, vt, be, ke in zip(wu_b, v_t, be_t, ke_t)]
        qy = [_dot(x, w) for x, w in zip(a_rb, wu_b)]
        q_mat = [b16(r + x[:, :LANES]) for r, x in zip(r_s, qy)]
        y0 = [x[:, LANES:] + _dot(ark, vt) for x, ark, vt in zip(qy, a_rk, v_t)]
        yh = []
        for i, (c, q) in enumerate(units):
            s = state[q]
            s_b = b16(s)
            yh.append(_dot_nt(q_mat[i], s_b) + y0[i])
            state[q] = s * g_incl[(c + 1) * T - 1:(c + 1) * T, lanes(q)] + _dot(s_b, p_mat[i]) + z_mat[i]
        for i, (c, q) in enumerate(units):
            mean = jnp.sum(yh[i], axis=-1, keepdims=True) * (1.0 / HEAD)
            cen = jnp.where(own, yh[i] - mean, 0.0)
            var = jnp.sum(cen * cen, axis=-1, keepdims=True) * (1.0 / HEAD)
            yn = cen * lax.rsqrt(var + GN_EPS)
            bonus = jnp.sum(stack(rkr[rows(c), lanes(q)]), axis=-1, keepdims=True) * v_s[i]
            yn = yn[:T] + yn[T:]
            bonus = bonus[:T] + bonus[T:]
            y_ref[rows(c), lanes(q)] = ((yn * lng_ref[:, lanes(q)] + lnb_ref[:, lanes(q)] + bonus)
                                        * gated[rows(c), lanes(q)])
    for q in range(N_PAIRS):
        s_ref[q] = state[q]


def _rwkv(proj, p, v_first):
    L = proj.shape[0]
    T = RW_T * RW_NB
    has_vmix = v_first is not None
    hb = T // SUBLANES

    def cur(width, col):
        return pl.BlockSpec((T, width), lambda i: (i, col // width))

    def halo(width, col):
        return pl.BlockSpec((SUBLANES, width), lambda i: (jnp.maximum(i * hb - 1, 0), col // width))

    def row(width):
        return pl.BlockSpec((1, width), lambda i: (0, 0))

    def full(a):
        return pl.BlockSpec(a.shape, lambda i: (0, 0))

    pos = np.arange(T)
    same = (pos[:, None] // RW_T) == (pos[None, :] // RW_T)
    tri = jnp.asarray(np.concatenate([same & (pos[None, :] <= pos[:, None]),
                                      same & (pos[None, :] > pos[:, None])], axis=0), BF16)
    args = [proj, proj, proj, proj]
    specs = [cur(4 * BW, COL_RW), halo(4 * BW, COL_RW), cur(LANES, COL_WA), halo(LANES, COL_WA)]
    if has_vmix:
        args += [proj, proj, v_first]
        specs += [cur(LANES, COL_MISC), halo(LANES, COL_MISC), pl.BlockSpec((T, BW), lambda i: (i, 0))]
        names = ['mu_rw', 'mu_wa', 'mu_misc', 'w_up', 'w0', 'a_up', 'a0', 'k_k', 'k_a', 'r_k',
                 'ln_g', 'ln_b', 'v_up', 'v0']
    else:
        names = ['mu_rw', 'mu_wa', 'w_up', 'w0', 'a_up', 'a0', 'k_k', 'k_a', 'r_k', 'ln_g', 'ln_b']
    for n in names:
        args.append(p[n])
        specs.append(full(p[n]))
    args.append(tri)
    specs.append(full(tri))
    tok = pl.BlockSpec((T, BW), lambda i: (i, 0))
    out_specs = tok if has_vmix else [tok, tok]
    tok_shape = jax.ShapeDtypeStruct((L, BW), F32)
    out_shape = tok_shape if has_vmix else [tok_shape, tok_shape]
    res = pl.pallas_call(
        functools.partial(_rwkv_kernel, has_vmix=has_vmix),
        grid=(L // T,),
        in_specs=specs,
        out_specs=out_specs,
        out_shape=out_shape,
        scratch_shapes=[pltpu.VMEM((N_PAIRS, LANES, LANES), F32)],
        compiler_params=pltpu.CompilerParams(
            dimension_semantics=("arbitrary",), vmem_limit_bytes=VMEM_LIMIT),
    )(*args)
    if has_vmix:
        return res, v_first
    return res[0], res[1]


def _mamba_kernel(xbc_ref, xbcp_ref, z_ref, misc_ref, cw_ref, cb_ref, dtb_ref, alog_ref, dfull_ref,
                  ng_ref, expand_ref, tri_ref, y_ref, st_ref):
    T = M2_T
    first = pl.program_id(0) == 0

    @pl.when(first)
    def _():
        st_ref[...] = jnp.zeros_like(st_ref)

    x = xbc_ref[...]
    halo = jnp.where(first, 0.0, xbcp_ref[...])
    row8 = lax.broadcasted_iota(jnp.int32, halo.shape, 0)
    acc = x * cw_ref[M2_CONV - 1:M2_CONV, :] + cb_ref[...]
    for j in range(1, M2_CONV):
        xr = pltpu.roll(x, j, axis=0)
        head = jnp.where(row8 < j, pltpu.roll(halo, j, axis=0), xr[:SUBLANES])
        xj = jnp.concatenate([head, xr[SUBLANES:]], axis=0)
        acc = acc + xj * cw_ref[M2_CONV - 1 - j:M2_CONV - j, :]
    xbc = _silu(acc)
    xm = xbc[:, :BW]
    bm = xbc[:, BW:BW + M2_GROUPS * M2_STATE]
    cm = xbc[:, BW + M2_GROUPS * M2_STATE:]
    bm_b, cm_b = bm.astype(BF16), cm.astype(BF16)

    dt = _softplus(misc_ref[...] + dtb_ref[...])
    da = dt * (-jnp.exp(alog_ref[...]))
    cs = _dot_split_rhs(tri_ref[...], da)
    cs_t = cs.T
    expand = expand_ref[...]
    cs_full = _dot_split_lhs(cs, expand)
    dt_full = _dot_split_lhs(dt, expand)
    cs_last = cs_full[T - 1:T, :]
    xdt = xm * dt_full
    ecs = jnp.exp(cs_full)
    xdte_b = (xdt * jnp.exp(cs_last - cs_full)).astype(BF16)
    xdt_b = xdt.astype(BF16)
    chunk_decay = jnp.exp(cs_last)

    rr = lax.broadcasted_iota(jnp.int32, (T, T), 0)
    cc = lax.broadcasted_iota(jnp.int32, (T, T), 1)
    causal = cc <= rr
    lane = lax.broadcasted_iota(jnp.int32, (T, LANES), 1)
    head0 = lane < HEAD

    scores = []
    for gi in range(M2_GROUPS):
        gs = slice(gi * M2_STATE, (gi + 1) * M2_STATE)
        scores.append(_dot_nt(cm_b[:, gs], bm_b[:, gs]))
    ys = []
    for q in range(N_PAIRS):
        gi = (2 * q) // (N_HEADS // M2_GROUPS)
        gs = slice(gi * M2_STATE, (gi + 1) * M2_STATE)
        sl = slice(q * LANES, (q + 1) * LANES)
        yd = []
        for h in (2 * q, 2 * q + 1):
            seg = cs[:, h:h + 1] - cs_t[h:h + 1, :]
            dec = jnp.where(causal, jnp.exp(jnp.minimum(seg, 0.0)), 0.0)
            yd.append(_dot((scores[gi] * dec).astype(BF16), xdt_b[:, sl]))
        st = st_ref[q]
        y_off = _dot(cm_b[:, gs], st.astype(BF16)) * ecs[:, sl]
        st_ref[q] = st * chunk_decay[:, sl] + _dot_tn(bm_b[:, gs], xdte_b[:, sl])
        ys.append(jnp.where(head0, yd[0], yd[1]) + y_off)
    y = jnp.concatenate(ys, axis=1) + xm * dfull_ref[...]
    y = y * _silu(z_ref[...])
    ms = jnp.mean(y * y, axis=-1, keepdims=True)
    y_ref[...] = y * lax.rsqrt(ms + NORM_EPS) * ng_ref[...]


def _mamba(proj, p):
    L = proj.shape[0]
    T = M2_T
    hb = T // SUBLANES
    tri = jnp.tril(jnp.ones((T, T), BF16))
    consts = [p['conv_w'], p['conv_b'], p['dt_bias'], p['a_log'], p['d_full'], p['norm_g'], p['expand'], tri]
    specs = [
        pl.BlockSpec((T, M2_XBC), lambda i: (i, COL_XBC // M2_XBC)),
        pl.BlockSpec((SUBLANES, M2_XBC), lambda i: (jnp.maximum(i * hb - 1, 0), COL_XBC // M2_XBC)),
        pl.BlockSpec((T, BW), lambda i: (i, COL_Z // BW)),
        pl.BlockSpec((T, LANES), lambda i: (i, COL_MISC // LANES)),
    ] + [pl.BlockSpec(c.shape, lambda i: (0, 0)) for c in consts]
    return pl.pallas_call(
        _mamba_kernel,
        grid=(L // T,),
        in_specs=specs,
        out_specs=pl.BlockSpec((T, BW), lambda i: (i, 0)),
        out_shape=jax.ShapeDtypeStruct((L, BW), F32),
        scratch_shapes=[pltpu.VMEM((N_PAIRS, M2_STATE, LANES), F32)],
        compiler_params=pltpu.CompilerParams(
            dimension_semantics=("arbitrary",), vmem_limit_bytes=VMEM_LIMIT),
    )(proj, proj, proj, proj, *consts)


def _out_kernel(h_ref, ys5_ref, gs5_ref, yrw_ref, ym2_ref, wglu_ref, bglu_ref, wo_ref, fg_ref, o_ref, *, final):
    ys = _gelu_tanh(ys5_ref[...])
    ys = ys * _sigmoid(_dot(ys.astype(BF16), wglu_ref[...]) + bglu_ref[...])
    ys = ys * _silu(gs5_ref[...])
    acc = _dot(ys.astype(BF16), wo_ref[0:BW, :])
    acc = acc + _dot(yrw_ref[...].astype(BF16), wo_ref[BW:2 * BW, :])
    acc = acc + _dot(ym2_ref[...].astype(BF16), wo_ref[2 * BW:3 * BW, :])
    hn = h_ref[...] + acc
    if final:
        ms = jnp.mean(hn * hn, axis=-1, keepdims=True)
        hn = hn * lax.rsqrt(ms + NORM_EPS) * fg_ref[...]
    o_ref[...] = hn


def _out(h, ys5, proj, yrw, ym2, w_glu, b_glu, w_out, final_g, final):
    L = h.shape[0]
    TM = OUT_TM
    tok = pl.BlockSpec((TM, BW), lambda i: (i, 0))
    return pl.pallas_call(
        functools.partial(_out_kernel, final=final),
        grid=(L // TM,),
        in_specs=[
            pl.BlockSpec((TM, D_MODEL), lambda i: (i, 0)),
            tok,
            pl.BlockSpec((TM, BW), lambda i: (i, COL_GS5 // BW)),
            tok,
            tok,
            pl.BlockSpec((BW, BW), lambda i: (0, 0)),
            pl.BlockSpec((1, BW), lambda i: (0, 0)),
            pl.BlockSpec((3 * BW, D_MODEL), lambda i: (0, 0)),
            pl.BlockSpec((1, D_MODEL), lambda i: (0, 0)),
        ],
        out_specs=pl.BlockSpec((TM, D_MODEL), lambda i: (i, 0)),
        out_shape=jax.ShapeDtypeStruct((L, D_MODEL), F32),
        compiler_params=pltpu.CompilerParams(
            dimension_semantics=("parallel",), vmem_limit_bytes=VMEM_LIMIT),
    )(h, ys5, proj, yrw, ym2, w_glu, b_glu, w_out, final_g)


def _reorder_w_in(w, w_vmix):
    o = np.cumsum([0, BW, BW, BW, BW, BW, LORA_W, LORA_A, BW, M2_XBC, N_HEADS, BW])
    u, gs5, r, k, v, wd, ad, g, xbc, dt, z = (w[:, o[i]:o[i + 1]] for i in range(11))
    vm = w_vmix if w_vmix is not None else jnp.zeros((w.shape[0], LORA_V), w.dtype)
    pad = jnp.zeros((w.shape[0], LANES - N_HEADS - LORA_V), w.dtype)
    return jnp.concatenate([r, k, v, g, u, gs5, xbc, z, wd, ad, dt, vm, pad], axis=1).astype(BF16)


def _pad_rows(w, start, total=LANES):
    return jnp.pad(w.astype(F32), ((start, total - start - w.shape[0]), (0, 0)))


def _row(a):
    return a.astype(F32).reshape(1, -1)


def kernel(x, norm_g, w_in, w_in_vmix, s5_log_dt, s5_a_re, s5_a_im, s5_b_re, s5_b_im, s5_c_re, s5_c_im, s5_d, s5_w_glu, s5_b_glu, rwkv_mu, rwkv_w_up, rwkv_w0, rwkv_a_up, rwkv_a0, rwkv_k_k, rwkv_k_a, rwkv_r_k, rwkv_ln_g, rwkv_ln_b, rwkv_vmix_mu, rwkv_v_up, rwkv_v0, m2_conv_w, m2_conv_b, m2_dt_bias, m2_a_log, m2_d, m2_norm_g, w_out, final_norm_g):
    bsz, L, d = x.shape
    assert bsz == 1 and d == D_MODEL
    assert L % max(PROJ_TM, OUT_TM, RW_T * RW_NB, M2_T, S5_T * SUBLANES) == 0
    depth = w_in.shape[0]
    h = x.reshape(L, d).astype(F32)
    expand = jnp.pad(jnp.repeat(jnp.eye(N_HEADS, dtype=F32), HEAD, axis=1), ((0, LANES - N_HEADS), (0, 0)))
    v_first = None
    for i in range(depth):
        w_cat = _reorder_w_in(w_in[i], w_in_vmix[i - 1] if i > 0 else None)
        proj = _proj(h, _row(norm_g[i]), w_cat)

        tables = _s5_tables(s5_log_dt[i], s5_a_re[i], s5_a_im[i], s5_b_re[i], s5_b_im[i],
                            s5_c_re[i], s5_c_im[i], s5_d[i], L // S5_T)
        y_s5 = _s5(proj[:, COL_U:COL_U + BW], tables)

        mu = rwkv_mu[i].astype(F32)
        mo = np.cumsum([0, BW, BW, BW, LORA_W, LORA_A, BW])
        mr, mk, mv, mwd, mad, mg = (mu[mo[j]:mo[j + 1]] for j in range(6))
        rp = {
            'mu_rw': jnp.concatenate([mr, mk, mv, mg]).reshape(1, -1),
            'mu_wa': jnp.concatenate([mwd, mad]).reshape(1, -1),
            'w_up': _pad_rows(rwkv_w_up[i], 0).astype(BF16), 'w0': _row(rwkv_w0[i]),
            'a_up': _pad_rows(rwkv_a_up[i], LORA_W).astype(BF16), 'a0': _row(rwkv_a0[i]),
            'k_k': _row(rwkv_k_k[i]), 'k_a': _row(rwkv_k_a[i]), 'r_k': _row(rwkv_r_k[i]),
            'ln_g': _row(rwkv_ln_g[i]), 'ln_b': _row(rwkv_ln_b[i]),
        }
        if i > 0:
            rp['mu_misc'] = jnp.pad(rwkv_vmix_mu[i - 1].astype(F32),
                                    (N_HEADS, LANES - N_HEADS - LORA_V)).reshape(1, -1)
            rp['v_up'] = _pad_rows(rwkv_v_up[i - 1], N_HEADS).astype(BF16)
            rp['v0'] = _row(rwkv_v0[i - 1])
        y_rw, v_first = _rwkv(proj, rp, v_first)

        mp = {
            'conv_w': m2_conv_w[i].astype(F32), 'conv_b': _row(m2_conv_b[i]),
            'dt_bias': jnp.pad(m2_dt_bias[i].astype(F32), (0, LANES - N_HEADS)).reshape(1, -1),
            'a_log': jnp.pad(m2_a_log[i].astype(F32), (0, LANES - N_HEADS)).reshape(1, -1),
            'd_full': jnp.repeat(m2_d[i].astype(F32), HEAD).reshape(1, -1),
            'norm_g': _row(m2_norm_g[i]), 'expand': expand.astype(BF16),
        }
        y_m2 = _mamba(proj, mp)

        h = _out(h, y_s5, proj, y_rw, y_m2, s5_w_glu[i].astype(BF16), _row(s5_b_glu[i]),
                 w_out[i].astype(BF16), _row(final_norm_g), final=(i == depth - 1))
    return h.reshape(bsz, L, d).astype(x.dtype)
```

```python
import functools
import math

import jax
import jax.numpy as jnp
import numpy as np
from jax import lax
from jax.experimental import pallas as pl
from jax.experimental.pallas import tpu as pltpu

F32 = jnp.float32
BF16 = jnp.bfloat16
HI = lax.Precision.HIGHEST

D_MODEL = 1024
BW = 512
S5_GROUP = 16
S5_GROUPS = 32
S5_STATE = 64
HEAD = 64
N_HEADS = 8
N_PAIRS = N_HEADS // 2
LORA_W = 64
LORA_A = 64
LORA_V = 32
GN_EPS = 64e-5
M2_GROUPS = 2
M2_STATE = 128
M2_CONV = 4
M2_XBC = 1024
NORM_EPS = 1e-5

LANES = 128
SUBLANES = 8

COL_RW = 0
COL_U = 2048
COL_GS5 = 2560
COL_XBC = 3072
COL_Z = 4096
COL_WA = 4608
COL_MISC = 4736
PROJ_PAD = 4864

PROJ_TM = 256
PROJ_TN = 512
S5_T = 32
S5_CT = 64
S5_TILE_G = LANES // S5_GROUP
RW_T = 64
RW_NB = 4
RW_GC = 2
M2_T = 128
OUT_TM = 256
VMEM_LIMIT = 48 * 1024 * 1024


def _dot(a, b, prec=None):
    return jnp.dot(a, b, precision=prec, preferred_element_type=F32)


def _dot_nt(a, b, prec=None):
    return lax.dot_general(a, b, (((1,), (1,)), ((), ())), precision=prec, preferred_element_type=F32)


def _dot_tn(a, b, prec=None):
    return lax.dot_general(a, b, (((0,), (0,)), ((), ())), precision=prec, preferred_element_type=F32)


def _split3(x):
    hi = x.astype(BF16)
    r1 = x - hi.astype(F32)
    mid = r1.astype(BF16)
    lo = (r1 - mid.astype(F32)).astype(BF16)
    return hi, mid, lo


def _dot_split_rhs(m, x):
    hi, mid, lo = _split3(x)
    return _dot(m, hi) + _dot(m, mid) + _dot(m, lo)


def _dot_split_lhs(x, m):
    hi, mid, lo = _split3(x)
    return _dot(hi, m) + _dot(mid, m) + _dot(lo, m)


def _sigmoid(x):
    return 1.0 / (1.0 + jnp.exp(-x))


def _silu(x):
    return x * _sigmoid(x)


def _softplus(x):
    return jnp.maximum(x, 0.0) + jnp.log1p(jnp.exp(-jnp.abs(x)))


def _gelu_tanh(x):
    c = math.sqrt(2.0 / math.pi)
    return 0.5 * x * (1.0 + jnp.tanh(c * (x + 0.044715 * (x * x * x))))


def _prev_rows(x, halo, is_first):
    rolled = pltpu.roll(x, 1, axis=0)
    last = jnp.where(is_first, 0.0, halo[SUBLANES - 1:SUBLANES, :])
    row = lax.broadcasted_iota(jnp.int32, x.shape, 0)
    return jnp.where(row == 0, last, rolled)


def _proj_kernel(h_ref, g_ref, w_ref, o_ref):
    x = h_ref[...]
    ms = jnp.mean(x * x, axis=-1, keepdims=True)
    hn = (x * lax.rsqrt(ms + NORM_EPS) * g_ref[...]).astype(BF16)
    for j in range(0, PROJ_PAD, PROJ_TN):
        w = min(PROJ_TN, PROJ_PAD - j)
        o_ref[:, j:j + w] = _dot(hn, w_ref[:, j:j + w])


def _proj(h, g, w):
    L = h.shape[0]
    return pl.pallas_call(
        _proj_kernel,
        grid=(L // PROJ_TM,),
        in_specs=[
            pl.BlockSpec((PROJ_TM, D_MODEL), lambda i: (i, 0)),
            pl.BlockSpec((1, D_MODEL), lambda i: (0, 0)),
            pl.BlockSpec((D_MODEL, PROJ_PAD), lambda i: (0, 0)),
        ],
        out_specs=pl.BlockSpec((PROJ_TM, PROJ_PAD), lambda i: (i, 0)),
        out_shape=jax.ShapeDtypeStruct((L, PROJ_PAD), F32),
        compiler_params=pltpu.CompilerParams(
            dimension_semantics=("parallel",), vmem_limit_bytes=VMEM_LIMIT),
    )(h, g, w)


def _s5_kernel(u_ref, k_ref, e_ref, cm_ref, a1_ref, a2_ref, y_ref, *, nsteps):
    ub = u_ref[0].astype(BF16)
    x = _dot(ub, e_ref[0])
    row = lax.broadcasted_iota(jnp.int32, x.shape, 0)
    for s in range(nsteps):
        d = 1 << s
        sh = jnp.where(row >= d, pltpu.roll(x, d, axis=0), 0.0)
        x = x + a1_ref[0, s:s + 1, :] * sh + a2_ref[0, s:s + 1, :] * pltpu.roll(sh, S5_STATE, axis=1)
    sprev = jnp.where(row >= 1, pltpu.roll(x, 1, axis=0), 0.0)
    y_ref[0] = _dot(ub, k_ref[0]) + _dot(sprev.astype(BF16), cm_ref[0])


def _s5_tables(log_dt, a_re, a_im, b_re, b_im, c_re, c_im, d_skip, n_chunks):
    T = S5_T
    G, P = a_re.shape
    dt = jnp.exp(log_dt.astype(F32))[:, None]
    ar, ai = a_re.astype(F32), a_im.astype(F32)
    mag = jnp.exp(dt * ar)
    abar_re, abar_im = mag * jnp.cos(dt * ai), mag * jnp.sin(dt * ai)
    den = ar * ar + ai * ai
    f_re = ((abar_re - 1.0) * ar + abar_im * ai) / den
    f_im = (abar_im * ar - (abar_re - 1.0) * ai) / den
    br, bi = b_re.astype(F32), b_im.astype(F32)
    bb_re = f_re[..., None] * br - f_im[..., None] * bi
    bb_im = f_re[..., None] * bi + f_im[..., None] * br

    def powers(tau):
        tau = tau.astype(F32)[None, :, None]
        m = jnp.exp(tau * (dt * ar)[:, None, :])
        ang = tau * (dt * ai)[:, None, :]
        return m * jnp.cos(ang), m * jnp.sin(ang)

    pw_re, pw_im = powers(jnp.arange(T + 1))
    cr, ci = c_re.astype(F32), c_im.astype(F32)
    cp_re = cr[:, None] * pw_re[:, :, None, :] - ci[:, None] * pw_im[:, :, None, :]
    cp_im = cr[:, None] * pw_im[:, :, None, :] + ci[:, None] * pw_re[:, :, None, :]
    k_lag = (jnp.einsum('gtop,gpi->gtio', cp_re[:, :T], bb_re, precision=HI)
             - jnp.einsum('gtop,gpi->gtio', cp_im[:, :T], bb_im, precision=HI))
    eye = jnp.eye(S5_GROUP, dtype=F32)
    k_lag = k_lag.at[:, 0].add(d_skip.astype(F32)[:, :, None] * eye)
    idx = jnp.arange(T)[None, :] - jnp.arange(T)[:, None]
    toe = jnp.where((idx >= 0)[None, :, :, None, None], k_lag[:, jnp.clip(idx, 0, T - 1)], 0.0)
    toe = toe.transpose(0, 1, 3, 2, 4).reshape(G, T * S5_GROUP, T * S5_GROUP)
    rev_re, rev_im = pw_re[:, T - 1::-1][:, :T], pw_im[:, T - 1::-1][:, :T]
    bbt_re, bbt_im = bb_re.transpose(0, 2, 1), bb_im.transpose(0, 2, 1)
    e_re = rev_re[:, :, None, :] * bbt_re[:, None] - rev_im[:, :, None, :] * bbt_im[:, None]
    e_im = rev_re[:, :, None, :] * bbt_im[:, None] + rev_im[:, :, None, :] * bbt_re[:, None]
    e_map = jnp.concatenate([e_re, e_im], axis=-1).reshape(G, T * S5_GROUP, 2 * P)
    cm_re = cp_re[:, 1:].transpose(0, 3, 1, 2)
    cm_im = -cp_im[:, 1:].transpose(0, 3, 1, 2)
    c_map = jnp.concatenate([cm_re, cm_im], axis=1).reshape(G, 2 * P, T * S5_GROUP)
    nsteps = max(1, int(math.ceil(math.log2(n_chunks))))
    sp_re, sp_im = powers(T * (2 ** jnp.arange(nsteps)))
    a1 = jnp.concatenate([sp_re, sp_re], axis=-1)
    a2 = jnp.concatenate([-sp_im, sp_im], axis=-1)
    pad = (-nsteps) % SUBLANES
    a1 = jnp.pad(a1, ((0, 0), (0, pad), (0, 0)))
    a2 = jnp.pad(a2, ((0, 0), (0, pad), (0, 0)))
    return toe.astype(BF16), e_map.astype(BF16), c_map.astype(BF16), a1, a2, nsteps


def _s5(u, tables):
    toe, e_map, c_map, a1, a2, nsteps = tables
    L = u.shape[0]
    T, G, W = S5_T, S5_GROUPS, S5_T * S5_GROUP
    C = L // T
    ug = u.reshape(C, T, G, S5_GROUP).transpose(2, 0, 1, 3).reshape(G, C, W)
    yg = pl.pallas_call(
        functools.partial(_s5_kernel, nsteps=nsteps),
        grid=(G,),
        in_specs=[
            pl.BlockSpec((1, C, W), lambda g: (g, 0, 0)),
            pl.BlockSpec((1, W, W), lambda g: (g, 0, 0)),
            pl.BlockSpec((1, W, 2 * S5_STATE), lambda g: (g, 0, 0)),
            pl.BlockSpec((1, 2 * S5_STATE, W), lambda g: (g, 0, 0)),
            pl.BlockSpec((1, a1.shape[1], 2 * S5_STATE), lambda g: (g, 0, 0)),
            pl.BlockSpec((1, a2.shape[1], 2 * S5_STATE), lambda g: (g, 0, 0)),
        ],
        out_specs=pl.BlockSpec((1, C, W), lambda g: (g, 0, 0)),
        out_shape=jax.ShapeDtypeStruct((G, C, W), F32),
        compiler_params=pltpu.CompilerParams(
            dimension_semantics=("parallel",), vmem_limit_bytes=VMEM_LIMIT),
    )(ug, toe, e_map, c_map, a1, a2)
    return yg.reshape(G, C, T, S5_GROUP).transpose(1, 2, 0, 3).reshape(L, BW)


def _rwkv_kernel(*refs, has_vmix):
    if has_vmix:
        (rw_ref, rwp_ref, wa_ref, wap_ref, misc_ref, miscp_ref, vfirst_ref,
         mu_rw_ref, mu_wa_ref, mu_misc_ref, wup_ref, w0_ref, aup_ref, a0_ref, kk_ref, ka_ref, rk_ref,
         lng_ref, lnb_ref, vup_ref, v0_ref, tri_ref, y_ref, s_ref) = refs
    else:
        (rw_ref, rwp_ref, wa_ref, wap_ref,
         mu_rw_ref, mu_wa_ref, wup_ref, w0_ref, aup_ref, a0_ref, kk_ref, ka_ref, rk_ref,
         lng_ref, lnb_ref, tri_ref, y_ref, vout_ref, s_ref) = refs
    T = RW_T
    TB = RW_NB * T
    first = pl.program_id(0) == 0

    @pl.when(first)
    def _():
        s_ref[...] = jnp.zeros_like(s_ref)

    def lerp(ref, halo_ref, mu_ref):
        x = ref[...]
        return x + (_prev_rows(x, halo_ref[...], first) - x) * mu_ref[...]

    xs = lerp(rw_ref, rwp_ref, mu_rw_ref)
    r, k, v, g = (xs[:, i * BW:(i + 1) * BW] for i in range(4))
    wa = lerp(wa_ref, wap_ref, mu_wa_ref)
    w_log = -_softplus(-(w0_ref[...] + _dot(jnp.tanh(wa).astype(BF16), wup_ref[...]))) - 0.5
    lw = -jnp.exp(w_log)
    a = _sigmoid(a0_ref[...] + _dot(wa.astype(BF16), aup_ref[...]))
    if has_vmix:
        vd = lerp(misc_ref, miscp_ref, mu_misc_ref)
        mix = _sigmoid(v0_ref[...] + _dot(vd.astype(BF16), vup_ref[...]))
        v = v + (vfirst_ref[...] - v) * mix
    else:
        vout_ref[...] = v
    kk = k * kk_ref[...]
    kmod = k * (1.0 + (a - 1.0) * ka_ref[...])
    cums = _dot_split_rhs(tri_ref[...], lw)
    cl, rv = cums[:TB], cums[TB:]
    g_incl = jnp.exp(cl)
    g_inv = jnp.exp(-cl)
    g_end = jnp.exp(rv)
    kka = kk * a
    pa = kk * jnp.exp(cl - lw)
    pb = kka * g_inv
    pbe = kka * g_end
    pk = kmod * g_inv
    pke = kmod * g_end
    pr = r * g_incl
    rkr = r * kmod * rk_ref[...]
    gated = _silu(g)

    lane = lax.broadcasted_iota(jnp.int32, (T, LANES), 1)
    head0 = lane < HEAD
    rr = lax.broadcasted_iota(jnp.int32, (2 * T, 2 * T), 0)
    cc = lax.broadcasted_iota(jnp.int32, (2 * T, 2 * T), 1)
    strict = cc < rr
    incl = cc <= rr
    eye = (cc == rr).astype(F32)
    srow = lax.broadcasted_iota(jnp.int32, (2 * T, LANES), 0)
    slane = lax.broadcasted_iota(jnp.int32, (2 * T, LANES), 1)
    own = (srow < T) == (slane < HEAD)

    def stack(x):
        return jnp.concatenate([jnp.where(head0, x, 0.0), jnp.where(head0, 0.0, x)], axis=0)

    def b16(x):
        return x.astype(BF16)

    def rows(c):
        return slice(c * T, (c + 1) * T)

    def lanes(q):
        return slice(q * LANES, (q + 1) * LANES)

    state = [s_ref[q] for q in range(N_PAIRS)]
    for c0 in range(0, RW_NB, RW_GC):
        units = [(c, q) for c in range(c0, c0 + RW_GC) for q in range(N_PAIRS)]
        inv = []
        for c, q in units:
            kk_h = stack(kk[rows(c), lanes(q)])
            nrm = jnp.sqrt(jnp.sum(kk_h * kk_h, axis=-1, keepdims=True))
            inv.append(1.0 / jnp.maximum(nrm, 1e-12))
        a_t = [b16(-stack(pa[rows(c), lanes(q)]) * i) for (c, q), i in zip(units, inv)]
        b_t = [b16(stack(pb[rows(c), lanes(q)]) * i) for (c, q), i in zip(units, inv)]
        be_t = [b16(stack(pbe[rows(c), lanes(q)]) * i) for (c, q), i in zip(units, inv)]
        k_t = [b16(stack(pk[rows(c), lanes(q)])) for c, q in units]
        ke_t = [b16(stack(pke[rows(c), lanes(q)])) for c, q in units]
        r_s = [stack(pr[rows(c), lanes(q)]) for c, q in units]
        v_s = [stack(v[rows(c), lanes(q)]) for c, q in units]
        v_t = [b16(x) for x in v_s]
        aa = [_dot_nt(jnp.concatenate([a, b16(r)], axis=0), jnp.concatenate([b, k_], axis=0))
              for a, r, b, k_ in zip(a_t, r_s, b_t, k_t)]
        a_ak = [b16(jnp.where(strict, x[:2 * T, 2 * T:], 0.0)) for x in aa]
        a_rb = [b16(jnp.where(incl, x[2 * T:, :2 * T], 0.0)) for x in aa]
        a_rk = [b16(jnp.where(incl, x[2 * T:, 2 * T:], 0.0)) for x in aa]
        p = [jnp.where(strict, x[:2 * T, :2 * T], 0.0) for x in aa]
        minv = [eye + x for x in p]
        for _ in range(int(math.log2(T)) - 1):
            p_b = [b16(x) for x in p]
            p = [_dot(x, x) for x in p_b]
            minv = [m + _dot(b16(x), b16(m)) for x, m in zip(p, minv)]
        akv = [_dot(x, y) for x, y in zip(a_ak, v_t)]
        wu_b = [b16(_dot(b16(m), jnp.concatenate([a, b16(x)], axis=1)))
                for m, a, x in zip(minv, a_t, akv)]
        p_mat = [b16(_dot_tn(w[:, :LANES], be)) for w, be in zip(wu_b, be_t)]
        z_mat = [_dot_tn(jnp.concatenate([w[:, LANES:], vt], axis=0), jnp.concatenate([be, ke], axis=0))
                 for w, vt, be, ke in zip(wu_b, v_t, be_t, ke_t)]
        qy = [_dot(x, w) for x, w in zip(a_rb, wu_b)]
        q_mat = [b16(r + x[:, :LANES]) for r, x in zip(r_s, qy)]
        y0 = [x[:, LANES:] + _dot(ark, vt) for x, ark, vt in zip(qy, a_rk, v_t)]
        yh = []
        for i, (c, q) in enumerate(units):
            s = state[q]
            s_b = b16(s)
            yh.append(_dot_nt(q_mat[i], s_b) + y0[i])
            state[q] = s * g_incl[(c + 1) * T - 1:(c + 1) * T, lanes(q)] + _dot(s_b, p_mat[i]) + z_mat[i]
        for i, (c, q) in enumerate(units):
            mean = jnp.sum(yh[i], axis=-1, keepdims=True) * (1.0 / HEAD)
            cen = jnp.where(own, yh[i] - mean, 0.0)
            var = jnp.sum(cen * cen, axis=-1, keepdims=True) * (1.0 / HEAD)
            yn = cen * lax.rsqrt(var + GN_EPS)
            bonus = jnp.sum(stack(rkr[rows(c), lanes(q)]), axis=-1, keepdims=True) * v_s[i]
            yn = yn[:T] + yn[T:]
            bonus = bonus[:T] + bonus[T:]
            y_ref[rows(c), lanes(q)] = ((yn * lng_ref[:, lanes(q)] + lnb_ref[:, lanes(q)] + bonus)
                                        * gated[rows(c), lanes(q)])
    for q in range(N_PAIRS):
        s_ref[q] = state[q]


def _rwkv(proj, p, v_first):
    L = proj.shape[0]
    T = RW_T * RW_NB
    has_vmix = v_first is not None
    hb = T // SUBLANES

    def cur(width, col):
        return pl.BlockSpec((T, width), lambda i: (i, col // width))

    def halo(width, col):
        return pl.BlockSpec((SUBLANES, width), lambda i: (jnp.maximum(i * hb - 1, 0), col // width))

    def row(width):
        return pl.BlockSpec((1, width), lambda i: (0, 0))

    def full(a):
        return pl.BlockSpec(a.shape, lambda i: (0, 0))

    pos = np.arange(T)
    same = (pos[:, None] // RW_T) == (pos[None, :] // RW_T)
    tri = jnp.asarray(np.concatenate([same & (pos[None, :] <= pos[:, None]),
                                      same & (pos[None, :] > pos[:, None])], axis=0), BF16)
    args = [proj, proj, proj, proj]
    specs = [cur(4 * BW, COL_RW), halo(4 * BW, COL_RW), cur(LANES, COL_WA), halo(LANES, COL_WA)]
    if has_vmix:
        args += [proj, proj, v_first]
        specs += [cur(LANES, COL_MISC), halo(LANES, COL_MISC), pl.BlockSpec((T, BW), lambda i: (i, 0))]
        names = ['mu_rw', 'mu_wa', 'mu_misc', 'w_up', 'w0', 'a_up', 'a0', 'k_k', 'k_a', 'r_k',
                 'ln_g', 'ln_b', 'v_up', 'v0']
    else:
        names = ['mu_rw', 'mu_wa', 'w_up', 'w0', 'a_up', 'a0', 'k_k', 'k_a', 'r_k', 'ln_g', 'ln_b']
    for n in names:
        args.append(p[n])
        specs.append(full(p[n]))
    args.append(tri)
    specs.append(full(tri))
    tok = pl.BlockSpec((T, BW), lambda i: (i, 0))
    out_specs = tok if has_vmix else [tok, tok]
    tok_shape = jax.ShapeDtypeStruct((L, BW), F32)
    out_shape = tok_shape if has_vmix else [tok_shape, tok_shape]
    res = pl.pallas_call(
        functools.partial(_rwkv_kernel, has_vmix=has_vmix),
        grid=(L // T,),
        in_specs=specs,
        out_specs=out_specs,
        out_shape=out_shape,
        scratch_shapes=[pltpu.VMEM((N_PAIRS, LANES, LANES), F32)],
        compiler_params=pltpu.CompilerParams(
            dimension_semantics=("arbitrary",), vmem_limit_bytes=VMEM_LIMIT),
    )(*args)
    if has_vmix:
        return res, v_first
    return res[0], res[1]


def _mamba_kernel(xbc_ref, xbcp_ref, z_ref, misc_ref, cw_ref, cb_ref, dtb_ref, alog_ref, dfull_ref,
                  ng_ref, expand_ref, tri_ref, y_ref, st_ref):
    T = M2_T
    first = pl.program_id(0) == 0

    @pl.when(first)
    def _():
        st_ref[...] = jnp.zeros_like(st_ref)

    x = xbc_ref[...]
    halo = jnp.where(first, 0.0, xbcp_ref[...])
    row8 = lax.broadcasted_iota(jnp.int32, halo.shape, 0)
    acc = x * cw_ref[M2_CONV - 1:M2_CONV, :] + cb_ref[...]
    for j in range(1, M2_CONV):
        xr = pltpu.roll(x, j, axis=0)
        head = jnp.where(row8 < j, pltpu.roll(halo, j, axis=0), xr[:SUBLANES])
        xj = jnp.concatenate([head, xr[SUBLANES:]], axis=0)
        acc = acc + xj * cw_ref[M2_CONV - 1 - j:M2_CONV - j, :]
    xbc = _silu(acc)
    xm = xbc[:, :BW]
    bm = xbc[:, BW:BW + M2_GROUPS * M2_STATE]
    cm = xbc[:, BW + M2_GROUPS * M2_STATE:]
    bm_b, cm_b = bm.astype(BF16), cm.astype(BF16)

    dt = _softplus(misc_ref[...] + dtb_ref[...])
    da = dt * (-jnp.exp(alog_ref[...]))
    cs = _dot_split_rhs(tri_ref[...], da)
    cs_t = cs.T
    expand = expand_ref[...]
    cs_full = _dot_split_lhs(cs, expand)
    dt_full = _dot_split_lhs(dt, expand)
    cs_last = cs_full[T - 1:T, :]
    xdt = xm * dt_full
    ecs = jnp.exp(cs_full)
    xdte_b = (xdt * jnp.exp(cs_last - cs_full)).astype(BF16)
    xdt_b = xdt.astype(BF16)
    chunk_decay = jnp.exp(cs_last)

    rr = lax.broadcasted_iota(jnp.int32, (T, T), 0)
    cc = lax.broadcasted_iota(jnp.int32, (T, T), 1)
    causal = cc <= rr
    lane = lax.broadcasted_iota(jnp.int32, (T, LANES), 1)
    head0 = lane < HEAD

    scores = []
    for gi in range(M2_GROUPS):
        gs = slice(gi * M2_STATE, (gi + 1) * M2_STATE)
        scores.append(_dot_nt(cm_b[:, gs], bm_b[:, gs]))
    ys = []
    for q in range(N_PAIRS):
        gi = (2 * q) // (N_HEADS // M2_GROUPS)
        gs = slice(gi * M2_STATE, (gi + 1) * M2_STATE)
        sl = slice(q * LANES, (q + 1) * LANES)
        yd = []
        for h in (2 * q, 2 * q + 1):
            seg = cs[:, h:h + 1] - cs_t[h:h + 1, :]
            dec = jnp.where(causal, jnp.exp(jnp.minimum(seg, 0.0)), 0.0)
            yd.append(_dot((scores[gi] * dec).astype(BF16), xdt_b[:, sl]))
        st = st_ref[q]
        y_off = _dot(cm_b[:, gs], st.astype(BF16)) * ecs[:, sl]
        st_ref[q] = st * chunk_decay[:, sl] + _dot_tn(bm_b[:, gs], xdte_b[:, sl])
        ys.append(jnp.where(head0, yd[0], yd[1]) + y_off)
    y = jnp.concatenate(ys, axis=1) + xm * dfull_ref[...]
    y = y * _silu(z_ref[...])
    ms = jnp.mean(y * y, axis=-1, keepdims=True)
    y_ref[...] = y * lax.rsqrt(ms + NORM_EPS) * ng_ref[...]


def _mamba(proj, p):
    L = proj.shape[0]
    T = M2_T
    hb = T // SUBLANES
    tri = jnp.tril(jnp.ones((T, T), BF16))
    consts = [p['conv_w'], p['conv_b'], p['dt_bias'], p['a_log'], p['d_full'], p['norm_g'], p['expand'], tri]
    specs = [
        pl.BlockSpec((T, M2_XBC), lambda i: (i, COL_XBC // M2_XBC)),
        pl.BlockSpec((SUBLANES, M2_XBC), lambda i: (jnp.maximum(i * hb - 1, 0), COL_XBC // M2_XBC)),
        pl.BlockSpec((T, BW), lambda i: (i, COL_Z // BW)),
        pl.BlockSpec((T, LANES), lambda i: (i, COL_MISC // LANES)),
    ] + [pl.BlockSpec(c.shape, lambda i: (0, 0)) for c in consts]
    return pl.pallas_call(
        _mamba_kernel,
        grid=(L // T,),
        in_specs=specs,
        out_specs=pl.BlockSpec((T, BW), lambda i: (i, 0)),
        out_shape=jax.ShapeDtypeStruct((L, BW), F32),
        scratch_shapes=[pltpu.VMEM((N_PAIRS, M2_STATE, LANES), F32)],
        compiler_params=pltpu.CompilerParams(
            dimension_semantics=("arbitrary",), vmem_limit_bytes=VMEM_LIMIT),
    )(proj, proj, proj, proj, *consts)


def _out_kernel(h_ref, ys5_ref, gs5_ref, yrw_ref, ym2_ref, wglu_ref, bglu_ref, wo_ref, fg_ref, o_ref, *, final):
    ys = _gelu_tanh(ys5_ref[...])
    ys = ys * _sigmoid(_dot(ys.astype(BF16), wglu_ref[...]) + bglu_ref[...])
    ys = ys * _silu(gs5_ref[...])
    acc = _dot(ys.astype(BF16), wo_ref[0:BW, :])
    acc = acc + _dot(yrw_ref[...].astype(BF16), wo_ref[BW:2 * BW, :])
    acc = acc + _dot(ym2_ref[...].astype(BF16), wo_ref[2 * BW:3 * BW, :])
    hn = h_ref[...] + acc
    if final:
        ms = jnp.mean(hn * hn, axis=-1, keepdims=True)
        hn = hn * lax.rsqrt(ms + NORM_EPS) * fg_ref[...]
    o_ref[...] = hn


def _out(h, ys5, proj, yrw, ym2, w_glu, b_glu, w_out, final_g, final):
    L = h.shape[0]
    TM = OUT_TM
    tok = pl.BlockSpec((TM, BW), lambda i: (i, 0))
    return pl.pallas_call(
        functools.partial(_out_kernel, final=final),
        grid=(L // TM,),
        in_specs=[
            pl.BlockSpec((TM, D_MODEL), lambda i: (i, 0)),
            tok,
            pl.BlockSpec((TM, BW), lambda i: (i, COL_GS5 // BW)),
            tok,
            tok,
            pl.BlockSpec((BW, BW), lambda i: (0, 0)),
            pl.BlockSpec((1, BW), lambda i: (0, 0)),
            pl.BlockSpec((3 * BW, D_MODEL), lambda i: (0, 0)),
            pl.BlockSpec((1, D_MODEL), lambda i: (0, 0)),
        ],
        out_specs=pl.BlockSpec((TM, D_MODEL), lambda i: (i, 0)),
        out_shape=jax.ShapeDtypeStruct((L, D_MODEL), F32),
        compiler_params=pltpu.CompilerParams(
            dimension_semantics=("parallel",), vmem_limit_bytes=VMEM_LIMIT),
    )(h, ys5, proj, yrw, ym2, w_glu, b_glu, w_out, final_g)


def _s5_state_tables(log_dt, a_re, a_im, b_re, b_im, c_re, c_im):
    T = S5_CT
    G, P = a_re.shape
    dt = jnp.exp(log_dt.astype(F32))[:, None]
    ar, ai = a_re.astype(F32), a_im.astype(F32)
    mag = jnp.exp(dt * ar)
    abar_re, abar_im = mag * jnp.cos(dt * ai), mag * jnp.sin(dt * ai)
    den = ar * ar + ai * ai
    f_re = ((abar_re - 1.0) * ar + abar_im * ai) / den
    f_im = (abar_im * ar - (abar_re - 1.0) * ai) / den
    br, bi = b_re.astype(F32), b_im.astype(F32)
    bb_re = f_re[..., None] * br - f_im[..., None] * bi
    bb_im = f_re[..., None] * bi + f_im[..., None] * br

    def powers(e):
        e = jnp.asarray(e, F32)[:, None, None]
        m = jnp.exp(e * (dt * ar)[None])
        ang = e * (dt * ai)[None]
        return (m * jnp.cos(ang)).reshape(-1, G * P), (m * jnp.sin(ang)).reshape(-1, G * P)

    pos = np.arange(T, dtype=np.float32)
    half = T / 2
    zr, zi = powers(half - pos)
    sr, si = powers(pos - half)
    cr, ci = powers(np.array([half + 1.0], np.float32))
    tiles = G // S5_TILE_G
    eye = jnp.eye(S5_TILE_G, dtype=F32)

    def b_side(bb):
        t = bb.reshape(tiles, S5_TILE_G, P, S5_GROUP).transpose(0, 1, 3, 2)
        return (t[:, :, :, None, :] * eye[None, :, None, :, None]).reshape(tiles, LANES, S5_TILE_G * P)

    def c_side(cc):
        t = cc.astype(F32).reshape(tiles, S5_TILE_G, S5_GROUP, P).transpose(0, 1, 3, 2)
        return (t[:, :, :, None, :] * eye[None, :, None, :, None]).reshape(tiles, S5_TILE_G * P, LANES)

    wb = jnp.concatenate([b_side(bb_re), b_side(bb_im)], axis=-1).astype(BF16)
    return dict(wb=wb, wc_re=c_side(c_re).astype(BF16), wc_im=c_side(-c_im.astype(F32)).astype(BF16),
                zr=zr, zi=zi, sr=sr, si=si, car=jnp.concatenate([cr, ci], axis=0))


def _s5out_kernel(h_ref, u_ref, gs5_ref, yrw_ref, ym2_ref, wb_ref, wcr_ref, wci_ref, zr_ref, zi_ref, sr_ref,
                  si_ref, car_ref, d_ref, tri_ref, wglu_ref, bglu_ref, wo_ref, fg_ref, o_ref,
                  st_ref, sre_ref, sim_ref, *, final):
    T = S5_CT
    NS = S5_GROUPS * S5_STATE
    half = NS // (S5_GROUPS // S5_TILE_G)

    @pl.when(pl.program_id(0) == 0)
    def _():
        st_ref[...] = jnp.zeros_like(st_ref)

    u = u_ref[...]
    ub = u.astype(BF16)
    tiles = S5_GROUPS // S5_TILE_G
    bu = [_dot(ub[:, j * LANES:(j + 1) * LANES], wb_ref[j]) for j in range(tiles)]
    bu_re = jnp.concatenate([b[:, :half] for b in bu], axis=1)
    bu_im = jnp.concatenate([b[:, half:] for b in bu], axis=1)
    zr, zi, sr, si = zr_ref[...], zi_ref[...], sr_ref[...], si_ref[...]
    car_re, car_im = car_ref[0:1, :], car_ref[1:2, :]
    p_re, p_im = st_ref[0:1, :], st_ref[1:2, :]
    tri = tri_ref[...]
    for c in range(OUT_TM // T):
        rows = slice(c * T, (c + 1) * T)
        b_re, b_im = bu_re[rows], bu_im[rows]
        z = jnp.concatenate([zr * b_re - zi * b_im, zr * b_im + zi * b_re], axis=1).astype(BF16)
        w = _dot(tri, z)
        w_re = w[:, :NS] + (car_re * p_re - car_im * p_im)
        w_im = w[:, NS:] + (car_re * p_im + car_im * p_re)
        s_re = sr * w_re - si * w_im
        s_im = sr * w_im + si * w_re
        p_re, p_im = s_re[T - 1:T, :], s_im[T - 1:T, :]
        sre_ref[rows, :] = s_re.astype(BF16)
        sim_ref[rows, :] = s_im.astype(BF16)
    st_ref[0:1, :] = p_re
    st_ref[1:2, :] = p_im
    ys = jnp.concatenate(
        [_dot(sre_ref[:, j * half:(j + 1) * half], wcr_ref[j]) + _dot(sim_ref[:, j * half:(j + 1) * half], wci_ref[j])
         for j in range(tiles)], axis=1) + d_ref[...] * u
    ys = _gelu_tanh(ys)
    ys = ys * _sigmoid(_dot(ys.astype(BF16), wglu_ref[...]) + bglu_ref[...])
    ys = ys * _silu(gs5_ref[...])
    acc = _dot(ys.astype(BF16), wo_ref[0:BW, :])
    acc = acc + _dot(yrw_ref[...].astype(BF16), wo_ref[BW:2 * BW, :])
    acc = acc + _dot(ym2_ref[...].astype(BF16), wo_ref[2 * BW:3 * BW, :])
    hn = h_ref[...] + acc
    if final:
        ms = jnp.mean(hn * hn, axis=-1, keepdims=True)
        hn = hn * lax.rsqrt(ms + NORM_EPS) * fg_ref[...]
    o_ref[...] = hn


def _s5out(h, proj, yrw, ym2, tb, d_row, w_glu, b_glu, w_out, final_g, final):
    L = h.shape[0]
    TM = OUT_TM
    NS = S5_GROUPS * S5_STATE
    tri = jnp.tril(jnp.ones((S5_CT, S5_CT), BF16))
    tok = pl.BlockSpec((TM, BW), lambda i: (i, 0))
    consts = [tb['wb'], tb['wc_re'], tb['wc_im'], tb['zr'], tb['zi'], tb['sr'], tb['si'], tb['car'],
              d_row, tri, w_glu, b_glu, w_out, final_g]
    const_specs = [pl.BlockSpec(c.shape, (lambda i: (0, 0, 0)) if c.ndim == 3 else (lambda i: (0, 0)))
                   for c in consts]
    return pl.pallas_call(
        functools.partial(_s5out_kernel, final=final),
        grid=(L // TM,),
        in_specs=[
            pl.BlockSpec((TM, D_MODEL), lambda i: (i, 0)),
            pl.BlockSpec((TM, BW), lambda i: (i, COL_U // BW)),
            pl.BlockSpec((TM, BW), lambda i: (i, COL_GS5 // BW)),
            tok,
            tok,
        ] + const_specs,
        out_specs=pl.BlockSpec((TM, D_MODEL), lambda i: (i, 0)),
        out_shape=jax.ShapeDtypeStruct((L, D_MODEL), F32),
        scratch_shapes=[pltpu.VMEM((SUBLANES, NS), F32), pltpu.VMEM((TM, NS), BF16), pltpu.VMEM((TM, NS), BF16)],
        compiler_params=pltpu.CompilerParams(
            dimension_semantics=("arbitrary",), vmem_limit_bytes=VMEM_LIMIT),
    )(h, proj, proj, yrw, ym2, *consts)


def _reorder_w_in(w, w_vmix):
    o = np.cumsum([0, BW, BW, BW, BW, BW, LORA_W, LORA_A, BW, M2_XBC, N_HEADS, BW])
    u, gs5, r, k, v, wd, ad, g, xbc, dt, z = (w[:, o[i]:o[i + 1]] for i in range(11))
    vm = w_vmix if w_vmix is not None else jnp.zeros((w.shape[0], LORA_V), w.dtype)
    pad = jnp.zeros((w.shape[0], LANES - N_HEADS - LORA_V), w.dtype)
    return jnp.concatenate([r, k, v, g, u, gs5, xbc, z, wd, ad, dt, vm, pad], axis=1).astype(BF16)


def _pad_rows(w, start, total=LANES):
    return jnp.pad(w.astype(F32), ((start, total - start - w.shape[0]), (0, 0)))


def _row(a):
    return a.astype(F32).reshape(1, -1)


def kernel(x, norm_g, w_in, w_in_vmix, s5_log_dt, s5_a_re, s5_a_im, s5_b_re, s5_b_im, s5_c_re, s5_c_im, s5_d, s5_w_glu, s5_b_glu, rwkv_mu, rwkv_w_up, rwkv_w0, rwkv_a_up, rwkv_a0, rwkv_k_k, rwkv_k_a, rwkv_r_k, rwkv_ln_g, rwkv_ln_b, rwkv_vmix_mu, rwkv_v_up, rwkv_v0, m2_conv_w, m2_conv_b, m2_dt_bias, m2_a_log, m2_d, m2_norm_g, w_out, final_norm_g):
    bsz, L, d = x.shape
    assert bsz == 1 and d == D_MODEL
    assert L % max(PROJ_TM, OUT_TM, RW_T * RW_NB, M2_T, S5_T * SUBLANES) == 0
    depth = w_in.shape[0]
    h = x.reshape(L, d).astype(F32)
    expand = jnp.pad(jnp.repeat(jnp.eye(N_HEADS, dtype=F32), HEAD, axis=1), ((0, LANES - N_HEADS), (0, 0)))
    v_first = None
    for i in range(depth):
        w_cat = _reorder_w_in(w_in[i], w_in_vmix[i - 1] if i > 0 else None)
        proj = _proj(h, _row(norm_g[i]), w_cat)

        s5_tb = _s5_state_tables(s5_log_dt[i], s5_a_re[i], s5_a_im[i], s5_b_re[i], s5_b_im[i],
                                 s5_c_re[i], s5_c_im[i])

        mu = rwkv_mu[i].astype(F32)
        mo = np.cumsum([0, BW, BW, BW, LORA_W, LORA_A, BW])
        mr, mk, mv, mwd, mad, mg = (mu[mo[j]:mo[j + 1]] for j in range(6))
        rp = {
            'mu_rw': jnp.concatenate([mr, mk, mv, mg]).reshape(1, -1),
            'mu_wa': jnp.concatenate([mwd, mad]).reshape(1, -1),
            'w_up': _pad_rows(rwkv_w_up[i], 0).astype(BF16), 'w0': _row(rwkv_w0[i]),
            'a_up': _pad_rows(rwkv_a_up[i], LORA_W).astype(BF16), 'a0': _row(rwkv_a0[i]),
            'k_k': _row(rwkv_k_k[i]), 'k_a': _row(rwkv_k_a[i]), 'r_k': _row(rwkv_r_k[i]),
            'ln_g': _row(rwkv_ln_g[i]), 'ln_b': _row(rwkv_ln_b[i]),
        }
        if i > 0:
            rp['mu_misc'] = jnp.pad(rwkv_vmix_mu[i - 1].astype(F32),
                                    (N_HEADS, LANES - N_HEADS - LORA_V)).reshape(1, -1)
            rp['v_up'] = _pad_rows(rwkv_v_up[i - 1], N_HEADS).astype(BF16)
            rp['v0'] = _row(rwkv_v0[i - 1])
        y_rw, v_first = _rwkv(proj, rp, v_first)

        mp = {
            'conv_w': m2_conv_w[i].astype(F32), 'conv_b': _row(m2_conv_b[i]),
            'dt_bias': jnp.pad(m2_dt_bias[i].astype(F32), (0, LANES - N_HEADS)).reshape(1, -1),
            'a_log': jnp.pad(m2_a_log[i].astype(F32), (0, LANES - N_HEADS)).reshape(1, -1),
            'd_full': jnp.repeat(m2_d[i].astype(F32), HEAD).reshape(1, -1),
            'norm_g': _row(m2_norm_g[i]), 'expand': expand.astype(BF16),
        }
        y_m2 = _mamba(proj, mp)

        h = _s5out(h, proj, y_rw, y_m2, s5_tb, _row(s5_d[i]), s5_w_glu[i].astype(BF16), _row(s5_b_glu[i]),
                   w_out[i].astype(BF16), _row(final_norm_g), final=(i == depth - 1))
    return h.reshape(bsz, L, d).astype(x.dtype)
```

```python
import functools
import math

import jax
import jax.numpy as jnp
import numpy as np
from jax import lax
from jax.experimental import pallas as pl
from jax.experimental.pallas import tpu as pltpu

F32 = jnp.float32
BF16 = jnp.bfloat16
HI = lax.Precision.HIGHEST

D_MODEL = 1024
BW = 512
S5_GROUP = 16
S5_GROUPS = 32
S5_STATE = 64
HEAD = 64
N_HEADS = 8
N_PAIRS = N_HEADS // 2
LORA_W = 64
LORA_A = 64
LORA_V = 32
GN_EPS = 64e-5
M2_GROUPS = 2
M2_STATE = 128
M2_CONV = 4
M2_XBC = 1024
NORM_EPS = 1e-5

LANES = 128
SUBLANES = 8

COL_RW = 0
COL_U = 2048
COL_GS5 = 2560
COL_XBC = 3072
COL_Z = 4096
COL_WA = 4608
COL_MISC = 4736
PROJ_PAD = 4864

PROJ_TM = 256
PROJ_TN = 512
S5_T = 32
S5_CT = 64
S5_TILE_G = LANES // S5_GROUP
RW_T = 64
RW_NB = 4
RW_GC = 2
M2_T = 128
OUT_TM = 256
VMEM_LIMIT = 48 * 1024 * 1024


def _dot(a, b, prec=None):
    return jnp.dot(a, b, precision=prec, preferred_element_type=F32)


def _dot_nt(a, b, prec=None):
    return lax.dot_general(a, b, (((1,), (1,)), ((), ())), precision=prec, preferred_element_type=F32)


def _dot_tn(a, b, prec=None):
    return lax.dot_general(a, b, (((0,), (0,)), ((), ())), precision=prec, preferred_element_type=F32)


def _split3(x):
    hi = x.astype(BF16)
    r1 = x - hi.astype(F32)
    mid = r1.astype(BF16)
    lo = (r1 - mid.astype(F32)).astype(BF16)
    return hi, mid, lo


def _dot_split_rhs(m, x):
    hi, mid, lo = _split3(x)
    return _dot(m, hi) + _dot(m, mid) + _dot(m, lo)


def _dot_split_lhs(x, m):
    hi, mid, lo = _split3(x)
    return _dot(hi, m) + _dot(mid, m) + _dot(lo, m)


def _sigmoid(x):
    return 1.0 / (1.0 + jnp.exp(-x))


def _silu(x):
    return x * _sigmoid(x)


def _softplus(x):
    return jnp.maximum(x, 0.0) + jnp.log1p(jnp.exp(-jnp.abs(x)))


def _gelu_tanh(x):
    c = math.sqrt(2.0 / math.pi)
    return 0.5 * x * (1.0 + jnp.tanh(c * (x + 0.044715 * (x * x * x))))


def _prev_rows(x, halo, is_first):
    rolled = pltpu.roll(x, 1, axis=0)
    last = jnp.where(is_first, 0.0, halo[SUBLANES - 1:SUBLANES, :])
    row = lax.broadcasted_iota(jnp.int32, x.shape, 0)
    return jnp.where(row == 0, last, rolled)


def _proj_kernel(h0_ref, hnext_ref, g_ref, w_ref, o_ref, hn_a, hn_b):
    i = pl.program_id(0)

    def norm(ref):
        x = ref[...]
        ms = jnp.mean(x * x, axis=-1, keepdims=True)
        return (x * lax.rsqrt(ms + NORM_EPS) * g_ref[...]).astype(BF16)

    @pl.when(i == 0)
    def _():
        hn_a[...] = norm(h0_ref)

    def body(cur, nxt):
        nxt[...] = norm(hnext_ref)
        hn = cur[...]
        for j in range(0, PROJ_PAD, PROJ_TN):
            w = min(PROJ_TN, PROJ_PAD - j)
            o_ref[:, j:j + w] = _dot(hn, w_ref[:, j:j + w])

    pl.when(i % 2 == 0)(functools.partial(body, hn_a, hn_b))
    pl.when(i % 2 == 1)(functools.partial(body, hn_b, hn_a))


def _proj(h, g, w):
    L = h.shape[0]
    nb = L // PROJ_TM
    return pl.pallas_call(
        _proj_kernel,
        grid=(nb,),
        in_specs=[
            pl.BlockSpec((PROJ_TM, D_MODEL), lambda i: (0, 0)),
            pl.BlockSpec((PROJ_TM, D_MODEL), lambda i: (jnp.minimum(i + 1, nb - 1), 0)),
            pl.BlockSpec((1, D_MODEL), lambda i: (0, 0)),
            pl.BlockSpec((D_MODEL, PROJ_PAD), lambda i: (0, 0)),
        ],
        out_specs=pl.BlockSpec((PROJ_TM, PROJ_PAD), lambda i: (i, 0)),
        out_shape=jax.ShapeDtypeStruct((L, PROJ_PAD), F32),
        scratch_shapes=[pltpu.VMEM((PROJ_TM, D_MODEL), BF16), pltpu.VMEM((PROJ_TM, D_MODEL), BF16)],
        compiler_params=pltpu.CompilerParams(
            dimension_semantics=("arbitrary",), vmem_limit_bytes=VMEM_LIMIT),
    )(h, h, g, w)


def _s5_kernel(u_ref, k_ref, e_ref, cm_ref, a1_ref, a2_ref, y_ref, *, nsteps):
    ub = u_ref[0].astype(BF16)
    x = _dot(ub, e_ref[0])
    row = lax.broadcasted_iota(jnp.int32, x.shape, 0)
    for s in range(nsteps):
        d = 1 << s
        sh = jnp.where(row >= d, pltpu.roll(x, d, axis=0), 0.0)
        x = x + a1_ref[0, s:s + 1, :] * sh + a2_ref[0, s:s + 1, :] * pltpu.roll(sh, S5_STATE, axis=1)
    sprev = jnp.where(row >= 1, pltpu.roll(x, 1, axis=0), 0.0)
    y_ref[0] = _dot(ub, k_ref[0]) + _dot(sprev.astype(BF16), cm_ref[0])


def _s5_tables(log_dt, a_re, a_im, b_re, b_im, c_re, c_im, d_skip, n_chunks):
    T = S5_T
    G, P = a_re.shape
    dt = jnp.exp(log_dt.astype(F32))[:, None]
    ar, ai = a_re.astype(F32), a_im.astype(F32)
    mag = jnp.exp(dt * ar)
    abar_re, abar_im = mag * jnp.cos(dt * ai), mag * jnp.sin(dt * ai)
    den = ar * ar + ai * ai
    f_re = ((abar_re - 1.0) * ar + abar_im * ai) / den
    f_im = (abar_im * ar - (abar_re - 1.0) * ai) / den
    br, bi = b_re.astype(F32), b_im.astype(F32)
    bb_re = f_re[..., None] * br - f_im[..., None] * bi
    bb_im = f_re[..., None] * bi + f_im[..., None] * br

    def powers(tau):
        tau = tau.astype(F32)[None, :, None]
        m = jnp.exp(tau * (dt * ar)[:, None, :])
        ang = tau * (dt * ai)[:, None, :]
        return m * jnp.cos(ang), m * jnp.sin(ang)

    pw_re, pw_im = powers(jnp.arange(T + 1))
    cr, ci = c_re.astype(F32), c_im.astype(F32)
    cp_re = cr[:, None] * pw_re[:, :, None, :] - ci[:, None] * pw_im[:, :, None, :]
    cp_im = cr[:, None] * pw_im[:, :, None, :] + ci[:, None] * pw_re[:, :, None, :]
    k_lag = (jnp.einsum('gtop,gpi->gtio', cp_re[:, :T], bb_re, precision=HI)
             - jnp.einsum('gtop,gpi->gtio', cp_im[:, :T], bb_im, precision=HI))
    eye = jnp.eye(S5_GROUP, dtype=F32)
    k_lag = k_lag.at[:, 0].add(d_skip.astype(F32)[:, :, None] * eye)
    idx = jnp.arange(T)[None, :] - jnp.arange(T)[:, None]
    toe = jnp.where((idx >= 0)[None, :, :, None, None], k_lag[:, jnp.clip(idx, 0, T - 1)], 0.0)
    toe = toe.transpose(0, 1, 3, 2, 4).reshape(G, T * S5_GROUP, T * S5_GROUP)
    rev_re, rev_im = pw_re[:, T - 1::-1][:, :T], pw_im[:, T - 1::-1][:, :T]
    bbt_re, bbt_im = bb_re.transpose(0, 2, 1), bb_im.transpose(0, 2, 1)
    e_re = rev_re[:, :, None, :] * bbt_re[:, None] - rev_im[:, :, None, :] * bbt_im[:, None]
    e_im = rev_re[:, :, None, :] * bbt_im[:, None] + rev_im[:, :, None, :] * bbt_re[:, None]
    e_map = jnp.concatenate([e_re, e_im], axis=-1).reshape(G, T * S5_GROUP, 2 * P)
    cm_re = cp_re[:, 1:].transpose(0, 3, 1, 2)
    cm_im = -cp_im[:, 1:].transpose(0, 3, 1, 2)
    c_map = jnp.concatenate([cm_re, cm_im], axis=1).reshape(G, 2 * P, T * S5_GROUP)
    nsteps = max(1, int(math.ceil(math.log2(n_chunks))))
    sp_re, sp_im = powers(T * (2 ** jnp.arange(nsteps)))
    a1 = jnp.concatenate([sp_re, sp_re], axis=-1)
    a2 = jnp.concatenate([-sp_im, sp_im], axis=-1)
    pad = (-nsteps) % SUBLANES
    a1 = jnp.pad(a1, ((0, 0), (0, pad), (0, 0)))
    a2 = jnp.pad(a2, ((0, 0), (0, pad), (0, 0)))
    return toe.astype(BF16), e_map.astype(BF16), c_map.astype(BF16), a1, a2, nsteps


def _s5(u, tables):
    toe, e_map, c_map, a1, a2, nsteps = tables
    L = u.shape[0]
    T, G, W = S5_T, S5_GROUPS, S5_T * S5_GROUP
    C = L // T
    ug = u.reshape(C, T, G, S5_GROUP).transpose(2, 0, 1, 3).reshape(G, C, W)
    yg = pl.pallas_call(
        functools.partial(_s5_kernel, nsteps=nsteps),
        grid=(G,),
        in_specs=[
            pl.BlockSpec((1, C, W), lambda g: (g, 0, 0)),
            pl.BlockSpec((1, W, W), lambda g: (g, 0, 0)),
            pl.BlockSpec((1, W, 2 * S5_STATE), lambda g: (g, 0, 0)),
            pl.BlockSpec((1, 2 * S5_STATE, W), lambda g: (g, 0, 0)),
            pl.BlockSpec((1, a1.shape[1], 2 * S5_STATE), lambda g: (g, 0, 0)),
            pl.BlockSpec((1, a2.shape[1], 2 * S5_STATE), lambda g: (g, 0, 0)),
        ],
        out_specs=pl.BlockSpec((1, C, W), lambda g: (g, 0, 0)),
        out_shape=jax.ShapeDtypeStruct((G, C, W), F32),
        compiler_params=pltpu.CompilerParams(
            dimension_semantics=("parallel",), vmem_limit_bytes=VMEM_LIMIT),
    )(ug, toe, e_map, c_map, a1, a2)
    return yg.reshape(G, C, T, S5_GROUP).transpose(1, 2, 0, 3).reshape(L, BW)


def _rwkv_kernel(*refs, has_vmix):
    if has_vmix:
        (rw_ref, rwp_ref, wa_ref, wap_ref, misc_ref, miscp_ref, vfirst_ref,
         mu_rw_ref, mu_wa_ref, mu_misc_ref, wup_ref, w0_ref, aup_ref, a0_ref, kk_ref, ka_ref, rk_ref,
         lng_ref, lnb_ref, vup_ref, v0_ref, tri_ref, y_ref, s_ref) = refs
    else:
        (rw_ref, rwp_ref, wa_ref, wap_ref,
         mu_rw_ref, mu_wa_ref, wup_ref, w0_ref, aup_ref, a0_ref, kk_ref, ka_ref, rk_ref,
         lng_ref, lnb_ref, tri_ref, y_ref, vout_ref, s_ref) = refs
    T = RW_T
    TB = RW_NB * T
    first = pl.program_id(0) == 0

    @pl.when(first)
    def _():
        s_ref[...] = jnp.zeros_like(s_ref)

    def lerp(ref, halo_ref, mu_ref):
        x = ref[...]
        return x + (_prev_rows(x, halo_ref[...], first) - x) * mu_ref[...]

    xs = lerp(rw_ref, rwp_ref, mu_rw_ref)
    r, k, v, g = (xs[:, i * BW:(i + 1) * BW] for i in range(4))
    wa = lerp(wa_ref, wap_ref, mu_wa_ref)
    w_log = -_softplus(-(w0_ref[...] + _dot(jnp.tanh(wa).astype(BF16), wup_ref[...]))) - 0.5
    lw = -jnp.exp(w_log)
    a = _sigmoid(a0_ref[...] + _dot(wa.astype(BF16), aup_ref[...]))
    if has_vmix:
        vd = lerp(misc_ref, miscp_ref, mu_misc_ref)
        mix = _sigmoid(v0_ref[...] + _dot(vd.astype(BF16), vup_ref[...]))
        v = v + (vfirst_ref[...] - v) * mix
    else:
        vout_ref[...] = v
    kk = k * kk_ref[...]
    kmod = k * (1.0 + (a - 1.0) * ka_ref[...])
    cums = _dot_split_rhs(tri_ref[...], lw)
    cl, rv = cums[:TB], cums[TB:]
    g_incl = jnp.exp(cl)
    g_inv = jnp.exp(-cl)
    g_end = jnp.exp(rv)
    kka = kk * a
    pa = kk * jnp.exp(cl - lw)
    pb = kka * g_inv
    pbe = kka * g_end
    pk = kmod * g_inv
    pke = kmod * g_end
    pr = r * g_incl
    rkr = r * kmod * rk_ref[...]
    gated = _silu(g)

    lane = lax.broadcasted_iota(jnp.int32, (T, LANES), 1)
    head0 = lane < HEAD
    rr = lax.broadcasted_iota(jnp.int32, (2 * T, 2 * T), 0)
    cc = lax.broadcasted_iota(jnp.int32, (2 * T, 2 * T), 1)
    strict = cc < rr
    incl = cc <= rr
    eye = (cc == rr).astype(F32)
    srow = lax.broadcasted_iota(jnp.int32, (2 * T, LANES), 0)
    slane = lax.broadcasted_iota(jnp.int32, (2 * T, LANES), 1)
    own = (srow < T) == (slane < HEAD)

    def stack(x):
        return jnp.concatenate([jnp.where(head0, x, 0.0), jnp.where(head0, 0.0, x)], axis=0)

    def b16(x):
        return x.astype(BF16)

    def rows(c):
        return slice(c * T, (c + 1) * T)

    def lanes(q):
        return slice(q * LANES, (q + 1) * LANES)

    state = [s_ref[q] for q in range(N_PAIRS)]
    for c0 in range(0, RW_NB, RW_GC):
        units = [(c, q) for c in range(c0, c0 + RW_GC) for q in range(N_PAIRS)]
        inv = []
        for c, q in units:
            kk_h = stack(kk[rows(c), lanes(q)])
            nrm = jnp.sqrt(jnp.sum(kk_h * kk_h, axis=-1, keepdims=True))
            inv.append(1.0 / jnp.maximum(nrm, 1e-12))
        a_t = [b16(-stack(pa[rows(c), lanes(q)]) * i) for (c, q), i in zip(units, inv)]
        b_t = [b16(stack(pb[rows(c), lanes(q)]) * i) for (c, q), i in zip(units, inv)]
        be_t = [b16(stack(pbe[rows(c), lanes(q)]) * i) for (c, q), i in zip(units, inv)]
        k_t = [b16(stack(pk[rows(c), lanes(q)])) for c, q in units]
        ke_t = [b16(stack(pke[rows(c), lanes(q)])) for c, q in units]
        r_s = [stack(pr[rows(c), lanes(q)]) for c, q in units]
        v_s = [stack(v[rows(c), lanes(q)]) for c, q in units]
        v_t = [b16(x) for x in v_s]
        aa = [_dot_nt(jnp.concatenate([a, b16(r)], axis=0), jnp.concatenate([b, k_], axis=0))
              for a, r, b, k_ in zip(a_t, r_s, b_t, k_t)]
        a_ak = [b16(jnp.where(strict, x[:2 * T, 2 * T:], 0.0)) for x in aa]
        a_rb = [b16(jnp.where(incl, x[2 * T:, :2 * T], 0.0)) for x in aa]
        a_rk = [b16(jnp.where(incl, x[2 * T:, 2 * T:], 0.0)) for x in aa]
        p = [jnp.where(strict, x[:2 * T, :2 * T], 0.0) for x in aa]
        minv = [eye + x for x in p]
        for _ in range(int(math.log2(T)) - 1):
            p_b = [b16(x) for x in p]
            p = [_dot(x, x) for x in p_b]
            minv = [m + _dot(b16(x), b16(m)) for x, m in zip(p, minv)]
        akv = [_dot(x, y) for x, y in zip(a_ak, v_t)]
        wu_b = [b16(_dot(b16(m), jnp.concatenate([a, b16(x)], axis=1)))
                for m, a, x in zip(minv, a_t, akv)]
        p_mat = [b16(_dot_tn(w[:, :LANES], be)) for w, be in zip(wu_b, be_t)]
        z_mat = [_dot_tn(jnp.concatenate([w[:, LANES:], vt], axis=0), jnp.concatenate([be, ke], axis=0))
                 for w, vt, be, ke in zip(wu_b, v_t, be_t, ke_t)]
        qy = [_dot(x, w) for x, w in zip(a_rb, wu_b)]
        q_mat = [b16(r + x[:, :LANES]) for r, x in zip(r_s, qy)]
        y0 = [x[:, LANES:] + _dot(ark, vt) for x, ark, vt in zip(qy, a_rk, v_t)]
        yh = []
        for i, (c, q) in enumerate(units):
            s = state[q]
            s_b = b16(s)
            yh.append(_dot_nt(q_mat[i], s_b) + y0[i])
            state[q] = s * g_incl[(c + 1) * T - 1:(c + 1) * T, lanes(q)] + _dot(s_b, p_mat[i]) + z_mat[i]
        for i, (c, q) in enumerate(units):
            mean = jnp.sum(yh[i], axis=-1, keepdims=True) * (1.0 / HEAD)
            cen = jnp.where(own, yh[i] - mean, 0.0)
            var = jnp.sum(cen * cen, axis=-1, keepdims=True) * (1.0 / HEAD)
            yn = cen * lax.rsqrt(var + GN_EPS)
            bonus = jnp.sum(stack(rkr[rows(c), lanes(q)]), axis=-1, keepdims=True) * v_s[i]
            yn = yn[:T] + yn[T:]
            bonus = bonus[:T] + bonus[T:]
            y_ref[rows(c), lanes(q)] = ((yn * lng_ref[:, lanes(q)] + lnb_ref[:, lanes(q)] + bonus)
                                        * gated[rows(c), lanes(q)])
    for q in range(N_PAIRS):
        s_ref[q] = state[q]


def _rwkv_kernel2(*refs, has_vmix):
    if has_vmix:
        (rw_ref, rwp_ref, wa_ref, wap_ref, misc_ref, miscp_ref, vfirst_ref,
         mu_rw_ref, mu_wa_ref, mu_misc_ref, wup_ref, w0_ref, aup_ref, a0_ref, kk_ref, ka_ref, rk_ref,
         lng_ref, lnb_ref, vup_ref, v0_ref, tri_ref, y_ref, s_ref) = refs
    else:
        (rw_ref, rwp_ref, wa_ref, wap_ref,
         mu_rw_ref, mu_wa_ref, wup_ref, w0_ref, aup_ref, a0_ref, kk_ref, ka_ref, rk_ref,
         lng_ref, lnb_ref, tri_ref, y_ref, vout_ref, s_ref) = refs
    T = RW_T
    TB = RW_NB * T
    first = pl.program_id(0) == 0

    @pl.when(first)
    def _():
        s_ref[...] = jnp.zeros_like(s_ref)

    def lerp(ref, halo_ref, mu_ref):
        x = ref[...]
        return x + (_prev_rows(x, halo_ref[...], first) - x) * mu_ref[...]

    def b16(x):
        return x.astype(BF16)

    xs = lerp(rw_ref, rwp_ref, mu_rw_ref)
    r, k, v, g = (xs[:, i * BW:(i + 1) * BW] for i in range(4))
    wa = lerp(wa_ref, wap_ref, mu_wa_ref)
    w_log = -_softplus(-(w0_ref[...] + _dot(b16(jnp.tanh(wa)), wup_ref[...]))) - 0.5
    lw = -jnp.exp(w_log)
    a = _sigmoid(a0_ref[...] + _dot(b16(wa), aup_ref[...]))
    if has_vmix:
        vd = lerp(misc_ref, miscp_ref, mu_misc_ref)
        mix = _sigmoid(v0_ref[...] + _dot(b16(vd), vup_ref[...]))
        v = v + (vfirst_ref[...] - v) * mix
    else:
        vout_ref[...] = v
    kk = k * kk_ref[...]
    kmod = k * (1.0 + (a - 1.0) * ka_ref[...])
    cums = _dot_split_rhs(tri_ref[...], lw)
    cl, rv = cums[:TB], cums[TB:]
    g_incl = jnp.exp(cl)
    g_inv = jnp.exp(-cl)
    g_end = jnp.exp(rv)
    rkr = r * kmod * rk_ref[...]

    def per_head_sum(x):
        h0 = lax.broadcasted_iota(jnp.int32, x.shape, 1) < HEAD
        f0 = h0.astype(F32)
        s0 = jnp.sum(jnp.where(h0, x, 0.0), axis=-1, keepdims=True)
        s1 = jnp.sum(jnp.where(h0, 0.0, x), axis=-1, keepdims=True)
        return s0 * f0 + s1 * (1.0 - f0)

    inv, bonus_w = [], []
    for q in range(N_PAIRS):
        sl = slice(q * LANES, (q + 1) * LANES)
        sq = kk[:, sl] * kk[:, sl]
        inv.append(1.0 / jnp.maximum(jnp.sqrt(per_head_sum(sq)), 1e-12))
        bonus_w.append(per_head_sum(rkr[:, sl]))
    kkn = kk * jnp.concatenate(inv, axis=1)
    bonus = jnp.concatenate(bonus_w, axis=1) * v
    kkna = kkn * a
    na_b = b16(-kkn * jnp.exp(cl - lw))
    nb_b = b16(kkna * g_inv)
    nbe_b = b16(kkna * g_end)
    pk_b = b16(kmod * g_inv)
    pke_b = b16(kmod * g_end)
    pr = r * g_incl
    pr_b = b16(pr)
    v_b = b16(v)
    gated = _silu(g)

    m0 = (lax.broadcasted_iota(jnp.int32, (1, LANES), 1) < HEAD).astype(BF16)
    m1 = 1.0 - m0
    row_w = lax.broadcasted_iota(jnp.int32, (T, LANES), 0)
    col_w = lax.broadcasted_iota(jnp.int32, (T, LANES), 1) % HEAD
    strict = col_w < row_w
    incl = col_w <= row_w
    eye_w = (col_w == row_w).astype(F32)
    rr = lax.broadcasted_iota(jnp.int32, (LANES, LANES), 0)
    cc = lax.broadcasted_iota(jnp.int32, (LANES, LANES), 1)
    same_head = (rr < HEAD) == (cc < HEAD)

    def stack(x):
        return jnp.concatenate([x * m0, x * m1], axis=0)

    def rows(c):
        return slice(c * T, (c + 1) * T)

    def lanes(q):
        return slice(q * LANES, (q + 1) * LANES)

    state = [s_ref[q] for q in range(N_PAIRS)]
    for c0 in range(0, RW_NB, RW_GC):
        units = [(c, q) for c in range(c0, c0 + RW_GC) for q in range(N_PAIRS)]
        a_n = [na_b[rows(c), lanes(q)] for c, q in units]
        r_n = [pr_b[rows(c), lanes(q)] for c, q in units]
        v_n = [v_b[rows(c), lanes(q)] for c, q in units]
        be_n = [nbe_b[rows(c), lanes(q)] for c, q in units]
        ke_n = [pke_b[rows(c), lanes(q)] for c, q in units]
        b_st = [stack(nb_b[rows(c), lanes(q)]) for c, q in units]
        k_st = [stack(pk_b[rows(c), lanes(q)]) for c, q in units]
        a_st = [stack(x) for x in a_n]
        v_st = [stack(x) for x in v_n]
        aa = [_dot_nt(jnp.concatenate([a_, r_], axis=0), jnp.concatenate([b_, k_], axis=0))
              for a_, r_, b_, k_ in zip(a_n, r_n, b_st, k_st)]
        a_ak = [b16(jnp.where(strict, x[:T, LANES:], 0.0)) for x in aa]
        a_rb = [b16(jnp.where(incl, x[T:, :LANES], 0.0)) for x in aa]
        a_rk = [b16(jnp.where(incl, x[T:, LANES:], 0.0)) for x in aa]
        p = [jnp.where(strict, x[:T, :LANES], 0.0) for x in aa]
        minv = [eye_w + x for x in p]
        for _ in range(int(math.log2(T)) - 1):
            p_b = [b16(x) for x in p]
            p = [_dot(x, stack(x)) for x in p_b]
            minv = [m + _dot(b16(x), stack(b16(m))) for x, m in zip(p, minv)]
        akv = [_dot(x, y) for x, y in zip(a_ak, v_st)]
        wu_b = [b16(_dot(b16(m), jnp.concatenate([a_, stack(b16(x))], axis=1)))
                for m, a_, x in zip(minv, a_st, akv)]
        p_mat = [b16(jnp.where(same_head, _dot_tn(w[:, :LANES], be), 0.0)) for w, be in zip(wu_b, be_n)]
        z_mat = [jnp.where(same_head,
                           _dot_tn(jnp.concatenate([w[:, LANES:], vn], axis=0), jnp.concatenate([be, ke], axis=0)),
                           0.0)
                 for w, vn, be, ke in zip(wu_b, v_n, be_n, ke_n)]
        q_mat = [b16(pr[rows(c), lanes(q)] + _dot(arb, stack(w[:, :LANES])))
                 for (c, q), arb, w in zip(units, a_rb, wu_b)]
        y0 = [_dot(jnp.concatenate([arb, ark], axis=1), jnp.concatenate([stack(w[:, LANES:]), vs], axis=0))
              for arb, ark, w, vs in zip(a_rb, a_rk, wu_b, v_st)]
        ys = []
        for i, (c, q) in enumerate(units):
            s = state[q]
            s_b = b16(s)
            ys.append(_dot_nt(q_mat[i], s_b) + y0[i])
            state[q] = s * g_incl[(c + 1) * T - 1:(c + 1) * T, lanes(q)] + _dot(s_b, p_mat[i]) + z_mat[i]
        for i, (c, q) in enumerate(units):
            y = ys[i]
            mean = per_head_sum(y) * (1.0 / HEAD)
            cen = y - mean
            var = per_head_sum(cen * cen) * (1.0 / HEAD)
            yn = cen * lax.rsqrt(var + GN_EPS)
            y_ref[rows(c), lanes(q)] = ((yn * lng_ref[:, lanes(q)] + lnb_ref[:, lanes(q)]
                                         + bonus[rows(c), lanes(q)]) * gated[rows(c), lanes(q)])
    for q in range(N_PAIRS):
        s_ref[q] = state[q]


def _rwkv(proj, p, v_first):
    L = proj.shape[0]
    T = RW_T * RW_NB
    has_vmix = v_first is not None
    hb = T // SUBLANES

    def cur(width, col):
        return pl.BlockSpec((T, width), lambda i: (i, col // width))

    def halo(width, col):
        return pl.BlockSpec((SUBLANES, width), lambda i: (jnp.maximum(i * hb - 1, 0), col // width))

    def row(width):
        return pl.BlockSpec((1, width), lambda i: (0, 0))

    def full(a):
        return pl.BlockSpec(a.shape, lambda i: (0, 0))

    pos = np.arange(T)
    same = (pos[:, None] // RW_T) == (pos[None, :] // RW_T)
    tri = jnp.asarray(np.concatenate([same & (pos[None, :] <= pos[:, None]),
                                      same & (pos[None, :] > pos[:, None])], axis=0), BF16)
    args = [proj, proj, proj, proj]
    specs = [cur(4 * BW, COL_RW), halo(4 * BW, COL_RW), cur(LANES, COL_WA), halo(LANES, COL_WA)]
    if has_vmix:
        args += [proj, proj, v_first]
        specs += [cur(LANES, COL_MISC), halo(LANES, COL_MISC), pl.BlockSpec((T, BW), lambda i: (i, 0))]
        names = ['mu_rw', 'mu_wa', 'mu_misc', 'w_up', 'w0', 'a_up', 'a0', 'k_k', 'k_a', 'r_k',
                 'ln_g', 'ln_b', 'v_up', 'v0']
    else:
        names = ['mu_rw', 'mu_wa', 'w_up', 'w0', 'a_up', 'a0', 'k_k', 'k_a', 'r_k', 'ln_g', 'ln_b']
    for n in names:
        args.append(p[n])
        specs.append(full(p[n]))
    args.append(tri)
    specs.append(full(tri))
    tok = pl.BlockSpec((T, BW), lambda i: (i, 0))
    out_specs = tok if has_vmix else [tok, tok]
    tok_shape = jax.ShapeDtypeStruct((L, BW), F32)
    out_shape = tok_shape if has_vmix else [tok_shape, tok_shape]
    res = pl.pallas_call(
        functools.partial(_rwkv_kernel2, has_vmix=has_vmix),
        grid=(L // T,),
        in_specs=specs,
        out_specs=out_specs,
        out_shape=out_shape,
        scratch_shapes=[pltpu.VMEM((N_PAIRS, LANES, LANES), F32)],
        compiler_params=pltpu.CompilerParams(
            dimension_semantics=("arbitrary",), vmem_limit_bytes=VMEM_LIMIT),
    )(*args)
    if has_vmix:
        return res, v_first
    return res[0], res[1]


def _mamba_kernel(xbc_ref, xbcp_ref, z_ref, misc_ref, cw_ref, cb_ref, dtb_ref, alog_ref, dfull_ref,
                  ng_ref, expand_ref, tri_ref, y_ref, st_ref):
    T = M2_T
    first = pl.program_id(0) == 0

    @pl.when(first)
    def _():
        st_ref[...] = jnp.zeros_like(st_ref)

    x = xbc_ref[...]
    halo = jnp.where(first, 0.0, xbcp_ref[...])
    row8 = lax.broadcasted_iota(jnp.int32, halo.shape, 0)
    acc = x * cw_ref[M2_CONV - 1:M2_CONV, :] + cb_ref[...]
    for j in range(1, M2_CONV):
        xr = pltpu.roll(x, j, axis=0)
        head = jnp.where(row8 < j, pltpu.roll(halo, j, axis=0), xr[:SUBLANES])
        xj = jnp.concatenate([head, xr[SUBLANES:]], axis=0)
        acc = acc + xj * cw_ref[M2_CONV - 1 - j:M2_CONV - j, :]
    xbc = _silu(acc)
    xm = xbc[:, :BW]
    bm = xbc[:, BW:BW + M2_GROUPS * M2_STATE]
    cm = xbc[:, BW + M2_GROUPS * M2_STATE:]
    bm_b, cm_b = bm.astype(BF16), cm.astype(BF16)

    dt = _softplus(misc_ref[...] + dtb_ref[...])
    da = dt * (-jnp.exp(alog_ref[...]))
    cs = _dot_split_rhs(tri_ref[...], da)
    cs_t = cs.T
    expand = expand_ref[...]
    cs_full = _dot_split_lhs(cs, expand)
    dt_full = _dot_split_lhs(dt, expand)
    cs_last = cs_full[T - 1:T, :]
    xdt = xm * dt_full
    ecs = jnp.exp(cs_full)
    xdte_b = (xdt * jnp.exp(cs_last - cs_full)).astype(BF16)
    xdt_b = xdt.astype(BF16)
    chunk_decay = jnp.exp(cs_last)

    rr = lax.broadcasted_iota(jnp.int32, (T, T), 0)
    cc = lax.broadcasted_iota(jnp.int32, (T, T), 1)
    causal = cc <= rr
    lane = lax.broadcasted_iota(jnp.int32, (T, LANES), 1)
    head0 = lane < HEAD

    scores = []
    for gi in range(M2_GROUPS):
        gs = slice(gi * M2_STATE, (gi + 1) * M2_STATE)
        scores.append(_dot_nt(cm_b[:, gs], bm_b[:, gs]))
    ys = []
    for q in range(N_PAIRS):
        gi = (2 * q) // (N_HEADS // M2_GROUPS)
        gs = slice(gi * M2_STATE, (gi + 1) * M2_STATE)
        sl = slice(q * LANES, (q + 1) * LANES)
        yd = []
        for h in (2 * q, 2 * q + 1):
            seg = cs[:, h:h + 1] - cs_t[h:h + 1, :]
            dec = jnp.where(causal, jnp.exp(jnp.minimum(seg, 0.0)), 0.0)
            yd.append(_dot((scores[gi] * dec).astype(BF16), xdt_b[:, sl]))
        st = st_ref[q]
        y_off = _dot(cm_b[:, gs], st.astype(BF16)) * ecs[:, sl]
        st_ref[q] = st * chunk_decay[:, sl] + _dot_tn(bm_b[:, gs], xdte_b[:, sl])
        ys.append(jnp.where(head0, yd[0], yd[1]) + y_off)
    y = jnp.concatenate(ys, axis=1) + xm * dfull_ref[...]
    y = y * _silu(z_ref[...])
    ms = jnp.mean(y * y, axis=-1, keepdims=True)
    y_ref[...] = y * lax.rsqrt(ms + NORM_EPS) * ng_ref[...]


def _mamba(proj, p):
    L = proj.shape[0]
    T = M2_T
    hb = T // SUBLANES
    tri = jnp.tril(jnp.ones((T, T), BF16))
    consts = [p['conv_w'], p['conv_b'], p['dt_bias'], p['a_log'], p['d_full'], p['norm_g'], p['expand'], tri]
    specs = [
        pl.BlockSpec((T, M2_XBC), lambda i: (i, COL_XBC // M2_XBC)),
        pl.BlockSpec((SUBLANES, M2_XBC), lambda i: (jnp.maximum(i * hb - 1, 0), COL_XBC // M2_XBC)),
        pl.BlockSpec((T, BW), lambda i: (i, COL_Z // BW)),
        pl.BlockSpec((T, LANES), lambda i: (i, COL_MISC // LANES)),
    ] + [pl.BlockSpec(c.shape, lambda i: (0, 0)) for c in consts]
    return pl.pallas_call(
        _mamba_kernel,
        grid=(L // T,),
        in_specs=specs,
        out_specs=pl.BlockSpec((T, BW), lambda i: (i, 0)),
        out_shape=jax.ShapeDtypeStruct((L, BW), F32),
        scratch_shapes=[pltpu.VMEM((N_PAIRS, M2_STATE, LANES), F32)],
        compiler_params=pltpu.CompilerParams(
            dimension_semantics=("arbitrary",), vmem_limit_bytes=VMEM_LIMIT),
    )(proj, proj, proj, proj, *consts)


def _out_kernel(h_ref, ys5_ref, gs5_ref, yrw_ref, ym2_ref, wglu_ref, bglu_ref, wo_ref, fg_ref, o_ref, *, final):
    ys = _gelu_tanh(ys5_ref[...])
    ys = ys * _sigmoid(_dot(ys.astype(BF16), wglu_ref[...]) + bglu_ref[...])
    ys = ys * _silu(gs5_ref[...])
    acc = _dot(ys.astype(BF16), wo_ref[0:BW, :])
    acc = acc + _dot(yrw_ref[...].astype(BF16), wo_ref[BW:2 * BW, :])
    acc = acc + _dot(ym2_ref[...].astype(BF16), wo_ref[2 * BW:3 * BW, :])
    hn = h_ref[...] + acc
    if final:
        ms = jnp.mean(hn * hn, axis=-1, keepdims=True)
        hn = hn * lax.rsqrt(ms + NORM_EPS) * fg_ref[...]
    o_ref[...] = hn


def _out(h, ys5, proj, yrw, ym2, w_glu, b_glu, w_out, final_g, final):
    L = h.shape[0]
    TM = OUT_TM
    tok = pl.BlockSpec((TM, BW), lambda i: (i, 0))
    return pl.pallas_call(
        functools.partial(_out_kernel, final=final),
        grid=(L // TM,),
        in_specs=[
            pl.BlockSpec((TM, D_MODEL), lambda i: (i, 0)),
            tok,
            pl.BlockSpec((TM, BW), lambda i: (i, COL_GS5 // BW)),
            tok,
            tok,
            pl.BlockSpec((BW, BW), lambda i: (0, 0)),
            pl.BlockSpec((1, BW), lambda i: (0, 0)),
            pl.BlockSpec((3 * BW, D_MODEL), lambda i: (0, 0)),
            pl.BlockSpec((1, D_MODEL), lambda i: (0, 0)),
        ],
        out_specs=pl.BlockSpec((TM, D_MODEL), lambda i: (i, 0)),
        out_shape=jax.ShapeDtypeStruct((L, D_MODEL), F32),
        compiler_params=pltpu.CompilerParams(
            dimension_semantics=("parallel",), vmem_limit_bytes=VMEM_LIMIT),
    )(h, ys5, proj, yrw, ym2, w_glu, b_glu, w_out, final_g)


def _s5_state_tables(log_dt, a_re, a_im, b_re, b_im, c_re, c_im):
    T = S5_CT
    G, P = a_re.shape
    dt = jnp.exp(log_dt.astype(F32))[:, None]
    ar, ai = a_re.astype(F32), a_im.astype(F32)
    mag = jnp.exp(dt * ar)
    abar_re, abar_im = mag * jnp.cos(dt * ai), mag * jnp.sin(dt * ai)
    den = ar * ar + ai * ai
    f_re = ((abar_re - 1.0) * ar + abar_im * ai) / den
    f_im = (abar_im * ar - (abar_re - 1.0) * ai) / den
    br, bi = b_re.astype(F32), b_im.astype(F32)
    bb_re = f_re[..., None] * br - f_im[..., None] * bi
    bb_im = f_re[..., None] * bi + f_im[..., None] * br

    def powers(e):
        e = jnp.asarray(e, F32)[:, None, None]
        m = jnp.exp(e * (dt * ar)[None])
        ang = e * (dt * ai)[None]
        return (m * jnp.cos(ang)).reshape(-1, G * P), (m * jnp.sin(ang)).reshape(-1, G * P)

    pos = np.arange(T, dtype=np.float32)
    half = T / 2
    zr, zi = powers(half - pos)
    sr, si = powers(pos - half)
    cr, ci = powers(np.array([half + 1.0], np.float32))
    tiles = G // S5_TILE_G
    eye = jnp.eye(S5_TILE_G, dtype=F32)

    def b_side(bb):
        t = bb.reshape(tiles, S5_TILE_G, P, S5_GROUP).transpose(0, 1, 3, 2)
        return (t[:, :, :, None, :] * eye[None, :, None, :, None]).reshape(tiles, LANES, S5_TILE_G * P)

    def c_side(cc):
        t = cc.astype(F32).reshape(tiles, S5_TILE_G, S5_GROUP, P).transpose(0, 1, 3, 2)
        return (t[:, :, :, None, :] * eye[None, :, None, :, None]).reshape(tiles, S5_TILE_G * P, LANES)

    wb = jnp.concatenate([b_side(bb_re), b_side(bb_im)], axis=-1).astype(BF16)
    return dict(wb=wb, wc_re=c_side(c_re).astype(BF16), wc_im=c_side(-c_im.astype(F32)).astype(BF16),
                zr=zr, zi=zi, sr=sr, si=si, car=jnp.concatenate([cr, ci], axis=0))


def _s5out_kernel(h_ref, u_ref, gs5_ref, yrw_ref, ym2_ref, wb_ref, wcr_ref, wci_ref, zr_ref, zi_ref, sr_ref,
                  si_ref, car_ref, d_ref, tri_ref, wglu_ref, bglu_ref, wo_ref, fg_ref, o_ref,
                  st_ref, sre_ref, sim_ref, *, final):
    T = S5_CT
    NS = S5_GROUPS * S5_STATE
    half = NS // (S5_GROUPS // S5_TILE_G)

    @pl.when(pl.program_id(0) == 0)
    def _():
        st_ref[...] = jnp.zeros_like(st_ref)

    u = u_ref[...]
    ub = u.astype(BF16)
    tiles = S5_GROUPS // S5_TILE_G
    bu = [_dot(ub[:, j * LANES:(j + 1) * LANES], wb_ref[j]) for j in range(tiles)]
    bu_re = jnp.concatenate([b[:, :half] for b in bu], axis=1)
    bu_im = jnp.concatenate([b[:, half:] for b in bu], axis=1)
    zr, zi, sr, si = zr_ref[...], zi_ref[...], sr_ref[...], si_ref[...]
    car_re, car_im = car_ref[0:1, :], car_ref[1:2, :]
    p_re, p_im = st_ref[0:1, :], st_ref[1:2, :]
    tri = tri_ref[...]
    for c in range(OUT_TM // T):
        rows = slice(c * T, (c + 1) * T)
        b_re, b_im = bu_re[rows], bu_im[rows]
        z = jnp.concatenate([zr * b_re - zi * b_im, zr * b_im + zi * b_re], axis=1).astype(BF16)
        w = _dot(tri, z)
        w_re = w[:, :NS] + (car_re * p_re - car_im * p_im)
        w_im = w[:, NS:] + (car_re * p_im + car_im * p_re)
        s_re = sr * w_re - si * w_im
        s_im = sr * w_im + si * w_re
        p_re, p_im = s_re[T - 1:T, :], s_im[T - 1:T, :]
        sre_ref[rows, :] = s_re.astype(BF16)
        sim_ref[rows, :] = s_im.astype(BF16)
    st_ref[0:1, :] = p_re
    st_ref[1:2, :] = p_im
    ys = jnp.concatenate(
        [_dot(sre_ref[:, j * half:(j + 1) * half], wcr_ref[j]) + _dot(sim_ref[:, j * half:(j + 1) * half], wci_ref[j])
         for j in range(tiles)], axis=1) + d_ref[...] * u
    ys = _gelu_tanh(ys)
    ys = ys * _sigmoid(_dot(ys.astype(BF16), wglu_ref[...]) + bglu_ref[...])
    ys = ys * _silu(gs5_ref[...])
    acc = _dot(ys.astype(BF16), wo_ref[0:BW, :])
    acc = acc + _dot(yrw_ref[...].astype(BF16), wo_ref[BW:2 * BW, :])
    acc = acc + _dot(ym2_ref[...].astype(BF16), wo_ref[2 * BW:3 * BW, :])
    hn = h_ref[...] + acc
    if final:
        ms = jnp.mean(hn * hn, axis=-1, keepdims=True)
        hn = hn * lax.rsqrt(ms + NORM_EPS) * fg_ref[...]
    o_ref[...] = hn


def _s5out(h, proj, yrw, ym2, tb, d_row, w_glu, b_glu, w_out, final_g, final):
    L = h.shape[0]
    TM = OUT_TM
    NS = S5_GROUPS * S5_STATE
    tri = jnp.tril(jnp.ones((S5_CT, S5_CT), BF16))
    tok = pl.BlockSpec((TM, BW), lambda i: (i, 0))
    consts = [tb['wb'], tb['wc_re'], tb['wc_im'], tb['zr'], tb['zi'], tb['sr'], tb['si'], tb['car'],
              d_row, tri, w_glu, b_glu, w_out, final_g]
    const_specs = [pl.BlockSpec(c.shape, (lambda i: (0, 0, 0)) if c.ndim == 3 else (lambda i: (0, 0)))
                   for c in consts]
    return pl.pallas_call(
        functools.partial(_s5out_kernel, final=final),
        grid=(L // TM,),
        in_specs=[
            pl.BlockSpec((TM, D_MODEL), lambda i: (i, 0)),
            pl.BlockSpec((TM, BW), lambda i: (i, COL_U // BW)),
            pl.BlockSpec((TM, BW), lambda i: (i, COL_GS5 // BW)),
            tok,
            tok,
        ] + const_specs,
        out_specs=pl.BlockSpec((TM, D_MODEL), lambda i: (i, 0)),
        out_shape=jax.ShapeDtypeStruct((L, D_MODEL), F32),
        scratch_shapes=[pltpu.VMEM((SUBLANES, NS), F32), pltpu.VMEM((TM, NS), BF16), pltpu.VMEM((TM, NS), BF16)],
        compiler_params=pltpu.CompilerParams(
            dimension_semantics=("arbitrary",), vmem_limit_bytes=VMEM_LIMIT),
    )(h, proj, proj, yrw, ym2, *consts)


def _reorder_w_in(w, w_vmix):
    o = np.cumsum([0, BW, BW, BW, BW, BW, LORA_W, LORA_A, BW, M2_XBC, N_HEADS, BW])
    u, gs5, r, k, v, wd, ad, g, xbc, dt, z = (w[:, o[i]:o[i + 1]] for i in range(11))
    vm = w_vmix if w_vmix is not None else jnp.zeros((w.shape[0], LORA_V), w.dtype)
    pad = jnp.zeros((w.shape[0], LANES - N_HEADS - LORA_V), w.dtype)
    return jnp.concatenate([r, k, v, g, u, gs5, xbc, z, wd, ad, dt, vm, pad], axis=1).astype(BF16)


def _pad_rows(w, start, total=LANES):
    return jnp.pad(w.astype(F32), ((start, total - start - w.shape[0]), (0, 0)))


def _row(a):
    return a.astype(F32).reshape(1, -1)


def kernel(x, norm_g, w_in, w_in_vmix, s5_log_dt, s5_a_re, s5_a_im, s5_b_re, s5_b_im, s5_c_re, s5_c_im, s5_d, s5_w_glu, s5_b_glu, rwkv_mu, rwkv_w_up, rwkv_w0, rwkv_a_up, rwkv_a0, rwkv_k_k, rwkv_k_a, rwkv_r_k, rwkv_ln_g, rwkv_ln_b, rwkv_vmix_mu, rwkv_v_up, rwkv_v0, m2_conv_w, m2_conv_b, m2_dt_bias, m2_a_log, m2_d, m2_norm_g, w_out, final_norm_g):
    bsz, L, d = x.shape
    assert bsz == 1 and d == D_MODEL
    assert L % max(PROJ_TM, OUT_TM, RW_T * RW_NB, M2_T, S5_T * SUBLANES) == 0
    depth = w_in.shape[0]
    h = x.reshape(L, d).astype(F32)
    expand = jnp.pad(jnp.repeat(jnp.eye(N_HEADS, dtype=F32), HEAD, axis=1), ((0, LANES - N_HEADS), (0, 0)))
    v_first = None
    for i in range(depth):
        w_cat = _reorder_w_in(w_in[i], w_in_vmix[i - 1] if i > 0 else None)
        proj = _proj(h, _row(norm_g[i]), w_cat)

        s5_tb = _s5_state_tables(s5_log_dt[i], s5_a_re[i], s5_a_im[i], s5_b_re[i], s5_b_im[i],
                                 s5_c_re[i], s5_c_im[i])

        mu = rwkv_mu[i].astype(F32)
        mo = np.cumsum([0, BW, BW, BW, LORA_W, LORA_A, BW])
        mr, mk, mv, mwd, mad, mg = (mu[mo[j]:mo[j + 1]] for j in range(6))
        rp = {
            'mu_rw': jnp.concatenate([mr, mk, mv, mg]).reshape(1, -1),
            'mu_wa': jnp.concatenate([mwd, mad]).reshape(1, -1),
            'w_up': _pad_rows(rwkv_w_up[i], 0).astype(BF16), 'w0': _row(rwkv_w0[i]),
            'a_up': _pad_rows(rwkv_a_up[i], LORA_W).astype(BF16), 'a0': _row(rwkv_a0[i]),
            'k_k': _row(rwkv_k_k[i]), 'k_a': _row(rwkv_k_a[i]), 'r_k': _row(rwkv_r_k[i]),
            'ln_g': _row(rwkv_ln_g[i]), 'ln_b': _row(rwkv_ln_b[i]),
        }
        if i > 0:
            rp['mu_misc'] = jnp.pad(rwkv_vmix_mu[i - 1].astype(F32),
                                    (N_HEADS, LANES - N_HEADS - LORA_V)).reshape(1, -1)
            rp['v_up'] = _pad_rows(rwkv_v_up[i - 1], N_HEADS).astype(BF16)
            rp['v0'] = _row(rwkv_v0[i - 1])
        y_rw, v_first = _rwkv(proj, rp, v_first)

        mp = {
            'conv_w': m2_conv_w[i].astype(F32), 'conv_b': _row(m2_conv_b[i]),
            'dt_bias': jnp.pad(m2_dt_bias[i].astype(F32), (0, LANES - N_HEADS)).reshape(1, -1),
            'a_log': jnp.pad(m2_a_log[i].astype(F32), (0, LANES - N_HEADS)).reshape(1, -1),
            'd_full': jnp.repeat(m2_d[i].astype(F32), HEAD).reshape(1, -1),
            'norm_g': _row(m2_norm_g[i]), 'expand': expand.astype(BF16),
        }
        y_m2 = _mamba(proj, mp)

        h = _s5out(h, proj, y_rw, y_m2, s5_tb, _row(s5_d[i]), s5_w_glu[i].astype(BF16), _row(s5_b_glu[i]),
                   w_out[i].astype(BF16), _row(final_norm_g), final=(i == depth - 1))
    return h.reshape(bsz, L, d).astype(x.dtype)
```

```python
import functools
import math

import jax
import jax.numpy as jnp
import numpy as np
from jax import lax
from jax.experimental import pallas as pl
from jax.experimental.pallas import tpu as pltpu

F32 = jnp.float32
BF16 = jnp.bfloat16
HI = lax.Precision.HIGHEST

D_MODEL = 1024
BW = 512
S5_GROUP = 16
S5_GROUPS = 32
S5_STATE = 64
HEAD = 64
N_HEADS = 8
N_PAIRS = N_HEADS // 2
LORA_W = 64
LORA_A = 64
LORA_V = 32
GN_EPS = 64e-5
M2_GROUPS = 2
M2_STATE = 128
M2_CONV = 4
M2_XBC = 1024
NORM_EPS = 1e-5

LANES = 128
SUBLANES = 8

COL_RW = 0
COL_U = 2048
COL_GS5 = 2560
COL_XBC = 3072
COL_Z = 4096
COL_WA = 4608
COL_MISC = 4736
PROJ_PAD = 4864

PROJ_TM = 256
PROJ_TN = 512
S5_CT = 64
S5_TILE_G = LANES // S5_GROUP
RW_T = 64
RW_NB = 4
RW_GC = 2
MIX_TB = RW_T * RW_NB
M2_T = 128
M2_TICK = 2
OUT_TM = 256
VMEM_LIMIT = 48 * 1024 * 1024


def _dot(a, b, prec=None):
    return jnp.dot(a, b, precision=prec, preferred_element_type=F32)


def _dot_nt(a, b, prec=None):
    return lax.dot_general(a, b, (((1,), (1,)), ((), ())), precision=prec, preferred_element_type=F32)


def _dot_tn(a, b, prec=None):
    return lax.dot_general(a, b, (((0,), (0,)), ((), ())), precision=prec, preferred_element_type=F32)


def _split3(x):
    hi = x.astype(BF16)
    r1 = x - hi.astype(F32)
    mid = r1.astype(BF16)
    lo = (r1 - mid.astype(F32)).astype(BF16)
    return hi, mid, lo


def _dot_split_rhs(m, x):
    hi, mid, lo = _split3(x)
    return _dot(m, hi) + _dot(m, mid) + _dot(m, lo)


def _dot_split_lhs(x, m):
    hi, mid, lo = _split3(x)
    return _dot(hi, m) + _dot(mid, m) + _dot(lo, m)


def _b16(x):
    return x.astype(BF16)


def _sigmoid(x):
    return 1.0 / (1.0 + jnp.exp(-x))


def _silu(x):
    return x * _sigmoid(x)


def _softplus(x):
    return jnp.maximum(x, 0.0) + jnp.log1p(jnp.exp(-jnp.abs(x)))


def _gelu_tanh(x):
    c = math.sqrt(2.0 / math.pi)
    return 0.5 * x * (1.0 + jnp.tanh(c * (x + 0.044715 * (x * x * x))))


def _prev_rows(x, halo, is_first):
    rolled = pltpu.roll(x, 1, axis=0)
    last = jnp.where(is_first, 0.0, halo[SUBLANES - 1:SUBLANES, :])
    row = lax.broadcasted_iota(jnp.int32, x.shape, 0)
    return jnp.where(row == 0, last, rolled)


def _proj_kernel(h_ref, g_ref, w_ref, o_ref):
    x = h_ref[...]
    ms = jnp.mean(x * x, axis=-1, keepdims=True)
    hn = (x * lax.rsqrt(ms + NORM_EPS) * g_ref[...]).astype(BF16)
    for j in range(0, PROJ_PAD, PROJ_TN):
        w = min(PROJ_TN, PROJ_PAD - j)
        o_ref[:, j:j + w] = _dot(hn, w_ref[:, j:j + w])


def _proj(h, g, w):
    L = h.shape[0]
    return pl.pallas_call(
        _proj_kernel,
        grid=(L // PROJ_TM,),
        in_specs=[
            pl.BlockSpec((PROJ_TM, D_MODEL), lambda i: (i, 0)),
            pl.BlockSpec((1, D_MODEL), lambda i: (0, 0)),
            pl.BlockSpec((D_MODEL, PROJ_PAD), lambda i: (0, 0)),
        ],
        out_specs=pl.BlockSpec((PROJ_TM, PROJ_PAD), lambda i: (i, 0)),
        out_shape=jax.ShapeDtypeStruct((L, PROJ_PAD), F32),
        compiler_params=pltpu.CompilerParams(
            dimension_semantics=("parallel",), vmem_limit_bytes=VMEM_LIMIT),
    )(h, g, w)


def _mamba_stages(first, xbc_ref, xbcp_ref, z_ref, misc_ref, cw_ref, cb_ref, dtb_ref, alog_ref, dfull_ref,
                  ng_ref, expand_ref, tri_ref, y_ref, st_ref):
    T = M2_T
    x = xbc_ref[...]
    halo = jnp.where(first, 0.0, xbcp_ref[...])
    row8 = lax.broadcasted_iota(jnp.int32, halo.shape, 0)
    acc = x * cw_ref[M2_CONV - 1:M2_CONV, :] + cb_ref[...]
    yield
    for j in range(1, M2_CONV):
        xr = pltpu.roll(x, j, axis=0)
        head = jnp.where(row8 < j, pltpu.roll(halo, j, axis=0), xr[:SUBLANES])
        xj = jnp.concatenate([head, xr[SUBLANES:]], axis=0)
        acc = acc + xj * cw_ref[M2_CONV - 1 - j:M2_CONV - j, :]
        yield
    xbc = _silu(acc)
    xm = xbc[:, :BW]
    bm_b = _b16(xbc[:, BW:BW + M2_GROUPS * M2_STATE])
    cm_b = _b16(xbc[:, BW + M2_GROUPS * M2_STATE:])
    yield
    dt = _softplus(misc_ref[...] + dtb_ref[...])
    da = dt * (-jnp.exp(alog_ref[...]))
    expand = expand_ref[...]
    xdt = xm * _dot_split_lhs(dt, expand)
    xdt_b = _b16(xdt)
    gate = _silu(z_ref[...])
    yield

    rr = lax.broadcasted_iota(jnp.int32, (T, T), 0)
    cc = lax.broadcasted_iota(jnp.int32, (T, T), 1)
    causal = cc <= rr
    head0 = lax.broadcasted_iota(jnp.int32, (T, LANES), 1) < HEAD
    states = [st_ref[q] for q in range(N_PAIRS)]
    for ch in range(x.shape[0] // T):
        rows = slice(ch * T, (ch + 1) * T)
        cs = _dot_split_rhs(tri_ref[...], da[rows])
        cs_t = cs.T
        cs_full = _dot_split_lhs(cs, expand)
        cs_last = cs_full[T - 1:T, :]
        ecs = jnp.exp(cs_full)
        xdte_b = _b16(xdt[rows] * jnp.exp(cs_last - cs_full))
        chunk_decay = jnp.exp(cs_last)
        yield
        scores = [_dot_nt(cm_b[rows, gi * M2_STATE:(gi + 1) * M2_STATE], bm_b[rows, gi * M2_STATE:(gi + 1) * M2_STATE])
                  for gi in range(M2_GROUPS)]
        yield
        ys = []
        for q in range(N_PAIRS):
            gi = (2 * q) // (N_HEADS // M2_GROUPS)
            gs = slice(gi * M2_STATE, (gi + 1) * M2_STATE)
            sl = slice(q * LANES, (q + 1) * LANES)
            yd = []
            for h in (2 * q, 2 * q + 1):
                seg = cs[:, h:h + 1] - cs_t[h:h + 1, :]
                dec = jnp.where(causal, jnp.exp(jnp.minimum(seg, 0.0)), 0.0)
                yd.append(_dot(_b16(scores[gi] * dec), xdt_b[rows, sl]))
            st = states[q]
            y_off = _dot(cm_b[rows, gs], _b16(st)) * ecs[:, sl]
            states[q] = st * chunk_decay[:, sl] + _dot_tn(bm_b[rows, gs], xdte_b[:, sl])
            ys.append(jnp.where(head0, yd[0], yd[1]) + y_off)
            yield
        y = jnp.concatenate(ys, axis=1) + xm[rows] * dfull_ref[...]
        y = y * gate[rows]
        ms = jnp.mean(y * y, axis=-1, keepdims=True)
        y_ref[rows, :] = y * lax.rsqrt(ms + NORM_EPS) * ng_ref[...]
        yield
    for q in range(N_PAIRS):
        st_ref[q] = states[q]


def _rwkv_body(first, tick, refs, has_vmix):
    if has_vmix:
        (rw_ref, rwp_ref, wa_ref, wap_ref, misc_ref, miscp_ref, vfirst_ref,
         mu_rw_ref, mu_wa_ref, mu_misc_ref, wup_ref, w0_ref, aup_ref, a0_ref, kk_ref, ka_ref, rk_ref,
         lng_ref, lnb_ref, vup_ref, v0_ref, tri_ref, y_ref, s_ref) = refs
    else:
        (rw_ref, rwp_ref, wa_ref, wap_ref,
         mu_rw_ref, mu_wa_ref, wup_ref, w0_ref, aup_ref, a0_ref, kk_ref, ka_ref, rk_ref,
         lng_ref, lnb_ref, tri_ref, y_ref, vout_ref, s_ref) = refs
    T = RW_T
    TB = MIX_TB
    b16 = _b16

    def lerp(ref, halo_ref, mu_ref):
        x = ref[...]
        return x + (_prev_rows(x, halo_ref[...], first) - x) * mu_ref[...]

    xs = lerp(rw_ref, rwp_ref, mu_rw_ref)
    r, k, v, g = (xs[:, i * BW:(i + 1) * BW] for i in range(4))
    wa = lerp(wa_ref, wap_ref, mu_wa_ref)
    w_log = -_softplus(-(w0_ref[...] + _dot(b16(jnp.tanh(wa)), wup_ref[...]))) - 0.5
    lw = -jnp.exp(w_log)
    a = _sigmoid(a0_ref[...] + _dot(b16(wa), aup_ref[...]))
    if has_vmix:
        vd = lerp(misc_ref, miscp_ref, mu_misc_ref)
        mix = _sigmoid(v0_ref[...] + _dot(b16(vd), vup_ref[...]))
        v = v + (vfirst_ref[...] - v) * mix
    else:
        vout_ref[...] = v
    kk = k * kk_ref[...]
    kmod = k * (1.0 + (a - 1.0) * ka_ref[...])
    cums = _dot_split_rhs(tri_ref[...], lw)
    cl, rv = cums[:TB], cums[TB:]
    g_incl = jnp.exp(cl)
    g_inv = jnp.exp(-cl)
    g_end = jnp.exp(rv)
    rkr = r * kmod * rk_ref[...]

    def per_head_sum(x):
        h0 = lax.broadcasted_iota(jnp.int32, x.shape, 1) < HEAD
        f0 = h0.astype(F32)
        s0 = jnp.sum(jnp.where(h0, x, 0.0), axis=-1, keepdims=True)
        s1 = jnp.sum(jnp.where(h0, 0.0, x), axis=-1, keepdims=True)
        return s0 * f0 + s1 * (1.0 - f0)

    inv, bonus_w = [], []
    for q in range(N_PAIRS):
        sl = slice(q * LANES, (q + 1) * LANES)
        sq = kk[:, sl] * kk[:, sl]
        inv.append(1.0 / jnp.maximum(jnp.sqrt(per_head_sum(sq)), 1e-12))
        bonus_w.append(per_head_sum(rkr[:, sl]))
    kkn = kk * jnp.concatenate(inv, axis=1)
    bonus = jnp.concatenate(bonus_w, axis=1) * v
    kkna = kkn * a
    na_b = b16(-kkn * jnp.exp(cl - lw))
    nb_b = b16(kkna * g_inv)
    nbe_b = b16(kkna * g_end)
    pk_b = b16(kmod * g_inv)
    pke_b = b16(kmod * g_end)
    pr = r * g_incl
    pr_b = b16(pr)
    v_b = b16(v)
    gated = _silu(g)

    m0 = (lax.broadcasted_iota(jnp.int32, (1, LANES), 1) < HEAD).astype(BF16)
    m1 = 1.0 - m0
    row_w = lax.broadcasted_iota(jnp.int32, (T, LANES), 0)
    col_w = lax.broadcasted_iota(jnp.int32, (T, LANES), 1) % HEAD
    strict = col_w < row_w
    incl = col_w <= row_w
    eye_w = (col_w == row_w).astype(F32)
    rr = lax.broadcasted_iota(jnp.int32, (LANES, LANES), 0)
    cc = lax.broadcasted_iota(jnp.int32, (LANES, LANES), 1)
    same_head = (rr < HEAD) == (cc < HEAD)

    def stack(x):
        return jnp.concatenate([x * m0, x * m1], axis=0)

    def rows(c):
        return slice(c * T, (c + 1) * T)

    def lanes(q):
        return slice(q * LANES, (q + 1) * LANES)

    state = [s_ref[q] for q in range(N_PAIRS)]
    for c0 in range(0, RW_NB, RW_GC):
        units = [(c, q) for c in range(c0, c0 + RW_GC) for q in range(N_PAIRS)]
        a_n = [na_b[rows(c), lanes(q)] for c, q in units]
        r_n = [pr_b[rows(c), lanes(q)] for c, q in units]
        v_n = [v_b[rows(c), lanes(q)] for c, q in units]
        be_n = [nbe_b[rows(c), lanes(q)] for c, q in units]
        ke_n = [pke_b[rows(c), lanes(q)] for c, q in units]
        b_st = [stack(nb_b[rows(c), lanes(q)]) for c, q in units]
        k_st = [stack(pk_b[rows(c), lanes(q)]) for c, q in units]
        a_st = [stack(x) for x in a_n]
        v_st = [stack(x) for x in v_n]
        tick()
        aa = [_dot_nt(jnp.concatenate([a_, r_], axis=0), jnp.concatenate([b_, k_], axis=0))
              for a_, r_, b_, k_ in zip(a_n, r_n, b_st, k_st)]
        tick()
        a_ak = [b16(jnp.where(strict, x[:T, LANES:], 0.0)) for x in aa]
        a_rb = [b16(jnp.where(incl, x[T:, :LANES], 0.0)) for x in aa]
        a_rk = [b16(jnp.where(incl, x[T:, LANES:], 0.0)) for x in aa]
        p = [jnp.where(strict, x[:T, :LANES], 0.0) for x in aa]
        minv = [eye_w + x for x in p]
        tick()
        for _ in range(int(math.log2(T)) - 1):
            p_b = [b16(x) for x in p]
            p = [_dot(x, stack(x)) for x in p_b]
            tick()
            minv = [m + _dot(b16(x), stack(b16(m))) for x, m in zip(p, minv)]
            tick()
        akv = [_dot(x, y) for x, y in zip(a_ak, v_st)]
        tick()
        wu_b = [b16(_dot(b16(m), jnp.concatenate([a_, stack(b16(x))], axis=1)))
                for m, a_, x in zip(minv, a_st, akv)]
        tick()
        p_mat = [b16(jnp.where(same_head, _dot_tn(w[:, :LANES], be), 0.0)) for w, be in zip(wu_b, be_n)]
        tick()
        z_mat = [jnp.where(same_head,
                           _dot_tn(jnp.concatenate([w[:, LANES:], vn], axis=0), jnp.concatenate([be, ke], axis=0)),
                           0.0)
                 for w, vn, be, ke in zip(wu_b, v_n, be_n, ke_n)]
        tick()
        q_mat = [b16(pr[rows(c), lanes(q)] + _dot(arb, stack(w[:, :LANES])))
                 for (c, q), arb, w in zip(units, a_rb, wu_b)]
        tick()
        y0 = [_dot(jnp.concatenate([arb, ark], axis=1), jnp.concatenate([stack(w[:, LANES:]), vs], axis=0))
              for arb, ark, w, vs in zip(a_rb, a_rk, wu_b, v_st)]
        tick()
        ys = []
        for i, (c, q) in enumerate(units):
            s = state[q]
            s_b = b16(s)
            ys.append(_dot_nt(q_mat[i], s_b) + y0[i])
            state[q] = s * g_incl[(c + 1) * T - 1:(c + 1) * T, lanes(q)] + _dot(s_b, p_mat[i]) + z_mat[i]
        tick()
        for i, (c, q) in enumerate(units):
            y = ys[i]
            mean = per_head_sum(y) * (1.0 / HEAD)
            cen = y - mean
            var = per_head_sum(cen * cen) * (1.0 / HEAD)
            yn = cen * lax.rsqrt(var + GN_EPS)
            y_ref[rows(c), lanes(q)] = ((yn * lng_ref[:, lanes(q)] + lnb_ref[:, lanes(q)]
                                         + bonus[rows(c), lanes(q)]) * gated[rows(c), lanes(q)])
        tick()
    for q in range(N_PAIRS):
        s_ref[q] = state[q]


N_MAMBA_IN = 12


def _mixers_kernel(*refs, has_vmix):
    n_rw_in = 22 if has_vmix else 16
    rw_in, m_in = refs[:n_rw_in], refs[n_rw_in:n_rw_in + N_MAMBA_IN]
    rest = refs[n_rw_in + N_MAMBA_IN:]
    if has_vmix:
        y_rw_ref, y_m2_ref, s_ref, st_ref = rest
        rw_refs = (*rw_in, y_rw_ref, s_ref)
    else:
        y_rw_ref, vout_ref, y_m2_ref, s_ref, st_ref = rest
        rw_refs = (*rw_in, y_rw_ref, vout_ref, s_ref)
    first = pl.program_id(0) == 0

    @pl.when(first)
    def _():
        s_ref[...] = jnp.zeros_like(s_ref)
        st_ref[...] = jnp.zeros_like(st_ref)

    mamba = _mamba_stages(first, *m_in, y_m2_ref, st_ref)
    calls = [0]

    def tick():
        calls[0] += 1
        if calls[0] % M2_TICK == 0:
            next(mamba, None)

    _rwkv_body(first, tick, rw_refs, has_vmix)
    for _ in mamba:
        pass


def _mixers(proj, rp, mp, v_first):
    L = proj.shape[0]
    TB = MIX_TB
    has_vmix = v_first is not None
    hb = TB // SUBLANES

    def cur(width, col):
        return pl.BlockSpec((TB, width), lambda i: (i, col // width))

    def halo(width, col):
        return pl.BlockSpec((SUBLANES, width), lambda i: (jnp.maximum(i * hb - 1, 0), col // width))

    def full(a):
        return pl.BlockSpec(a.shape, lambda i: (0, 0))

    pos = np.arange(TB)
    same = (pos[:, None] // RW_T) == (pos[None, :] // RW_T)
    rw_tri = jnp.asarray(np.concatenate([same & (pos[None, :] <= pos[:, None]),
                                         same & (pos[None, :] > pos[:, None])], axis=0), BF16)
    args = [proj, proj, proj, proj]
    specs = [cur(4 * BW, COL_RW), halo(4 * BW, COL_RW), cur(LANES, COL_WA), halo(LANES, COL_WA)]
    if has_vmix:
        args += [proj, proj, v_first]
        specs += [cur(LANES, COL_MISC), halo(LANES, COL_MISC), pl.BlockSpec((TB, BW), lambda i: (i, 0))]
        names = ['mu_rw', 'mu_wa', 'mu_misc', 'w_up', 'w0', 'a_up', 'a0', 'k_k', 'k_a', 'r_k',
                 'ln_g', 'ln_b', 'v_up', 'v0']
    else:
        names = ['mu_rw', 'mu_wa', 'w_up', 'w0', 'a_up', 'a0', 'k_k', 'k_a', 'r_k', 'ln_g', 'ln_b']
    consts = [rp[n] for n in names] + [rw_tri]
    args += consts
    specs += [full(c) for c in consts]
    assert len(args) == (22 if has_vmix else 16)

    m_tri = jnp.tril(jnp.ones((M2_T, M2_T), BF16))
    m_consts = [mp['conv_w'], mp['conv_b'], mp['dt_bias'], mp['a_log'], mp['d_full'], mp['norm_g'],
                mp['expand'], m_tri]
    args += [proj, proj, proj, proj] + m_consts
    specs += [cur(M2_XBC, COL_XBC), halo(M2_XBC, COL_XBC), cur(BW, COL_Z), cur(LANES, COL_MISC)]
    specs += [full(c) for c in m_consts]
    assert len(m_consts) + 4 == N_MAMBA_IN

    tok = pl.BlockSpec((TB, BW), lambda i: (i, 0))
    tok_shape = jax.ShapeDtypeStruct((L, BW), F32)
    n_out = 2 if has_vmix else 3
    res = pl.pallas_call(
        functools.partial(_mixers_kernel, has_vmix=has_vmix),
        grid=(L // TB,),
        in_specs=specs,
        out_specs=[tok] * n_out,
        out_shape=[tok_shape] * n_out,
        scratch_shapes=[pltpu.VMEM((N_PAIRS, LANES, LANES), F32), pltpu.VMEM((N_PAIRS, M2_STATE, LANES), F32)],
        compiler_params=pltpu.CompilerParams(
            dimension_semantics=("arbitrary",), vmem_limit_bytes=VMEM_LIMIT),
    )(*args)
    if has_vmix:
        return res[0], res[1], v_first
    return res[0], res[2], res[1]


def _s5_state_tables(log_dt, a_re, a_im, b_re, b_im, c_re, c_im):
    T = S5_CT
    G, P = a_re.shape
    dt = jnp.exp(log_dt.astype(F32))[:, None]
    ar, ai = a_re.astype(F32), a_im.astype(F32)
    mag = jnp.exp(dt * ar)
    abar_re, abar_im = mag * jnp.cos(dt * ai), mag * jnp.sin(dt * ai)
    den = ar * ar + ai * ai
    f_re = ((abar_re - 1.0) * ar + abar_im * ai) / den
    f_im = (abar_im * ar - (abar_re - 1.0) * ai) / den
    br, bi = b_re.astype(F32), b_im.astype(F32)
    bb_re = f_re[..., None] * br - f_im[..., None] * bi
    bb_im = f_re[..., None] * bi + f_im[..., None] * br

    def powers(e):
        e = jnp.asarray(e, F32)[:, None, None]
        m = jnp.exp(e * (dt * ar)[None])
        ang = e * (dt * ai)[None]
        return (m * jnp.cos(ang)).reshape(-1, G * P), (m * jnp.sin(ang)).reshape(-1, G * P)

    pos = np.arange(T, dtype=np.float32)
    half = T / 2
    zr, zi = powers(half - pos)
    sr, si = powers(pos - half)
    cr, ci = powers(np.array([half + 1.0], np.float32))
    tiles = G // S5_TILE_G
    eye = jnp.eye(S5_TILE_G, dtype=F32)

    def b_side(bb):
        t = bb.reshape(tiles, S5_TILE_G, P, S5_GROUP).transpose(0, 1, 3, 2)
        return (t[:, :, :, None, :] * eye[None, :, None, :, None]).reshape(tiles, LANES, S5_TILE_G * P)

    def c_side(cc):
        t = cc.astype(F32).reshape(tiles, S5_TILE_G, S5_GROUP, P).transpose(0, 1, 3, 2)
        return (t[:, :, :, None, :] * eye[None, :, None, :, None]).reshape(tiles, S5_TILE_G * P, LANES)

    wb = jnp.concatenate([b_side(bb_re), b_side(bb_im)], axis=-1).astype(BF16)
    return dict(wb=wb, wc_re=c_side(c_re).astype(BF16), wc_im=c_side(-c_im.astype(F32)).astype(BF16),
                zr=zr, zi=zi, sr=sr, si=si, car=jnp.concatenate([cr, ci], axis=0))


def _s5out_kernel(h_ref, u_ref, gs5_ref, yrw_ref, ym2_ref, wb_ref, wcr_ref, wci_ref, zr_ref, zi_ref, sr_ref,
                  si_ref, car_ref, d_ref, tri_ref, wglu_ref, bglu_ref, wo_ref, fg_ref, o_ref,
                  st_ref, sre_ref, sim_ref, *, final):
    T = S5_CT
    NS = S5_GROUPS * S5_STATE
    tiles = S5_GROUPS // S5_TILE_G
    half = NS // tiles

    @pl.when(pl.program_id(0) == 0)
    def _():
        st_ref[...] = jnp.zeros_like(st_ref)

    u = u_ref[...]
    ub = u.astype(BF16)
    bu = [_dot(ub[:, j * LANES:(j + 1) * LANES], wb_ref[j]) for j in range(tiles)]
    bu_re = jnp.concatenate([b[:, :half] for b in bu], axis=1)
    bu_im = jnp.concatenate([b[:, half:] for b in bu], axis=1)
    zr, zi, sr, si = zr_ref[...], zi_ref[...], sr_ref[...], si_ref[...]
    car_re, car_im = car_ref[0:1, :], car_ref[1:2, :]
    p_re, p_im = st_ref[0:1, :], st_ref[1:2, :]
    tri = tri_ref[...]
    for c in range(OUT_TM // T):
        rows = slice(c * T, (c + 1) * T)
        b_re, b_im = bu_re[rows], bu_im[rows]
        z = jnp.concatenate([zr * b_re - zi * b_im, zr * b_im + zi * b_re], axis=1).astype(BF16)
        w = _dot(tri, z)
        w_re = w[:, :NS] + (car_re * p_re - car_im * p_im)
        w_im = w[:, NS:] + (car_re * p_im + car_im * p_re)
        s_re = sr * w_re - si * w_im
        s_im = sr * w_im + si * w_re
        p_re, p_im = s_re[T - 1:T, :], s_im[T - 1:T, :]
        sre_ref[rows, :] = s_re.astype(BF16)
        sim_ref[rows, :] = s_im.astype(BF16)
    st_ref[0:1, :] = p_re
    st_ref[1:2, :] = p_im
    ys = jnp.concatenate(
        [_dot(sre_ref[:, j * half:(j + 1) * half], wcr_ref[j]) + _dot(sim_ref[:, j * half:(j + 1) * half], wci_ref[j])
         for j in range(tiles)], axis=1) + d_ref[...] * u
    ys = _gelu_tanh(ys)
    ys = ys * _sigmoid(_dot(ys.astype(BF16), wglu_ref[...]) + bglu_ref[...])
    ys = ys * _silu(gs5_ref[...])
    acc = _dot(ys.astype(BF16), wo_ref[0:BW, :])
    acc = acc + _dot(yrw_ref[...].astype(BF16), wo_ref[BW:2 * BW, :])
    acc = acc + _dot(ym2_ref[...].astype(BF16), wo_ref[2 * BW:3 * BW, :])
    hn = h_ref[...] + acc
    if final:
        ms = jnp.mean(hn * hn, axis=-1, keepdims=True)
        hn = hn * lax.rsqrt(ms + NORM_EPS) * fg_ref[...]
    o_ref[...] = hn


def _s5out(h, proj, yrw, ym2, tb, d_row, w_glu, b_glu, w_out, final_g, final):
    L = h.shape[0]
    TM = OUT_TM
    NS = S5_GROUPS * S5_STATE
    tri = jnp.tril(jnp.ones((S5_CT, S5_CT), BF16))
    tok = pl.BlockSpec((TM, BW), lambda i: (i, 0))
    consts = [tb['wb'], tb['wc_re'], tb['wc_im'], tb['zr'], tb['zi'], tb['sr'], tb['si'], tb['car'],
              d_row, tri, w_glu, b_glu, w_out, final_g]
    const_specs = [pl.BlockSpec(c.shape, (lambda i: (0, 0, 0)) if c.ndim == 3 else (lambda i: (0, 0)))
                   for c in consts]
    return pl.pallas_call(
        functools.partial(_s5out_kernel, final=final),
        grid=(L // TM,),
        in_specs=[
            pl.BlockSpec((TM, D_MODEL), lambda i: (i, 0)),
            pl.BlockSpec((TM, BW), lambda i: (i, COL_U // BW)),
            pl.BlockSpec((TM, BW), lambda i: (i, COL_GS5 // BW)),
            tok,
            tok,
        ] + const_specs,
        out_specs=pl.BlockSpec((TM, D_MODEL), lambda i: (i, 0)),
        out_shape=jax.ShapeDtypeStruct((L, D_MODEL), F32),
        scratch_shapes=[pltpu.VMEM((SUBLANES, NS), F32), pltpu.VMEM((TM, NS), BF16), pltpu.VMEM((TM, NS), BF16)],
        compiler_params=pltpu.CompilerParams(
            dimension_semantics=("arbitrary",), vmem_limit_bytes=VMEM_LIMIT),
    )(h, proj, proj, yrw, ym2, *consts)


def _reorder_w_in(w, w_vmix):
    o = np.cumsum([0, BW, BW, BW, BW, BW, LORA_W, LORA_A, BW, M2_XBC, N_HEADS, BW])
    u, gs5, r, k, v, wd, ad, g, xbc, dt, z = (w[:, o[i]:o[i + 1]] for i in range(11))
    vm = w_vmix if w_vmix is not None else jnp.zeros((w.shape[0], LORA_V), w.dtype)
    pad = jnp.zeros((w.shape[0], LANES - N_HEADS - LORA_V), w.dtype)
    return jnp.concatenate([r, k, v, g, u, gs5, xbc, z, wd, ad, dt, vm, pad], axis=1).astype(BF16)


def _pad_rows(w, start, total=LANES):
    return jnp.pad(w.astype(F32), ((start, total - start - w.shape[0]), (0, 0)))


def _row(a):
    return a.astype(F32).reshape(1, -1)


def kernel(x, norm_g, w_in, w_in_vmix, s5_log_dt, s5_a_re, s5_a_im, s5_b_re, s5_b_im, s5_c_re, s5_c_im, s5_d, s5_w_glu, s5_b_glu, rwkv_mu, rwkv_w_up, rwkv_w0, rwkv_a_up, rwkv_a0, rwkv_k_k, rwkv_k_a, rwkv_r_k, rwkv_ln_g, rwkv_ln_b, rwkv_vmix_mu, rwkv_v_up, rwkv_v0, m2_conv_w, m2_conv_b, m2_dt_bias, m2_a_log, m2_d, m2_norm_g, w_out, final_norm_g):
    bsz, L, d = x.shape
    assert bsz == 1 and d == D_MODEL
    assert L % max(PROJ_TM, OUT_TM, MIX_TB) == 0 and MIX_TB % M2_T == 0 and OUT_TM % S5_CT == 0
    depth = w_in.shape[0]
    h = x.reshape(L, d).astype(F32)
    expand = jnp.pad(jnp.repeat(jnp.eye(N_HEADS, dtype=F32), HEAD, axis=1), ((0, LANES - N_HEADS), (0, 0)))
    v_first = None
    for i in range(depth):
        w_cat = _reorder_w_in(w_in[i], w_in_vmix[i - 1] if i > 0 else None)
        proj = _proj(h, _row(norm_g[i]), w_cat)

        mu = rwkv_mu[i].astype(F32)
        mo = np.cumsum([0, BW, BW, BW, LORA_W, LORA_A, BW])
        mr, mk, mv, mwd, mad, mg = (mu[mo[j]:mo[j + 1]] for j in range(6))
        rp = {
            'mu_rw': jnp.concatenate([mr, mk, mv, mg]).reshape(1, -1),
            'mu_wa': jnp.concatenate([mwd, mad]).reshape(1, -1),
            'w_up': _pad_rows(rwkv_w_up[i], 0).astype(BF16), 'w0': _row(rwkv_w0[i]),
            'a_up': _pad_rows(rwkv_a_up[i], LORA_W).astype(BF16), 'a0': _row(rwkv_a0[i]),
            'k_k': _row(rwkv_k_k[i]), 'k_a': _row(rwkv_k_a[i]), 'r_k': _row(rwkv_r_k[i]),
            'ln_g': _row(rwkv_ln_g[i]), 'ln_b': _row(rwkv_ln_b[i]),
        }
        if i > 0:
            rp['mu_misc'] = jnp.pad(rwkv_vmix_mu[i - 1].astype(F32),
                                    (N_HEADS, LANES - N_HEADS - LORA_V)).reshape(1, -1)
            rp['v_up'] = _pad_rows(rwkv_v_up[i - 1], N_HEADS).astype(BF16)
            rp['v0'] = _row(rwkv_v0[i - 1])
        mp = {
            'conv_w': m2_conv_w[i].astype(F32), 'conv_b': _row(m2_conv_b[i]),
            'dt_bias': jnp.pad(m2_dt_bias[i].astype(F32), (0, LANES - N_HEADS)).reshape(1, -1),
            'a_log': jnp.pad(m2_a_log[i].astype(F32), (0, LANES - N_HEADS)).reshape(1, -1),
            'd_full': jnp.repeat(m2_d[i].astype(F32), HEAD).reshape(1, -1),
            'norm_g': _row(m2_norm_g[i]), 'expand': expand.astype(BF16),
        }
        y_rw, y_m2, v_first = _mixers(proj, rp, mp, v_first)

        s5_tb = _s5_state_tables(s5_log_dt[i], s5_a_re[i], s5_a_im[i], s5_b_re[i], s5_b_im[i],
                                 s5_c_re[i], s5_c_im[i])
        h = _s5out(h, proj, y_rw, y_m2, s5_tb, _row(s5_d[i]), s5_w_glu[i].astype(BF16), _row(s5_b_glu[i]),
                   w_out[i].astype(BF16), _row(final_norm_g), final=(i == depth - 1))
    return h.reshape(bsz, L, d).astype(x.dtype)
```

```python
import functools
import math

import jax
import jax.numpy as jnp
import numpy as np
from jax import lax
from jax.experimental import pallas as pl
from jax.experimental.pallas import tpu as pltpu

F32 = jnp.float32
BF16 = jnp.bfloat16
HI = lax.Precision.HIGHEST

D_MODEL = 1024
BW = 512
S5_GROUP = 16
S5_GROUPS = 32
S5_STATE = 64
HEAD = 64
N_HEADS = 8
N_PAIRS = N_HEADS // 2
LORA_W = 64
LORA_A = 64
LORA_V = 32
GN_EPS = 64e-5
M2_GROUPS = 2
M2_STATE = 128
M2_CONV = 4
M2_XBC = 1024
NORM_EPS = 1e-5

LANES = 128
SUBLANES = 8

COL_RW = 0
COL_U = 2048
COL_GS5 = 2560
COL_XBC = 3072
COL_Z = 4096
COL_WA = 4608
COL_MISC = 4736
PROJ_PAD = 4864

PROJ_TM = 256
PROJ_TN = 512
S5_CT = 64
S5_TILE_G = LANES // S5_GROUP
RW_T = 64
RW_NB = 4
RW_GC = 2
MIX_TB = RW_T * RW_NB
M2_T = 128
M2_TICK = 2
OUT_TM = 256
VMEM_LIMIT = 48 * 1024 * 1024


def _dot(a, b, prec=None):
    return jnp.dot(a, b, precision=prec, preferred_element_type=F32)


def _dot_nt(a, b, prec=None):
    return lax.dot_general(a, b, (((1,), (1,)), ((), ())), precision=prec, preferred_element_type=F32)


def _dot_tn(a, b, prec=None):
    return lax.dot_general(a, b, (((0,), (0,)), ((), ())), precision=prec, preferred_element_type=F32)


def _split3(x):
    hi = x.astype(BF16)
    r1 = x - hi.astype(F32)
    mid = r1.astype(BF16)
    lo = (r1 - mid.astype(F32)).astype(BF16)
    return hi, mid, lo


def _dot_split_rhs(m, x):
    hi, mid, lo = _split3(x)
    return _dot(m, hi) + _dot(m, mid) + _dot(m, lo)


def _dot_split_lhs(x, m):
    hi, mid, lo = _split3(x)
    return _dot(hi, m) + _dot(mid, m) + _dot(lo, m)


def _b16(x):
    return x.astype(BF16)


def _sigmoid(x):
    return 1.0 / (1.0 + jnp.exp(-x))


def _silu(x):
    return x * _sigmoid(x)


def _softplus(x):
    return jnp.maximum(x, 0.0) + jnp.log1p(jnp.exp(-jnp.abs(x)))


def _gelu_tanh(x):
    c = math.sqrt(2.0 / math.pi)
    return 0.5 * x * (1.0 + jnp.tanh(c * (x + 0.044715 * (x * x * x))))


def _prev_rows(x, halo, is_first):
    rolled = pltpu.roll(x, 1, axis=0)
    last = jnp.where(is_first, 0.0, halo[SUBLANES - 1:SUBLANES, :])
    row = lax.broadcasted_iota(jnp.int32, x.shape, 0)
    return jnp.where(row == 0, last, rolled)


def _proj_kernel(h_ref, g_ref, w_ref, o_ref):
    x = h_ref[...]
    ms = jnp.mean(x * x, axis=-1, keepdims=True)
    hn = (x * lax.rsqrt(ms + NORM_EPS) * g_ref[...]).astype(BF16)
    for j in range(0, PROJ_PAD, PROJ_TN):
        w = min(PROJ_TN, PROJ_PAD - j)
        o_ref[:, j:j + w] = _dot(hn, w_ref[:, j:j + w])


def _proj(h, g, w):
    L = h.shape[0]
    return pl.pallas_call(
        _proj_kernel,
        grid=(L // PROJ_TM,),
        in_specs=[
            pl.BlockSpec((PROJ_TM, D_MODEL), lambda i: (i, 0)),
            pl.BlockSpec((1, D_MODEL), lambda i: (0, 0)),
            pl.BlockSpec((D_MODEL, PROJ_PAD), lambda i: (0, 0)),
        ],
        out_specs=pl.BlockSpec((PROJ_TM, PROJ_PAD), lambda i: (i, 0)),
        out_shape=jax.ShapeDtypeStruct((L, PROJ_PAD), F32),
        compiler_params=pltpu.CompilerParams(
            dimension_semantics=("parallel",), vmem_limit_bytes=VMEM_LIMIT),
    )(h, g, w)


def _mamba_stages(first, xbc_ref, xbcp_ref, z_ref, misc_ref, cw_ref, cb_ref, dtb_ref, alog_ref, dfull_ref,
                  ng_ref, expand_ref, tri_ref, y_ref, st_ref):
    T = M2_T
    x = xbc_ref[...]
    halo = jnp.where(first, 0.0, xbcp_ref[...])
    row8 = lax.broadcasted_iota(jnp.int32, halo.shape, 0)
    acc = x * cw_ref[M2_CONV - 1:M2_CONV, :] + cb_ref[...]
    yield
    for j in range(1, M2_CONV):
        xr = pltpu.roll(x, j, axis=0)
        head = jnp.where(row8 < j, pltpu.roll(halo, j, axis=0), xr[:SUBLANES])
        xj = jnp.concatenate([head, xr[SUBLANES:]], axis=0)
        acc = acc + xj * cw_ref[M2_CONV - 1 - j:M2_CONV - j, :]
        yield
    xbc = _silu(acc)
    xm = xbc[:, :BW]
    bm_b = _b16(xbc[:, BW:BW + M2_GROUPS * M2_STATE])
    cm_b = _b16(xbc[:, BW + M2_GROUPS * M2_STATE:])
    yield
    dt = _softplus(misc_ref[...] + dtb_ref[...])
    da = dt * (-jnp.exp(alog_ref[...]))
    expand = expand_ref[...]
    xdt = xm * _dot_split_lhs(dt, expand)
    xdt_b = _b16(xdt)
    gate = _silu(z_ref[...])
    yield

    rr = lax.broadcasted_iota(jnp.int32, (T, T), 0)
    cc = lax.broadcasted_iota(jnp.int32, (T, T), 1)
    causal = cc <= rr
    head0 = lax.broadcasted_iota(jnp.int32, (T, LANES), 1) < HEAD
    states = [st_ref[q] for q in range(N_PAIRS)]
    for ch in range(x.shape[0] // T):
        rows = slice(ch * T, (ch + 1) * T)
        cs = _dot_split_rhs(tri_ref[...], da[rows])
        cs_t = cs.T
        cs_full = _dot_split_lhs(cs, expand)
        cs_last = cs_full[T - 1:T, :]
        ecs = jnp.exp(cs_full)
        xdte_b = _b16(xdt[rows] * jnp.exp(cs_last - cs_full))
        chunk_decay = jnp.exp(cs_last)
        yield
        scores = [_dot_nt(cm_b[rows, gi * M2_STATE:(gi + 1) * M2_STATE], bm_b[rows, gi * M2_STATE:(gi + 1) * M2_STATE])
                  for gi in range(M2_GROUPS)]
        yield
        ys = []
        for q in range(N_PAIRS):
            gi = (2 * q) // (N_HEADS // M2_GROUPS)
            gs = slice(gi * M2_STATE, (gi + 1) * M2_STATE)
            sl = slice(q * LANES, (q + 1) * LANES)
            yd = []
            for h in (2 * q, 2 * q + 1):
                seg = cs[:, h:h + 1] - cs_t[h:h + 1, :]
                dec = jnp.where(causal, jnp.exp(jnp.minimum(seg, 0.0)), 0.0)
                yd.append(_dot(_b16(scores[gi] * dec), xdt_b[rows, sl]))
            st = states[q]
            y_off = _dot(cm_b[rows, gs], _b16(st)) * ecs[:, sl]
            states[q] = st * chunk_decay[:, sl] + _dot_tn(bm_b[rows, gs], xdte_b[:, sl])
            ys.append(jnp.where(head0, yd[0], yd[1]) + y_off)
            yield
        y = jnp.concatenate(ys, axis=1) + xm[rows] * dfull_ref[...]
        y = y * gate[rows]
        ms = jnp.mean(y * y, axis=-1, keepdims=True)
        y_ref[rows, :] = y * lax.rsqrt(ms + NORM_EPS) * ng_ref[...]
        yield
    for q in range(N_PAIRS):
        st_ref[q] = states[q]


def _rwkv_body(first, tick, refs, has_vmix):
    if has_vmix:
        (rw_ref, rwp_ref, wa_ref, wap_ref, misc_ref, miscp_ref, vfirst_ref,
         mu_rw_ref, mu_wa_ref, mu_misc_ref, wup_ref, w0_ref, aup_ref, a0_ref, kk_ref, ka_ref, rk_ref,
         lng_ref, lnb_ref, vup_ref, v0_ref, tri_ref, y_ref, s_ref) = refs
    else:
        (rw_ref, rwp_ref, wa_ref, wap_ref,
         mu_rw_ref, mu_wa_ref, wup_ref, w0_ref, aup_ref, a0_ref, kk_ref, ka_ref, rk_ref,
         lng_ref, lnb_ref, tri_ref, y_ref, vout_ref, s_ref) = refs
    T = RW_T
    TB = MIX_TB
    b16 = _b16

    def lerp(ref, halo_ref, mu_ref):
        x = ref[...]
        return x + (_prev_rows(x, halo_ref[...], first) - x) * mu_ref[...]

    xs = lerp(rw_ref, rwp_ref, mu_rw_ref)
    r, k, v, g = (xs[:, i * BW:(i + 1) * BW] for i in range(4))
    wa = lerp(wa_ref, wap_ref, mu_wa_ref)
    w_log = -_softplus(-(w0_ref[...] + _dot(b16(jnp.tanh(wa)), wup_ref[...]))) - 0.5
    lw = -jnp.exp(w_log)
    a = _sigmoid(a0_ref[...] + _dot(b16(wa), aup_ref[...]))
    if has_vmix:
        vd = lerp(misc_ref, miscp_ref, mu_misc_ref)
        mix = _sigmoid(v0_ref[...] + _dot(b16(vd), vup_ref[...]))
        v = v + (vfirst_ref[...] - v) * mix
    else:
        vout_ref[...] = v
    kk = k * kk_ref[...]
    kmod = k * (1.0 + (a - 1.0) * ka_ref[...])
    cums = _dot_split_rhs(tri_ref[...], lw)
    cl, rv = cums[:TB], cums[TB:]
    g_incl = jnp.exp(cl)
    g_inv = jnp.exp(-cl)
    g_end = jnp.exp(rv)
    rkr = r * kmod * rk_ref[...]

    def per_head_sum(x):
        h0 = lax.broadcasted_iota(jnp.int32, x.shape, 1) < HEAD
        f0 = h0.astype(F32)
        s0 = jnp.sum(jnp.where(h0, x, 0.0), axis=-1, keepdims=True)
        s1 = jnp.sum(jnp.where(h0, 0.0, x), axis=-1, keepdims=True)
        return s0 * f0 + s1 * (1.0 - f0)

    inv, bonus_w = [], []
    for q in range(N_PAIRS):
        sl = slice(q * LANES, (q + 1) * LANES)
        sq = kk[:, sl] * kk[:, sl]
        inv.append(1.0 / jnp.maximum(jnp.sqrt(per_head_sum(sq)), 1e-12))
        bonus_w.append(per_head_sum(rkr[:, sl]))
    kkn = kk * jnp.concatenate(inv, axis=1)
    bonus = jnp.concatenate(bonus_w, axis=1) * v
    kkna = kkn * a
    na_b = b16(-kkn * jnp.exp(cl - lw))
    nb_b = b16(kkna * g_inv)
    nbe_b = b16(kkna * g_end)
    pk_b = b16(kmod * g_inv)
    pke_b = b16(kmod * g_end)
    pr = r * g_incl
    pr_b = b16(pr)
    v_b = b16(v)
    gated = _silu(g)

    m0 = (lax.broadcasted_iota(jnp.int32, (1, LANES), 1) < HEAD).astype(BF16)
    m1 = 1.0 - m0
    row_w = lax.broadcasted_iota(jnp.int32, (T, LANES), 0)
    col_w = lax.broadcasted_iota(jnp.int32, (T, LANES), 1) % HEAD
    strict = col_w < row_w
    incl = col_w <= row_w
    eye_w = (col_w == row_w).astype(F32)
    rr = lax.broadcasted_iota(jnp.int32, (LANES, LANES), 0)
    cc = lax.broadcasted_iota(jnp.int32, (LANES, LANES), 1)
    same_head = (rr < HEAD) == (cc < HEAD)

    def stack(x):
        return jnp.concatenate([x * m0, x * m1], axis=0)

    def rows(c):
        return slice(c * T, (c + 1) * T)

    def lanes(q):
        return slice(q * LANES, (q + 1) * LANES)

    state = [s_ref[q] for q in range(N_PAIRS)]
    for c0 in range(0, RW_NB, RW_GC):
        units = [(c, q) for c in range(c0, c0 + RW_GC) for q in range(N_PAIRS)]
        a_n = [na_b[rows(c), lanes(q)] for c, q in units]
        r_n = [pr_b[rows(c), lanes(q)] for c, q in units]
        v_n = [v_b[rows(c), lanes(q)] for c, q in units]
        be_n = [nbe_b[rows(c), lanes(q)] for c, q in units]
        ke_n = [pke_b[rows(c), lanes(q)] for c, q in units]
        b_st = [stack(nb_b[rows(c), lanes(q)]) for c, q in units]
        k_st = [stack(pk_b[rows(c), lanes(q)]) for c, q in units]
        a_st = [stack(x) for x in a_n]
        v_st = [stack(x) for x in v_n]
        tick()
        aa = [_dot_nt(jnp.concatenate([a_, r_], axis=0), jnp.concatenate([b_, k_], axis=0))
              for a_, r_, b_, k_ in zip(a_n, r_n, b_st, k_st)]
        tick()
        a_ak = [b16(jnp.where(strict, x[:T, LANES:], 0.0)) for x in aa]
        a_rb = [b16(jnp.where(incl, x[T:, :LANES], 0.0)) for x in aa]
        a_rk = [b16(jnp.where(incl, x[T:, LANES:], 0.0)) for x in aa]
        p = [jnp.where(strict, x[:T, :LANES], 0.0) for x in aa]
        minv = [eye_w + x for x in p]
        tick()
        for _ in range(int(math.log2(T)) - 1):
            p_b = [b16(x) for x in p]
            p = [_dot(x, stack(x)) for x in p_b]
            tick()
            minv = [m + _dot(b16(x), stack(b16(m))) for x, m in zip(p, minv)]
            tick()
        akv = [_dot(x, y) for x, y in zip(a_ak, v_st)]
        tick()
        wu_b = [b16(_dot(b16(m), jnp.concatenate([a_, stack(b16(x))], axis=1)))
                for m, a_, x in zip(minv, a_st, akv)]
        tick()
        p_mat = [b16(jnp.where(same_head, _dot_tn(w[:, :LANES], be), 0.0)) for w, be in zip(wu_b, be_n)]
        tick()
        z_mat = [jnp.where(same_head,
                           _dot_tn(jnp.concatenate([w[:, LANES:], vn], axis=0), jnp.concatenate([be, ke], axis=0)),
                           0.0)
                 for w, vn, be, ke in zip(wu_b, v_n, be_n, ke_n)]
        tick()
        q_mat = [b16(pr[rows(c), lanes(q)] + _dot(arb, stack(w[:, :LANES])))
                 for (c, q), arb, w in zip(units, a_rb, wu_b)]
        tick()
        y0 = [_dot(jnp.concatenate([arb, ark], axis=1), jnp.concatenate([stack(w[:, LANES:]), vs], axis=0))
              for arb, ark, w, vs in zip(a_rb, a_rk, wu_b, v_st)]
        tick()
        ys = []
        for i, (c, q) in enumerate(units):
            s = state[q]
            s_b = b16(s)
            ys.append(_dot_nt(q_mat[i], s_b) + y0[i])
            state[q] = s * g_incl[(c + 1) * T - 1:(c + 1) * T, lanes(q)] + _dot(s_b, p_mat[i]) + z_mat[i]
        tick()
        for i, (c, q) in enumerate(units):
            y = ys[i]
            mean = per_head_sum(y) * (1.0 / HEAD)
            cen = y - mean
            var = per_head_sum(cen * cen) * (1.0 / HEAD)
            yn = cen * lax.rsqrt(var + GN_EPS)
            y_ref[rows(c), lanes(q)] = ((yn * lng_ref[:, lanes(q)] + lnb_ref[:, lanes(q)]
                                         + bonus[rows(c), lanes(q)]) * gated[rows(c), lanes(q)])
        tick()
    for q in range(N_PAIRS):
        s_ref[q] = state[q]


N_MAMBA_IN = 12


def _mixers_kernel(*refs, has_vmix):
    n_rw_in = 22 if has_vmix else 16
    rw_in, m_in = refs[:n_rw_in], refs[n_rw_in:n_rw_in + N_MAMBA_IN]
    rest = refs[n_rw_in + N_MAMBA_IN:]
    if has_vmix:
        y_rw_ref, y_m2_ref, s_ref, st_ref = rest
        rw_refs = (*rw_in, y_rw_ref, s_ref)
    else:
        y_rw_ref, vout_ref, y_m2_ref, s_ref, st_ref = rest
        rw_refs = (*rw_in, y_rw_ref, vout_ref, s_ref)
    first = pl.program_id(0) == 0

    @pl.when(first)
    def _():
        s_ref[...] = jnp.zeros_like(s_ref)
        st_ref[...] = jnp.zeros_like(st_ref)

    mamba = _mamba_stages(first, *m_in, y_m2_ref, st_ref)
    calls = [0]

    def tick():
        calls[0] += 1
        if calls[0] % M2_TICK == 0:
            next(mamba, None)

    _rwkv_body(first, tick, rw_refs, has_vmix)
    for _ in mamba:
        pass


def _mixers(proj, rp, mp, v_first):
    L = proj.shape[0]
    TB = MIX_TB
    has_vmix = v_first is not None
    hb = TB // SUBLANES

    def cur(width, col):
        return pl.BlockSpec((TB, width), lambda i: (i, col // width))

    def halo(width, col):
        return pl.BlockSpec((SUBLANES, width), lambda i: (jnp.maximum(i * hb - 1, 0), col // width))

    def full(a):
        return pl.BlockSpec(a.shape, lambda i: (0, 0))

    pos = np.arange(TB)
    same = (pos[:, None] // RW_T) == (pos[None, :] // RW_T)
    rw_tri = jnp.asarray(np.concatenate([same & (pos[None, :] <= pos[:, None]),
                                         same & (pos[None, :] > pos[:, None])], axis=0), BF16)
    args = [proj, proj, proj, proj]
    specs = [cur(4 * BW, COL_RW), halo(4 * BW, COL_RW), cur(LANES, COL_WA), halo(LANES, COL_WA)]
    if has_vmix:
        args += [proj, proj, v_first]
        specs += [cur(LANES, COL_MISC), halo(LANES, COL_MISC), pl.BlockSpec((TB, BW), lambda i: (i, 0))]
        names = ['mu_rw', 'mu_wa', 'mu_misc', 'w_up', 'w0', 'a_up', 'a0', 'k_k', 'k_a', 'r_k',
                 'ln_g', 'ln_b', 'v_up', 'v0']
    else:
        names = ['mu_rw', 'mu_wa', 'w_up', 'w0', 'a_up', 'a0', 'k_k', 'k_a', 'r_k', 'ln_g', 'ln_b']
    consts = [rp[n] for n in names] + [rw_tri]
    args += consts
    specs += [full(c) for c in consts]
    assert len(args) == (22 if has_vmix else 16)

    m_tri = jnp.tril(jnp.ones((M2_T, M2_T), BF16))
    m_consts = [mp['conv_w'], mp['conv_b'], mp['dt_bias'], mp['a_log'], mp['d_full'], mp['norm_g'],
                mp['expand'], m_tri]
    args += [proj, proj, proj, proj] + m_consts
    specs += [cur(M2_XBC, COL_XBC), halo(M2_XBC, COL_XBC), cur(BW, COL_Z), cur(LANES, COL_MISC)]
    specs += [full(c) for c in m_consts]
    assert len(m_consts) + 4 == N_MAMBA_IN

    tok = pl.BlockSpec((TB, BW), lambda i: (i, 0))
    tok_shape = jax.ShapeDtypeStruct((L, BW), F32)
    n_out = 2 if has_vmix else 3
    res = pl.pallas_call(
        functools.partial(_mixers_kernel, has_vmix=has_vmix),
        grid=(L // TB,),
        in_specs=specs,
        out_specs=[tok] * n_out,
        out_shape=[tok_shape] * n_out,
        scratch_shapes=[pltpu.VMEM((N_PAIRS, LANES, LANES), F32), pltpu.VMEM((N_PAIRS, M2_STATE, LANES), F32)],
        compiler_params=pltpu.CompilerParams(
            dimension_semantics=("arbitrary",), vmem_limit_bytes=VMEM_LIMIT),
    )(*args)
    if has_vmix:
        return res[0], res[1], v_first
    return res[0], res[2], res[1]


def _s5_state_tables(log_dt, a_re, a_im, b_re, b_im, c_re, c_im):
    T = S5_CT
    G, P = a_re.shape
    dt = jnp.exp(log_dt.astype(F32))[:, None]
    ar, ai = a_re.astype(F32), a_im.astype(F32)
    mag = jnp.exp(dt * ar)
    abar_re, abar_im = mag * jnp.cos(dt * ai), mag * jnp.sin(dt * ai)
    den = ar * ar + ai * ai
    f_re = ((abar_re - 1.0) * ar + abar_im * ai) / den
    f_im = (abar_im * ar - (abar_re - 1.0) * ai) / den
    br, bi = b_re.astype(F32), b_im.astype(F32)
    bb_re = f_re[..., None] * br - f_im[..., None] * bi
    bb_im = f_re[..., None] * bi + f_im[..., None] * br

    def powers(e):
        e = jnp.asarray(e, F32)[:, None, None]
        m = jnp.exp(e * (dt * ar)[None])
        ang = e * (dt * ai)[None]
        return (m * jnp.cos(ang)).reshape(-1, G * P), (m * jnp.sin(ang)).reshape(-1, G * P)

    pos = np.arange(T, dtype=np.float32)
    half = T / 2
    zr, zi = powers(half - pos)
    sr, si = powers(pos - half)
    cr, ci = powers(np.array([half + 1.0], np.float32))
    tiles = G // S5_TILE_G
    eye = jnp.eye(S5_TILE_G, dtype=F32)

    def b_side(bb):
        t = bb.reshape(tiles, S5_TILE_G, P, S5_GROUP).transpose(0, 1, 3, 2)
        return (t[:, :, :, None, :] * eye[None, :, None, :, None]).reshape(tiles, LANES, S5_TILE_G * P)

    def c_side(cc):
        t = cc.astype(F32).reshape(tiles, S5_TILE_G, S5_GROUP, P).transpose(0, 1, 3, 2)
        return (t[:, :, :, None, :] * eye[None, :, None, :, None]).reshape(tiles, S5_TILE_G * P, LANES)

    wb = jnp.concatenate([b_side(bb_re), b_side(bb_im)], axis=-1).astype(BF16)
    return dict(wb=wb, wc_re=c_side(c_re).astype(BF16), wc_im=c_side(-c_im.astype(F32)).astype(BF16),
                zr=zr, zi=zi, sr=sr, si=si, car=jnp.concatenate([cr, ci], axis=0))


def _s5out_kernel(h_ref, u_ref, gs5_ref, yrw_ref, ym2_ref, wb_ref, wcr_ref, wci_ref, zr_ref, zi_ref, sr_ref,
                  si_ref, car_ref, d_ref, tri_ref, wglu_ref, bglu_ref, wo_ref, fg_ref, o_ref,
                  st_ref, sre_ref, sim_ref, *, final):
    T = S5_CT
    NS = S5_GROUPS * S5_STATE
    tiles = S5_GROUPS // S5_TILE_G
    half = NS // tiles

    @pl.when(pl.program_id(0) == 0)
    def _():
        st_ref[...] = jnp.zeros_like(st_ref)

    u = u_ref[...]
    ub = u.astype(BF16)
    bu = [_dot(ub[:, j * LANES:(j + 1) * LANES], wb_ref[j]) for j in range(tiles)]
    bu_re = jnp.concatenate([b[:, :half] for b in bu], axis=1)
    bu_im = jnp.concatenate([b[:, half:] for b in bu], axis=1)
    zr, zi, sr, si = zr_ref[...], zi_ref[...], sr_ref[...], si_ref[...]
    car_re, car_im = car_ref[0:1, :], car_ref[1:2, :]
    p_re, p_im = st_ref[0:1, :], st_ref[1:2, :]
    tri = tri_ref[...]
    for c in range(OUT_TM // T):
        rows = slice(c * T, (c + 1) * T)
        b_re, b_im = bu_re[rows], bu_im[rows]
        z = jnp.concatenate([zr * b_re - zi * b_im, zr * b_im + zi * b_re], axis=1).astype(BF16)
        w = _dot(tri, z)
        w_re = w[:, :NS] + (car_re * p_re - car_im * p_im)
        w_im = w[:, NS:] + (car_re * p_im + car_im * p_re)
        s_re = sr * w_re - si * w_im
        s_im = sr * w_im + si * w_re
        p_re, p_im = s_re[T - 1:T, :], s_im[T - 1:T, :]
        sre_ref[rows, :] = s_re.astype(BF16)
        sim_ref[rows, :] = s_im.astype(BF16)
    st_ref[0:1, :] = p_re
    st_ref[1:2, :] = p_im
    ys = jnp.concatenate(
        [_dot(sre_ref[:, j * half:(j + 1) * half], wcr_ref[j]) + _dot(sim_ref[:, j * half:(j + 1) * half], wci_ref[j])
         for j in range(tiles)], axis=1) + d_ref[...] * u
    ys = _gelu_tanh(ys)
    ys = ys * _sigmoid(_dot(ys.astype(BF16), wglu_ref[...]) + bglu_ref[...])
    ys = ys * _silu(gs5_ref[...])
    acc = _dot(ys.astype(BF16), wo_ref[0:BW, :])
    acc = acc + _dot(yrw_ref[...].astype(BF16), wo_ref[BW:2 * BW, :])
    acc = acc + _dot(ym2_ref[...].astype(BF16), wo_ref[2 * BW:3 * BW, :])
    hn = h_ref[...] + acc
    if final:
        ms = jnp.mean(hn * hn, axis=-1, keepdims=True)
        hn = hn * lax.rsqrt(ms + NORM_EPS) * fg_ref[...]
    o_ref[...] = hn


def _s5out(h, proj, yrw, ym2, tb, d_row, w_glu, b_glu, w_out, final_g, final):
    L = h.shape[0]
    TM = OUT_TM
    NS = S5_GROUPS * S5_STATE
    tri = jnp.tril(jnp.ones((S5_CT, S5_CT), BF16))
    tok = pl.BlockSpec((TM, BW), lambda i: (i, 0))
    consts = [tb['wb'], tb['wc_re'], tb['wc_im'], tb['zr'], tb['zi'], tb['sr'], tb['si'], tb['car'],
              d_row, tri, w_glu, b_glu, w_out, final_g]
    const_specs = [pl.BlockSpec(c.shape, (lambda i: (0, 0, 0)) if c.ndim == 3 else (lambda i: (0, 0)))
                   for c in consts]
    return pl.pallas_call(
        functools.partial(_s5out_kernel, final=final),
        grid=(L // TM,),
        in_specs=[
            pl.BlockSpec((TM, D_MODEL), lambda i: (i, 0)),
            pl.BlockSpec((TM, BW), lambda i: (i, COL_U // BW)),
            pl.BlockSpec((TM, BW), lambda i: (i, COL_GS5 // BW)),
            tok,
            tok,
        ] + const_specs,
        out_specs=pl.BlockSpec((TM, D_MODEL), lambda i: (i, 0)),
        out_shape=jax.ShapeDtypeStruct((L, D_MODEL), F32),
        scratch_shapes=[pltpu.VMEM((SUBLANES, NS), F32), pltpu.VMEM((TM, NS), BF16), pltpu.VMEM((TM, NS), BF16)],
        compiler_params=pltpu.CompilerParams(
            dimension_semantics=("arbitrary",), vmem_limit_bytes=VMEM_LIMIT),
    )(h, proj, proj, yrw, ym2, *consts)


def _s5_stages(box, u_ref, wb_ref, wcr_ref, wci_ref, zr_ref, zi_ref, sr_ref, si_ref, car_ref, d_ref, tri_ref,
               st_ref, sre_ref, sim_ref):
    T = S5_CT
    NS = S5_GROUPS * S5_STATE
    tiles = S5_GROUPS // S5_TILE_G
    half = NS // tiles
    u = u_ref[...]
    ub = u.astype(BF16)
    bu = []
    for j in range(tiles):
        bu.append(_dot(ub[:, j * LANES:(j + 1) * LANES], wb_ref[j]))
        yield
    bu_re = jnp.concatenate([b[:, :half] for b in bu], axis=1)
    bu_im = jnp.concatenate([b[:, half:] for b in bu], axis=1)
    zr, zi, sr, si = zr_ref[...], zi_ref[...], sr_ref[...], si_ref[...]
    car_re, car_im = car_ref[0:1, :], car_ref[1:2, :]
    p_re, p_im = st_ref[0:1, :], st_ref[1:2, :]
    tri = tri_ref[...]
    for c in range(u.shape[0] // T):
        rows = slice(c * T, (c + 1) * T)
        b_re, b_im = bu_re[rows], bu_im[rows]
        z = jnp.concatenate([zr * b_re - zi * b_im, zr * b_im + zi * b_re], axis=1).astype(BF16)
        yield
        w = _dot(tri, z)
        yield
        w_re = w[:, :NS] + (car_re * p_re - car_im * p_im)
        w_im = w[:, NS:] + (car_re * p_im + car_im * p_re)
        s_re = sr * w_re - si * w_im
        s_im = sr * w_im + si * w_re
        p_re, p_im = s_re[T - 1:T, :], s_im[T - 1:T, :]
        sre_ref[rows, :] = s_re.astype(BF16)
        sim_ref[rows, :] = s_im.astype(BF16)
        yield
    st_ref[0:1, :] = p_re
    st_ref[1:2, :] = p_im
    ys = []
    for j in range(tiles):
        ys.append(_dot(sre_ref[:, j * half:(j + 1) * half], wcr_ref[j])
                  + _dot(sim_ref[:, j * half:(j + 1) * half], wci_ref[j]))
        yield
    box['ys'] = jnp.concatenate(ys, axis=1) + d_ref[...] * u


N_S5_IN = 17


def _layer_kernel(*refs, has_vmix, final):
    n_rw_in = 22 if has_vmix else 16
    rw_in = refs[:n_rw_in]
    m_in = refs[n_rw_in:n_rw_in + N_MAMBA_IN]
    s5_in = refs[n_rw_in + N_MAMBA_IN:n_rw_in + N_MAMBA_IN + N_S5_IN]
    rest = refs[n_rw_in + N_MAMBA_IN + N_S5_IN:]
    (h_ref, u_ref, gs5_ref, wb_ref, wcr_ref, wci_ref, zr_ref, zi_ref, sr_ref, si_ref, car_ref, d_ref, tri5_ref,
     wglu_ref, bglu_ref, wo_ref, fg_ref) = s5_in
    if has_vmix:
        o_ref, s_ref, st_ref, s5st_ref, sre_ref, sim_ref, yrw_ref, ym2_ref = rest
        rw_refs = (*rw_in, yrw_ref, s_ref)
    else:
        o_ref, vout_ref, s_ref, st_ref, s5st_ref, sre_ref, sim_ref, yrw_ref, ym2_ref = rest
        rw_refs = (*rw_in, yrw_ref, vout_ref, s_ref)
    first = pl.program_id(0) == 0

    @pl.when(first)
    def _():
        s_ref[...] = jnp.zeros_like(s_ref)
        st_ref[...] = jnp.zeros_like(st_ref)
        s5st_ref[...] = jnp.zeros_like(s5st_ref)

    box = {}
    mamba = _mamba_stages(first, *m_in, ym2_ref, st_ref)
    s5 = _s5_stages(box, u_ref, wb_ref, wcr_ref, wci_ref, zr_ref, zi_ref, sr_ref, si_ref, car_ref, d_ref,
                    tri5_ref, s5st_ref, sre_ref, sim_ref)
    calls = [0]

    def tick():
        calls[0] += 1
        next(mamba if calls[0] % 2 == 0 else s5, None)

    _rwkv_body(first, tick, rw_refs, has_vmix)
    for _ in mamba:
        pass
    for _ in s5:
        pass
    ys = _gelu_tanh(box['ys'])
    ys = ys * _sigmoid(_dot(ys.astype(BF16), wglu_ref[...]) + bglu_ref[...])
    ys = ys * _silu(gs5_ref[...])
    acc = _dot(ys.astype(BF16), wo_ref[0:BW, :])
    acc = acc + _dot(yrw_ref[...].astype(BF16), wo_ref[BW:2 * BW, :])
    acc = acc + _dot(ym2_ref[...].astype(BF16), wo_ref[2 * BW:3 * BW, :])
    hn = h_ref[...] + acc
    if final:
        ms = jnp.mean(hn * hn, axis=-1, keepdims=True)
        hn = hn * lax.rsqrt(ms + NORM_EPS) * fg_ref[...]
    o_ref[...] = hn


def _layer(h, proj, rp, mp, tb, d_row, w_glu, b_glu, w_out, final_g, v_first, final):
    L = proj.shape[0]
    TB = MIX_TB
    NS = S5_GROUPS * S5_STATE
    has_vmix = v_first is not None
    hb = TB // SUBLANES

    def cur(width, col):
        return pl.BlockSpec((TB, width), lambda i: (i, col // width))

    def halo(width, col):
        return pl.BlockSpec((SUBLANES, width), lambda i: (jnp.maximum(i * hb - 1, 0), col // width))

    def full(a):
        return pl.BlockSpec(a.shape, (lambda i: (0, 0, 0)) if a.ndim == 3 else (lambda i: (0, 0)))

    pos = np.arange(TB)
    same = (pos[:, None] // RW_T) == (pos[None, :] // RW_T)
    rw_tri = jnp.asarray(np.concatenate([same & (pos[None, :] <= pos[:, None]),
                                         same & (pos[None, :] > pos[:, None])], axis=0), BF16)
    args = [proj, proj, proj, proj]
    specs = [cur(4 * BW, COL_RW), halo(4 * BW, COL_RW), cur(LANES, COL_WA), halo(LANES, COL_WA)]
    if has_vmix:
        args += [proj, proj, v_first]
        specs += [cur(LANES, COL_MISC), halo(LANES, COL_MISC), pl.BlockSpec((TB, BW), lambda i: (i, 0))]
        names = ['mu_rw', 'mu_wa', 'mu_misc', 'w_up', 'w0', 'a_up', 'a0', 'k_k', 'k_a', 'r_k',
                 'ln_g', 'ln_b', 'v_up', 'v0']
    else:
        names = ['mu_rw', 'mu_wa', 'w_up', 'w0', 'a_up', 'a0', 'k_k', 'k_a', 'r_k', 'ln_g', 'ln_b']
    consts = [rp[n] for n in names] + [rw_tri]
    args += consts
    specs += [full(c) for c in consts]
    assert len(args) == (22 if has_vmix else 16)

    m_tri = jnp.tril(jnp.ones((M2_T, M2_T), BF16))
    m_consts = [mp['conv_w'], mp['conv_b'], mp['dt_bias'], mp['a_log'], mp['d_full'], mp['norm_g'],
                mp['expand'], m_tri]
    args += [proj, proj, proj, proj] + m_consts
    specs += [cur(M2_XBC, COL_XBC), halo(M2_XBC, COL_XBC), cur(BW, COL_Z), cur(LANES, COL_MISC)]
    specs += [full(c) for c in m_consts]

    s5_tri = jnp.tril(jnp.ones((S5_CT, S5_CT), BF16))
    s5_consts = [tb['wb'], tb['wc_re'], tb['wc_im'], tb['zr'], tb['zi'], tb['sr'], tb['si'], tb['car'],
                 d_row, s5_tri, w_glu, b_glu, w_out, final_g]
    args += [h, proj, proj] + s5_consts
    specs += [pl.BlockSpec((TB, D_MODEL), lambda i: (i, 0)), cur(BW, COL_U), cur(BW, COL_GS5)]
    specs += [full(c) for c in s5_consts]
    assert len(s5_consts) + 3 == N_S5_IN

    tok = pl.BlockSpec((TB, BW), lambda i: (i, 0))
    out_specs = [pl.BlockSpec((TB, D_MODEL), lambda i: (i, 0))]
    out_shape = [jax.ShapeDtypeStruct((L, D_MODEL), F32)]
    if not has_vmix:
        out_specs.append(tok)
        out_shape.append(jax.ShapeDtypeStruct((L, BW), F32))
    res = pl.pallas_call(
        functools.partial(_layer_kernel, has_vmix=has_vmix, final=final),
        grid=(L // TB,),
        in_specs=specs,
        out_specs=out_specs,
        out_shape=out_shape,
        scratch_shapes=[pltpu.VMEM((N_PAIRS, LANES, LANES), F32), pltpu.VMEM((N_PAIRS, M2_STATE, LANES), F32),
                        pltpu.VMEM((SUBLANES, NS), F32), pltpu.VMEM((TB, NS), BF16), pltpu.VMEM((TB, NS), BF16),
                        pltpu.VMEM((TB, BW), F32), pltpu.VMEM((TB, BW), F32)],
        compiler_params=pltpu.CompilerParams(
            dimension_semantics=("arbitrary",), vmem_limit_bytes=VMEM_LIMIT),
    )(*args)
    return res[0], (v_first if has_vmix else res[1])


def _reorder_w_in(w, w_vmix):
    o = np.cumsum([0, BW, BW, BW, BW, BW, LORA_W, LORA_A, BW, M2_XBC, N_HEADS, BW])
    u, gs5, r, k, v, wd, ad, g, xbc, dt, z = (w[:, o[i]:o[i + 1]] for i in range(11))
    vm = w_vmix if w_vmix is not None else jnp.zeros((w.shape[0], LORA_V), w.dtype)
    pad = jnp.zeros((w.shape[0], LANES - N_HEADS - LORA_V), w.dtype)
    return jnp.concatenate([r, k, v, g, u, gs5, xbc, z, wd, ad, dt, vm, pad], axis=1).astype(BF16)


def _pad_rows(w, start, total=LANES):
    return jnp.pad(w.astype(F32), ((start, total - start - w.shape[0]), (0, 0)))


def _row(a):
    return a.astype(F32).reshape(1, -1)


def kernel(x, norm_g, w_in, w_in_vmix, s5_log_dt, s5_a_re, s5_a_im, s5_b_re, s5_b_im, s5_c_re, s5_c_im, s5_d, s5_w_glu, s5_b_glu, rwkv_mu, rwkv_w_up, rwkv_w0, rwkv_a_up, rwkv_a0, rwkv_k_k, rwkv_k_a, rwkv_r_k, rwkv_ln_g, rwkv_ln_b, rwkv_vmix_mu, rwkv_v_up, rwkv_v0, m2_conv_w, m2_conv_b, m2_dt_bias, m2_a_log, m2_d, m2_norm_g, w_out, final_norm_g):
    bsz, L, d = x.shape
    assert bsz == 1 and d == D_MODEL
    assert L % max(PROJ_TM, OUT_TM, MIX_TB) == 0 and MIX_TB % M2_T == 0 and OUT_TM % S5_CT == 0
    depth = w_in.shape[0]
    h = x.reshape(L, d).astype(F32)
    expand = jnp.pad(jnp.repeat(jnp.eye(N_HEADS, dtype=F32), HEAD, axis=1), ((0, LANES - N_HEADS), (0, 0)))
    v_first = None
    for i in range(depth):
        w_cat = _reorder_w_in(w_in[i], w_in_vmix[i - 1] if i > 0 else None)
        proj = _proj(h, _row(norm_g[i]), w_cat)

        mu = rwkv_mu[i].astype(F32)
        mo = np.cumsum([0, BW, BW, BW, LORA_W, LORA_A, BW])
        mr, mk, mv, mwd, mad, mg = (mu[mo[j]:mo[j + 1]] for j in range(6))
        rp = {
            'mu_rw': jnp.concatenate([mr, mk, mv, mg]).reshape(1, -1),
            'mu_wa': jnp.concatenate([mwd, mad]).reshape(1, -1),
            'w_up': _pad_rows(rwkv_w_up[i], 0).astype(BF16), 'w0': _row(rwkv_w0[i]),
            'a_up': _pad_rows(rwkv_a_up[i], LORA_W).astype(BF16), 'a0': _row(rwkv_a0[i]),
            'k_k': _row(rwkv_k_k[i]), 'k_a': _row(rwkv_k_a[i]), 'r_k': _row(rwkv_r_k[i]),
            'ln_g': _row(rwkv_ln_g[i]), 'ln_b': _row(rwkv_ln_b[i]),
        }
        if i > 0:
            rp['mu_misc'] = jnp.pad(rwkv_vmix_mu[i - 1].astype(F32),
                                    (N_HEADS, LANES - N_HEADS - LORA_V)).reshape(1, -1)
            rp['v_up'] = _pad_rows(rwkv_v_up[i - 1], N_HEADS).astype(BF16)
            rp['v0'] = _row(rwkv_v0[i - 1])
        mp = {
            'conv_w': m2_conv_w[i].astype(F32), 'conv_b': _row(m2_conv_b[i]),
            'dt_bias': jnp.pad(m2_dt_bias[i].astype(F32), (0, LANES - N_HEADS)).reshape(1, -1),
            'a_log': jnp.pad(m2_a_log[i].astype(F32), (0, LANES - N_HEADS)).reshape(1, -1),
            'd_full': jnp.repeat(m2_d[i].astype(F32), HEAD).reshape(1, -1),
            'norm_g': _row(m2_norm_g[i]), 'expand': expand.astype(BF16),
        }
        s5_tb = _s5_state_tables(s5_log_dt[i], s5_a_re[i], s5_a_im[i], s5_b_re[i], s5_b_im[i],
                                 s5_c_re[i], s5_c_im[i])
        h, v_first = _layer(h, proj, rp, mp, s5_tb, _row(s5_d[i]), s5_w_glu[i].astype(BF16),
                            _row(s5_b_glu[i]), w_out[i].astype(BF16), _row(final_norm_g), v_first,
                            final=(i == depth - 1))
    return h.reshape(bsz, L, d).astype(x.dtype)
```

```python
import functools
import math

import jax
import jax.numpy as jnp
import numpy as np
from jax import lax
from jax.experimental import pallas as pl
from jax.experimental.pallas import tpu as pltpu

F32 = jnp.float32
BF16 = jnp.bfloat16
HI = lax.Precision.HIGHEST

D_MODEL = 1024
BW = 512
S5_GROUP = 16
S5_GROUPS = 32
S5_STATE = 64
HEAD = 64
N_HEADS = 8
N_PAIRS = N_HEADS // 2
LORA_W = 64
LORA_A = 64
LORA_V = 32
GN_EPS = 64e-5
M2_GROUPS = 2
M2_STATE = 128
M2_CONV = 4
M2_XBC = 1024
NORM_EPS = 1e-5

LANES = 128
SUBLANES = 8

COL_RW = 0
COL_U = 2048
COL_GS5 = 2560
COL_XBC = 3072
COL_Z = 4096
COL_WA = 4608
COL_MISC = 4736
PROJ_PAD = 4864

PROJ_TM = 256
PROJ_TN = 512
S5_CT = 64
S5_TILE_G = LANES // S5_GROUP
RW_T = 64
RW_NB = 4
RW_GC = 2
MIX_TB = RW_T * RW_NB
M2_T = 128
M2_TICK = 2
OUT_TM = 256
VMEM_LIMIT = 48 * 1024 * 1024
FUSED_VMEM_LIMIT = 56 * 1024 * 1024


def _dot(a, b, prec=None):
    return jnp.dot(a, b, precision=prec, preferred_element_type=F32)


def _dot_nt(a, b, prec=None):
    return lax.dot_general(a, b, (((1,), (1,)), ((), ())), precision=prec, preferred_element_type=F32)


def _dot_tn(a, b, prec=None):
    return lax.dot_general(a, b, (((0,), (0,)), ((), ())), precision=prec, preferred_element_type=F32)


def _split3(x):
    hi = x.astype(BF16)
    r1 = x - hi.astype(F32)
    mid = r1.astype(BF16)
    lo = (r1 - mid.astype(F32)).astype(BF16)
    return hi, mid, lo


def _dot_split_rhs(m, x):
    hi, mid, lo = _split3(x)
    return _dot(m, hi) + _dot(m, mid) + _dot(m, lo)


def _dot_split_lhs(x, m):
    hi, mid, lo = _split3(x)
    return _dot(hi, m) + _dot(mid, m) + _dot(lo, m)


def _b16(x):
    return x.astype(BF16)


def _sigmoid(x):
    return 1.0 / (1.0 + jnp.exp(-x))


def _silu(x):
    return x * _sigmoid(x)


def _softplus(x):
    return jnp.maximum(x, 0.0) + jnp.log1p(jnp.exp(-jnp.abs(x)))


def _gelu_tanh(x):
    c = math.sqrt(2.0 / math.pi)
    return 0.5 * x * (1.0 + jnp.tanh(c * (x + 0.044715 * (x * x * x))))


def _prev_rows(x, halo, is_first):
    rolled = pltpu.roll(x, 1, axis=0)
    last = jnp.where(is_first, 0.0, halo[SUBLANES - 1:SUBLANES, :])
    row = lax.broadcasted_iota(jnp.int32, x.shape, 0)
    return jnp.where(row == 0, last, rolled)


def _proj_kernel(h_ref, g_ref, w_ref, o_ref):
    x = h_ref[...]
    ms = jnp.mean(x * x, axis=-1, keepdims=True)
    hn = (x * lax.rsqrt(ms + NORM_EPS) * g_ref[...]).astype(BF16)
    for j in range(0, PROJ_PAD, PROJ_TN):
        w = min(PROJ_TN, PROJ_PAD - j)
        o_ref[:, j:j + w] = _dot(hn, w_ref[:, j:j + w])


def _proj(h, g, w):
    L = h.shape[0]
    return pl.pallas_call(
        _proj_kernel,
        grid=(L // PROJ_TM,),
        in_specs=[
            pl.BlockSpec((PROJ_TM, D_MODEL), lambda i: (i, 0)),
            pl.BlockSpec((1, D_MODEL), lambda i: (0, 0)),
            pl.BlockSpec((D_MODEL, PROJ_PAD), lambda i: (0, 0)),
        ],
        out_specs=pl.BlockSpec((PROJ_TM, PROJ_PAD), lambda i: (i, 0)),
        out_shape=jax.ShapeDtypeStruct((L, PROJ_PAD), F32),
        compiler_params=pltpu.CompilerParams(
            dimension_semantics=("parallel",), vmem_limit_bytes=VMEM_LIMIT),
    )(h, g, w)


def _mamba_stages(first, xbc_ref, xbcp_ref, z_ref, misc_ref, cw_ref, cb_ref, dtb_ref, alog_ref, dfull_ref,
                  ng_ref, expand_ref, tri_ref, y_ref, st_ref):
    T = M2_T
    x = xbc_ref[...]
    halo = jnp.where(first, 0.0, xbcp_ref[...])
    row8 = lax.broadcasted_iota(jnp.int32, halo.shape, 0)
    acc = x * cw_ref[M2_CONV - 1:M2_CONV, :] + cb_ref[...]
    yield
    for j in range(1, M2_CONV):
        xr = pltpu.roll(x, j, axis=0)
        head = jnp.where(row8 < j, pltpu.roll(halo, j, axis=0), xr[:SUBLANES])
        xj = jnp.concatenate([head, xr[SUBLANES:]], axis=0)
        acc = acc + xj * cw_ref[M2_CONV - 1 - j:M2_CONV - j, :]
        yield
    xbc = _silu(acc)
    xm = xbc[:, :BW]
    bm_b = _b16(xbc[:, BW:BW + M2_GROUPS * M2_STATE])
    cm_b = _b16(xbc[:, BW + M2_GROUPS * M2_STATE:])
    yield
    dt = _softplus(misc_ref[...] + dtb_ref[...])
    da = dt * (-jnp.exp(alog_ref[...]))
    expand = expand_ref[...]
    xdt = xm * _dot_split_lhs(dt, expand)
    xdt_b = _b16(xdt)
    gate = _silu(z_ref[...])
    yield

    rr = lax.broadcasted_iota(jnp.int32, (T, T), 0)
    cc = lax.broadcasted_iota(jnp.int32, (T, T), 1)
    causal = cc <= rr
    head0 = lax.broadcasted_iota(jnp.int32, (T, LANES), 1) < HEAD
    states = [st_ref[q] for q in range(N_PAIRS)]
    for ch in range(x.shape[0] // T):
        rows = slice(ch * T, (ch + 1) * T)
        cs = _dot_split_rhs(tri_ref[...], da[rows])
        cs_t = cs.T
        cs_full = _dot_split_lhs(cs, expand)
        cs_last = cs_full[T - 1:T, :]
        ecs = jnp.exp(cs_full)
        xdte_b = _b16(xdt[rows] * jnp.exp(cs_last - cs_full))
        chunk_decay = jnp.exp(cs_last)
        yield
        scores = [_dot_nt(cm_b[rows, gi * M2_STATE:(gi + 1) * M2_STATE], bm_b[rows, gi * M2_STATE:(gi + 1) * M2_STATE])
                  for gi in range(M2_GROUPS)]
        yield
        ys = []
        for q in range(N_PAIRS):
            gi = (2 * q) // (N_HEADS // M2_GROUPS)
            gs = slice(gi * M2_STATE, (gi + 1) * M2_STATE)
            sl = slice(q * LANES, (q + 1) * LANES)
            yd = []
            for h in (2 * q, 2 * q + 1):
                seg = cs[:, h:h + 1] - cs_t[h:h + 1, :]
                dec = jnp.where(causal, jnp.exp(jnp.minimum(seg, 0.0)), 0.0)
                yd.append(_dot(_b16(scores[gi] * dec), xdt_b[rows, sl]))
            st = states[q]
            y_off = _dot(cm_b[rows, gs], _b16(st)) * ecs[:, sl]
            states[q] = st * chunk_decay[:, sl] + _dot_tn(bm_b[rows, gs], xdte_b[:, sl])
            ys.append(jnp.where(head0, yd[0], yd[1]) + y_off)
            yield
        y = jnp.concatenate(ys, axis=1) + xm[rows] * dfull_ref[...]
        y = y * gate[rows]
        ms = jnp.mean(y * y, axis=-1, keepdims=True)
        y_ref[rows, :] = y * lax.rsqrt(ms + NORM_EPS) * ng_ref[...]
        yield
    for q in range(N_PAIRS):
        st_ref[q] = states[q]


def _rwkv_body(first, tick, refs, has_vmix):
    if has_vmix:
        (rw_ref, rwp_ref, wa_ref, wap_ref, misc_ref, miscp_ref, vfirst_ref,
         mu_rw_ref, mu_wa_ref, mu_misc_ref, wup_ref, w0_ref, aup_ref, a0_ref, kk_ref, ka_ref, rk_ref,
         lng_ref, lnb_ref, vup_ref, v0_ref, tri_ref, y_ref, s_ref) = refs
    else:
        (rw_ref, rwp_ref, wa_ref, wap_ref,
         mu_rw_ref, mu_wa_ref, wup_ref, w0_ref, aup_ref, a0_ref, kk_ref, ka_ref, rk_ref,
         lng_ref, lnb_ref, tri_ref, y_ref, vout_ref, s_ref) = refs
    T = RW_T
    TB = MIX_TB
    b16 = _b16

    def lerp(ref, halo_ref, mu_ref):
        x = ref[...]
        return x + (_prev_rows(x, halo_ref[...], first) - x) * mu_ref[...]

    xs = lerp(rw_ref, rwp_ref, mu_rw_ref)
    r, k, v, g = (xs[:, i * BW:(i + 1) * BW] for i in range(4))
    wa = lerp(wa_ref, wap_ref, mu_wa_ref)
    w_log = -_softplus(-(w0_ref[...] + _dot(b16(jnp.tanh(wa)), wup_ref[...]))) - 0.5
    lw = -jnp.exp(w_log)
    a = _sigmoid(a0_ref[...] + _dot(b16(wa), aup_ref[...]))
    if has_vmix:
        vd = lerp(misc_ref, miscp_ref, mu_misc_ref)
        mix = _sigmoid(v0_ref[...] + _dot(b16(vd), vup_ref[...]))
        v = v + (vfirst_ref[...] - v) * mix
    else:
        vout_ref[...] = v
    kk = k * kk_ref[...]
    kmod = k * (1.0 + (a - 1.0) * ka_ref[...])
    cums = _dot_split_rhs(tri_ref[...], lw)
    cl, rv = cums[:TB], cums[TB:]
    g_incl = jnp.exp(cl)
    g_inv = jnp.exp(-cl)
    g_end = jnp.exp(rv)
    rkr = r * kmod * rk_ref[...]

    def per_head_sum(x):
        h0 = lax.broadcasted_iota(jnp.int32, x.shape, 1) < HEAD
        f0 = h0.astype(F32)
        s0 = jnp.sum(jnp.where(h0, x, 0.0), axis=-1, keepdims=True)
        s1 = jnp.sum(jnp.where(h0, 0.0, x), axis=-1, keepdims=True)
        return s0 * f0 + s1 * (1.0 - f0)

    inv, bonus_w = [], []
    for q in range(N_PAIRS):
        sl = slice(q * LANES, (q + 1) * LANES)
        sq = kk[:, sl] * kk[:, sl]
        inv.append(1.0 / jnp.maximum(jnp.sqrt(per_head_sum(sq)), 1e-12))
        bonus_w.append(per_head_sum(rkr[:, sl]))
    kkn = kk * jnp.concatenate(inv, axis=1)
    bonus = jnp.concatenate(bonus_w, axis=1) * v
    kkna = kkn * a
    na_b = b16(-kkn * jnp.exp(cl - lw))
    nb_b = b16(kkna * g_inv)
    nbe_b = b16(kkna * g_end)
    pk_b = b16(kmod * g_inv)
    pke_b = b16(kmod * g_end)
    pr = r * g_incl
    pr_b = b16(pr)
    v_b = b16(v)
    gated = _silu(g)

    m0 = (lax.broadcasted_iota(jnp.int32, (1, LANES), 1) < HEAD).astype(BF16)
    m1 = 1.0 - m0
    row_w = lax.broadcasted_iota(jnp.int32, (T, LANES), 0)
    col_w = lax.broadcasted_iota(jnp.int32, (T, LANES), 1) % HEAD
    strict = col_w < row_w
    incl = col_w <= row_w
    eye_w = (col_w == row_w).astype(F32)
    rr = lax.broadcasted_iota(jnp.int32, (LANES, LANES), 0)
    cc = lax.broadcasted_iota(jnp.int32, (LANES, LANES), 1)
    same_head = (rr < HEAD) == (cc < HEAD)

    def stack(x):
        return jnp.concatenate([x * m0, x * m1], axis=0)

    def rows(c):
        return slice(c * T, (c + 1) * T)

    def lanes(q):
        return slice(q * LANES, (q + 1) * LANES)

    state = [s_ref[q] for q in range(N_PAIRS)]
    for c0 in range(0, RW_NB, RW_GC):
        units = [(c, q) for c in range(c0, c0 + RW_GC) for q in range(N_PAIRS)]
        a_n = [na_b[rows(c), lanes(q)] for c, q in units]
        r_n = [pr_b[rows(c), lanes(q)] for c, q in units]
        v_n = [v_b[rows(c), lanes(q)] for c, q in units]
        be_n = [nbe_b[rows(c), lanes(q)] for c, q in units]
        ke_n = [pke_b[rows(c), lanes(q)] for c, q in units]
        b_st = [stack(nb_b[rows(c), lanes(q)]) for c, q in units]
        k_st = [stack(pk_b[rows(c), lanes(q)]) for c, q in units]
        a_st = [stack(x) for x in a_n]
        v_st = [stack(x) for x in v_n]
        tick()
        aa = [_dot_nt(jnp.concatenate([a_, r_], axis=0), jnp.concatenate([b_, k_], axis=0))
              for a_, r_, b_, k_ in zip(a_n, r_n, b_st, k_st)]
        tick()
        a_ak = [b16(jnp.where(strict, x[:T, LANES:], 0.0)) for x in aa]
        a_rb = [b16(jnp.where(incl, x[T:, :LANES], 0.0)) for x in aa]
        a_rk = [b16(jnp.where(incl, x[T:, LANES:], 0.0)) for x in aa]
        p = [jnp.where(strict, x[:T, :LANES], 0.0) for x in aa]
        minv = [eye_w + x for x in p]
        tick()
        for _ in range(int(math.log2(T)) - 1):
            p_b = [b16(x) for x in p]
            p = [_dot(x, stack(x)) for x in p_b]
            tick()
            minv = [m + _dot(b16(x), stack(b16(m))) for x, m in zip(p, minv)]
            tick()
        akv = [_dot(x, y) for x, y in zip(a_ak, v_st)]
        tick()
        wu_b = [b16(_dot(b16(m), jnp.concatenate([a_, stack(b16(x))], axis=1)))
                for m, a_, x in zip(minv, a_st, akv)]
        tick()
        p_mat = [b16(jnp.where(same_head, _dot_tn(w[:, :LANES], be), 0.0)) for w, be in zip(wu_b, be_n)]
        tick()
        z_mat = [jnp.where(same_head,
                           _dot_tn(jnp.concatenate([w[:, LANES:], vn], axis=0), jnp.concatenate([be, ke], axis=0)),
                           0.0)
                 for w, vn, be, ke in zip(wu_b, v_n, be_n, ke_n)]
        tick()
        q_mat = [b16(pr[rows(c), lanes(q)] + _dot(arb, stack(w[:, :LANES])))
                 for (c, q), arb, w in zip(units, a_rb, wu_b)]
        tick()
        y0 = [_dot(jnp.concatenate([arb, ark], axis=1), jnp.concatenate([stack(w[:, LANES:]), vs], axis=0))
              for arb, ark, w, vs in zip(a_rb, a_rk, wu_b, v_st)]
        tick()
        ys = []
        for i, (c, q) in enumerate(units):
            s = state[q]
            s_b = b16(s)
            ys.append(_dot_nt(q_mat[i], s_b) + y0[i])
            state[q] = s * g_incl[(c + 1) * T - 1:(c + 1) * T, lanes(q)] + _dot(s_b, p_mat[i]) + z_mat[i]
        tick()
        for i, (c, q) in enumerate(units):
            y = ys[i]
            mean = per_head_sum(y) * (1.0 / HEAD)
            cen = y - mean
            var = per_head_sum(cen * cen) * (1.0 / HEAD)
            yn = cen * lax.rsqrt(var + GN_EPS)
            y_ref[rows(c), lanes(q)] = ((yn * lng_ref[:, lanes(q)] + lnb_ref[:, lanes(q)]
                                         + bonus[rows(c), lanes(q)]) * gated[rows(c), lanes(q)])
        tick()
    for q in range(N_PAIRS):
        s_ref[q] = state[q]


N_MAMBA_IN = 12


def _mixers_kernel(*refs, has_vmix):
    n_rw_in = 22 if has_vmix else 16
    rw_in, m_in = refs[:n_rw_in], refs[n_rw_in:n_rw_in + N_MAMBA_IN]
    rest = refs[n_rw_in + N_MAMBA_IN:]
    if has_vmix:
        y_rw_ref, y_m2_ref, s_ref, st_ref = rest
        rw_refs = (*rw_in, y_rw_ref, s_ref)
    else:
        y_rw_ref, vout_ref, y_m2_ref, s_ref, st_ref = rest
        rw_refs = (*rw_in, y_rw_ref, vout_ref, s_ref)
    first = pl.program_id(0) == 0

    @pl.when(first)
    def _():
        s_ref[...] = jnp.zeros_like(s_ref)
        st_ref[...] = jnp.zeros_like(st_ref)

    mamba = _mamba_stages(first, *m_in, y_m2_ref, st_ref)
    calls = [0]

    def tick():
        calls[0] += 1
        if calls[0] % M2_TICK == 0:
            next(mamba, None)

    _rwkv_body(first, tick, rw_refs, has_vmix)
    for _ in mamba:
        pass


def _mixers(proj, rp, mp, v_first):
    L = proj.shape[0]
    TB = MIX_TB
    has_vmix = v_first is not None
    hb = TB // SUBLANES

    def cur(width, col):
        return pl.BlockSpec((TB, width), lambda i: (i, col // width))

    def halo(width, col):
        return pl.BlockSpec((SUBLANES, width), lambda i: (jnp.maximum(i * hb - 1, 0), col // width))

    def full(a):
        return pl.BlockSpec(a.shape, lambda i: (0, 0))

    pos = np.arange(TB)
    same = (pos[:, None] // RW_T) == (pos[None, :] // RW_T)
    rw_tri = jnp.asarray(np.concatenate([same & (pos[None, :] <= pos[:, None]),
                                         same & (pos[None, :] > pos[:, None])], axis=0), BF16)
    args = [proj, proj, proj, proj]
    specs = [cur(4 * BW, COL_RW), halo(4 * BW, COL_RW), cur(LANES, COL_WA), halo(LANES, COL_WA)]
    if has_vmix:
        args += [proj, proj, v_first]
        specs += [cur(LANES, COL_MISC), halo(LANES, COL_MISC), pl.BlockSpec((TB, BW), lambda i: (i, 0))]
        names = ['mu_rw', 'mu_wa', 'mu_misc', 'w_up', 'w0', 'a_up', 'a0', 'k_k', 'k_a', 'r_k',
                 'ln_g', 'ln_b', 'v_up', 'v0']
    else:
        names = ['mu_rw', 'mu_wa', 'w_up', 'w0', 'a_up', 'a0', 'k_k', 'k_a', 'r_k', 'ln_g', 'ln_b']
    consts = [rp[n] for n in names] + [rw_tri]
    args += consts
    specs += [full(c) for c in consts]
    assert len(args) == (22 if has_vmix else 16)

    m_tri = jnp.tril(jnp.ones((M2_T, M2_T), BF16))
    m_consts = [mp['conv_w'], mp['conv_b'], mp['dt_bias'], mp['a_log'], mp['d_full'], mp['norm_g'],
                mp['expand'], m_tri]
    args += [proj, proj, proj, proj] + m_consts
    specs += [cur(M2_XBC, COL_XBC), halo(M2_XBC, COL_XBC), cur(BW, COL_Z), cur(LANES, COL_MISC)]
    specs += [full(c) for c in m_consts]
    assert len(m_consts) + 4 == N_MAMBA_IN

    tok = pl.BlockSpec((TB, BW), lambda i: (i, 0))
    tok_shape = jax.ShapeDtypeStruct((L, BW), F32)
    n_out = 2 if has_vmix else 3
    res = pl.pallas_call(
        functools.partial(_mixers_kernel, has_vmix=has_vmix),
        grid=(L // TB,),
        in_specs=specs,
        out_specs=[tok] * n_out,
        out_shape=[tok_shape] * n_out,
        scratch_shapes=[pltpu.VMEM((N_PAIRS, LANES, LANES), F32), pltpu.VMEM((N_PAIRS, M2_STATE, LANES), F32)],
        compiler_params=pltpu.CompilerParams(
            dimension_semantics=("arbitrary",), vmem_limit_bytes=VMEM_LIMIT),
    )(*args)
    if has_vmix:
        return res[0], res[1], v_first
    return res[0], res[2], res[1]


def _s5_state_tables(log_dt, a_re, a_im, b_re, b_im, c_re, c_im):
    T = S5_CT
    G, P = a_re.shape
    dt = jnp.exp(log_dt.astype(F32))[:, None]
    ar, ai = a_re.astype(F32), a_im.astype(F32)
    mag = jnp.exp(dt * ar)
    abar_re, abar_im = mag * jnp.cos(dt * ai), mag * jnp.sin(dt * ai)
    den = ar * ar + ai * ai
    f_re = ((abar_re - 1.0) * ar + abar_im * ai) / den
    f_im = (abar_im * ar - (abar_re - 1.0) * ai) / den
    br, bi = b_re.astype(F32), b_im.astype(F32)
    bb_re = f_re[..., None] * br - f_im[..., None] * bi
    bb_im = f_re[..., None] * bi + f_im[..., None] * br

    def powers(e):
        e = jnp.asarray(e, F32)[:, None, None]
        m = jnp.exp(e * (dt * ar)[None])
        ang = e * (dt * ai)[None]
        return (m * jnp.cos(ang)).reshape(-1, G * P), (m * jnp.sin(ang)).reshape(-1, G * P)

    pos = np.arange(T, dtype=np.float32)
    half = T / 2
    zr, zi = powers(half - pos)
    sr, si = powers(pos - half)
    cr, ci = powers(np.array([half + 1.0], np.float32))
    tiles = G // S5_TILE_G
    eye = jnp.eye(S5_TILE_G, dtype=F32)

    def b_side(bb):
        t = bb.reshape(tiles, S5_TILE_G, P, S5_GROUP).transpose(0, 1, 3, 2)
        return (t[:, :, :, None, :] * eye[None, :, None, :, None]).reshape(tiles, LANES, S5_TILE_G * P)

    def c_side(cc):
        t = cc.astype(F32).reshape(tiles, S5_TILE_G, S5_GROUP, P).transpose(0, 1, 3, 2)
        return (t[:, :, :, None, :] * eye[None, :, None, :, None]).reshape(tiles, S5_TILE_G * P, LANES)

    wb = jnp.concatenate([b_side(bb_re), b_side(bb_im)], axis=-1).astype(BF16)
    return dict(wb=wb, wc_re=c_side(c_re).astype(BF16), wc_im=c_side(-c_im.astype(F32)).astype(BF16),
                zr=zr, zi=zi, sr=sr, si=si, car=jnp.concatenate([cr, ci], axis=0))


def _s5out_kernel(h_ref, u_ref, gs5_ref, yrw_ref, ym2_ref, wb_ref, wcr_ref, wci_ref, zr_ref, zi_ref, sr_ref,
                  si_ref, car_ref, d_ref, tri_ref, wglu_ref, bglu_ref, wo_ref, fg_ref, o_ref,
                  st_ref, sre_ref, sim_ref, *, final):
    T = S5_CT
    NS = S5_GROUPS * S5_STATE
    tiles = S5_GROUPS // S5_TILE_G
    half = NS // tiles

    @pl.when(pl.program_id(0) == 0)
    def _():
        st_ref[...] = jnp.zeros_like(st_ref)

    u = u_ref[...]
    ub = u.astype(BF16)
    bu = [_dot(ub[:, j * LANES:(j + 1) * LANES], wb_ref[j]) for j in range(tiles)]
    bu_re = jnp.concatenate([b[:, :half] for b in bu], axis=1)
    bu_im = jnp.concatenate([b[:, half:] for b in bu], axis=1)
    zr, zi, sr, si = zr_ref[...], zi_ref[...], sr_ref[...], si_ref[...]
    car_re, car_im = car_ref[0:1, :], car_ref[1:2, :]
    p_re, p_im = st_ref[0:1, :], st_ref[1:2, :]
    tri = tri_ref[...]
    for c in range(OUT_TM // T):
        rows = slice(c * T, (c + 1) * T)
        b_re, b_im = bu_re[rows], bu_im[rows]
        z = jnp.concatenate([zr * b_re - zi * b_im, zr * b_im + zi * b_re], axis=1).astype(BF16)
        w = _dot(tri, z)
        w_re = w[:, :NS] + (car_re * p_re - car_im * p_im)
        w_im = w[:, NS:] + (car_re * p_im + car_im * p_re)
        s_re = sr * w_re - si * w_im
        s_im = sr * w_im + si * w_re
        p_re, p_im = s_re[T - 1:T, :], s_im[T - 1:T, :]
        sre_ref[rows, :] = s_re.astype(BF16)
        sim_ref[rows, :] = s_im.astype(BF16)
    st_ref[0:1, :] = p_re
    st_ref[1:2, :] = p_im
    ys = jnp.concatenate(
        [_dot(sre_ref[:, j * half:(j + 1) * half], wcr_ref[j]) + _dot(sim_ref[:, j * half:(j + 1) * half], wci_ref[j])
         for j in range(tiles)], axis=1) + d_ref[...] * u
    ys = _gelu_tanh(ys)
    ys = ys * _sigmoid(_dot(ys.astype(BF16), wglu_ref[...]) + bglu_ref[...])
    ys = ys * _silu(gs5_ref[...])
    acc = _dot(ys.astype(BF16), wo_ref[0:BW, :])
    acc = acc + _dot(yrw_ref[...].astype(BF16), wo_ref[BW:2 * BW, :])
    acc = acc + _dot(ym2_ref[...].astype(BF16), wo_ref[2 * BW:3 * BW, :])
    hn = h_ref[...] + acc
    if final:
        ms = jnp.mean(hn * hn, axis=-1, keepdims=True)
        hn = hn * lax.rsqrt(ms + NORM_EPS) * fg_ref[...]
    o_ref[...] = hn


def _s5out(h, proj, yrw, ym2, tb, d_row, w_glu, b_glu, w_out, final_g, final):
    L = h.shape[0]
    TM = OUT_TM
    NS = S5_GROUPS * S5_STATE
    tri = jnp.tril(jnp.ones((S5_CT, S5_CT), BF16))
    tok = pl.BlockSpec((TM, BW), lambda i: (i, 0))
    consts = [tb['wb'], tb['wc_re'], tb['wc_im'], tb['zr'], tb['zi'], tb['sr'], tb['si'], tb['car'],
              d_row, tri, w_glu, b_glu, w_out, final_g]
    const_specs = [pl.BlockSpec(c.shape, (lambda i: (0, 0, 0)) if c.ndim == 3 else (lambda i: (0, 0)))
                   for c in consts]
    return pl.pallas_call(
        functools.partial(_s5out_kernel, final=final),
        grid=(L // TM,),
        in_specs=[
            pl.BlockSpec((TM, D_MODEL), lambda i: (i, 0)),
            pl.BlockSpec((TM, BW), lambda i: (i, COL_U // BW)),
            pl.BlockSpec((TM, BW), lambda i: (i, COL_GS5 // BW)),
            tok,
            tok,
        ] + const_specs,
        out_specs=pl.BlockSpec((TM, D_MODEL), lambda i: (i, 0)),
        out_shape=jax.ShapeDtypeStruct((L, D_MODEL), F32),
        scratch_shapes=[pltpu.VMEM((SUBLANES, NS), F32), pltpu.VMEM((TM, NS), BF16), pltpu.VMEM((TM, NS), BF16)],
        compiler_params=pltpu.CompilerParams(
            dimension_semantics=("arbitrary",), vmem_limit_bytes=VMEM_LIMIT),
    )(h, proj, proj, yrw, ym2, *consts)


def _s5_stages(box, u_ref, wb_ref, wcr_ref, wci_ref, zr_ref, zi_ref, sr_ref, si_ref, car_ref, d_ref, tri_ref,
               st_ref, sre_ref, sim_ref):
    T = S5_CT
    NS = S5_GROUPS * S5_STATE
    tiles = S5_GROUPS // S5_TILE_G
    half = NS // tiles
    u = u_ref[...]
    ub = u.astype(BF16)
    bu = []
    for j in range(tiles):
        bu.append(_dot(ub[:, j * LANES:(j + 1) * LANES], wb_ref[j]))
        yield
    bu_re = jnp.concatenate([b[:, :half] for b in bu], axis=1)
    bu_im = jnp.concatenate([b[:, half:] for b in bu], axis=1)
    zr, zi, sr, si = zr_ref[...], zi_ref[...], sr_ref[...], si_ref[...]
    car_re, car_im = car_ref[0:1, :], car_ref[1:2, :]
    p_re, p_im = st_ref[0:1, :], st_ref[1:2, :]
    tri = tri_ref[...]
    for c in range(u.shape[0] // T):
        rows = slice(c * T, (c + 1) * T)
        b_re, b_im = bu_re[rows], bu_im[rows]
        z = jnp.concatenate([zr * b_re - zi * b_im, zr * b_im + zi * b_re], axis=1).astype(BF16)
        yield
        w = _dot(tri, z)
        yield
        w_re = w[:, :NS] + (car_re * p_re - car_im * p_im)
        w_im = w[:, NS:] + (car_re * p_im + car_im * p_re)
        s_re = sr * w_re - si * w_im
        s_im = sr * w_im + si * w_re
        p_re, p_im = s_re[T - 1:T, :], s_im[T - 1:T, :]
        sre_ref[rows, :] = s_re.astype(BF16)
        sim_ref[rows, :] = s_im.astype(BF16)
        yield
    st_ref[0:1, :] = p_re
    st_ref[1:2, :] = p_im
    ys = []
    for j in range(tiles):
        ys.append(_dot(sre_ref[:, j * half:(j + 1) * half], wcr_ref[j])
                  + _dot(sim_ref[:, j * half:(j + 1) * half], wci_ref[j]))
        yield
    box['ys'] = jnp.concatenate(ys, axis=1) + d_ref[...] * u


N_S5_IN = 17


def _layer_kernel(*refs, has_vmix, final):
    n_rw_in = 22 if has_vmix else 16
    rw_in = refs[:n_rw_in]
    m_in = refs[n_rw_in:n_rw_in + N_MAMBA_IN]
    s5_in = refs[n_rw_in + N_MAMBA_IN:n_rw_in + N_MAMBA_IN + N_S5_IN]
    rest = refs[n_rw_in + N_MAMBA_IN + N_S5_IN:]
    (h_ref, u_ref, gs5_ref, wb_ref, wcr_ref, wci_ref, zr_ref, zi_ref, sr_ref, si_ref, car_ref, d_ref, tri5_ref,
     wglu_ref, bglu_ref, wo_ref, fg_ref) = s5_in
    if has_vmix:
        o_ref, s_ref, st_ref, s5st_ref, sre_ref, sim_ref, yrw_ref, ym2_ref = rest
        rw_refs = (*rw_in, yrw_ref, s_ref)
    else:
        o_ref, vout_ref, s_ref, st_ref, s5st_ref, sre_ref, sim_ref, yrw_ref, ym2_ref = rest
        rw_refs = (*rw_in, yrw_ref, vout_ref, s_ref)
    first = pl.program_id(0) == 0

    @pl.when(first)
    def _():
        s_ref[...] = jnp.zeros_like(s_ref)
        st_ref[...] = jnp.zeros_like(st_ref)
        s5st_ref[...] = jnp.zeros_like(s5st_ref)

    box = {}
    mamba = _mamba_stages(first, *m_in, ym2_ref, st_ref)
    s5 = _s5_stages(box, u_ref, wb_ref, wcr_ref, wci_ref, zr_ref, zi_ref, sr_ref, si_ref, car_ref, d_ref,
                    tri5_ref, s5st_ref, sre_ref, sim_ref)
    calls = [0]

    def tick():
        calls[0] += 1
        next(mamba if calls[0] % 2 == 0 else s5, None)

    _rwkv_body(first, tick, rw_refs, has_vmix)
    for _ in mamba:
        pass
    for _ in s5:
        pass
    ys = _gelu_tanh(box['ys'])
    ys = ys * _sigmoid(_dot(ys.astype(BF16), wglu_ref[...]) + bglu_ref[...])
    ys = ys * _silu(gs5_ref[...])
    acc = _dot(ys.astype(BF16), wo_ref[0:BW, :])
    acc = acc + _dot(yrw_ref[...].astype(BF16), wo_ref[BW:2 * BW, :])
    acc = acc + _dot(ym2_ref[...].astype(BF16), wo_ref[2 * BW:3 * BW, :])
    hn = h_ref[...] + acc
    if final:
        ms = jnp.mean(hn * hn, axis=-1, keepdims=True)
        hn = hn * lax.rsqrt(ms + NORM_EPS) * fg_ref[...]
    o_ref[...] = hn


def _layer(h, proj, rp, mp, tb, d_row, w_glu, b_glu, w_out, final_g, v_first, final):
    L = proj.shape[0]
    TB = MIX_TB
    NS = S5_GROUPS * S5_STATE
    has_vmix = v_first is not None
    hb = TB // SUBLANES

    def cur(width, col):
        return pl.BlockSpec((TB, width), lambda i: (i, col // width))

    def halo(width, col):
        return pl.BlockSpec((SUBLANES, width), lambda i: (jnp.maximum(i * hb - 1, 0), col // width))

    def full(a):
        return pl.BlockSpec(a.shape, (lambda i: (0, 0, 0)) if a.ndim == 3 else (lambda i: (0, 0)))

    pos = np.arange(TB)
    same = (pos[:, None] // RW_T) == (pos[None, :] // RW_T)
    rw_tri = jnp.asarray(np.concatenate([same & (pos[None, :] <= pos[:, None]),
                                         same & (pos[None, :] > pos[:, None])], axis=0), BF16)
    args = [proj, proj, proj, proj]
    specs = [cur(4 * BW, COL_RW), halo(4 * BW, COL_RW), cur(LANES, COL_WA), halo(LANES, COL_WA)]
    if has_vmix:
        args += [proj, proj, v_first]
        specs += [cur(LANES, COL_MISC), halo(LANES, COL_MISC), pl.BlockSpec((TB, BW), lambda i: (i, 0))]
        names = ['mu_rw', 'mu_wa', 'mu_misc', 'w_up', 'w0', 'a_up', 'a0', 'k_k', 'k_a', 'r_k',
                 'ln_g', 'ln_b', 'v_up', 'v0']
    else:
        names = ['mu_rw', 'mu_wa', 'w_up', 'w0', 'a_up', 'a0', 'k_k', 'k_a', 'r_k', 'ln_g', 'ln_b']
    consts = [rp[n] for n in names] + [rw_tri]
    args += consts
    specs += [full(c) for c in consts]
    assert len(args) == (22 if has_vmix else 16)

    m_tri = jnp.tril(jnp.ones((M2_T, M2_T), BF16))
    m_consts = [mp['conv_w'], mp['conv_b'], mp['dt_bias'], mp['a_log'], mp['d_full'], mp['norm_g'],
                mp['expand'], m_tri]
    args += [proj, proj, proj, proj] + m_consts
    specs += [cur(M2_XBC, COL_XBC), halo(M2_XBC, COL_XBC), cur(BW, COL_Z), cur(LANES, COL_MISC)]
    specs += [full(c) for c in m_consts]

    s5_tri = jnp.tril(jnp.ones((S5_CT, S5_CT), BF16))
    s5_consts = [tb['wb'], tb['wc_re'], tb['wc_im'], tb['zr'], tb['zi'], tb['sr'], tb['si'], tb['car'],
                 d_row, s5_tri, w_glu, b_glu, w_out, final_g]
    args += [h, proj, proj] + s5_consts
    specs += [pl.BlockSpec((TB, D_MODEL), lambda i: (i, 0)), cur(BW, COL_U), cur(BW, COL_GS5)]
    specs += [full(c) for c in s5_consts]
    assert len(s5_consts) + 3 == N_S5_IN

    tok = pl.BlockSpec((TB, BW), lambda i: (i, 0))
    out_specs = [pl.BlockSpec((TB, D_MODEL), lambda i: (i, 0))]
    out_shape = [jax.ShapeDtypeStruct((L, D_MODEL), F32)]
    if not has_vmix:
        out_specs.append(tok)
        out_shape.append(jax.ShapeDtypeStruct((L, BW), F32))
    res = pl.pallas_call(
        functools.partial(_layer_kernel, has_vmix=has_vmix, final=final),
        grid=(L // TB,),
        in_specs=specs,
        out_specs=out_specs,
        out_shape=out_shape,
        scratch_shapes=[pltpu.VMEM((N_PAIRS, LANES, LANES), F32), pltpu.VMEM((N_PAIRS, M2_STATE, LANES), F32),
                        pltpu.VMEM((SUBLANES, NS), F32), pltpu.VMEM((TB, NS), BF16), pltpu.VMEM((TB, NS), BF16),
                        pltpu.VMEM((TB, BW), F32), pltpu.VMEM((TB, BW), F32)],
        compiler_params=pltpu.CompilerParams(
            dimension_semantics=("arbitrary",), vmem_limit_bytes=VMEM_LIMIT),
    )(*args)
    return res[0], (v_first if has_vmix else res[1])


def _proj_stages(h_ref, g_ref, w_ref, dst_ref):
    x = h_ref[...]
    ms = jnp.mean(x * x, axis=-1, keepdims=True)
    hn = (x * lax.rsqrt(ms + NORM_EPS) * g_ref[...]).astype(BF16)
    yield
    for j in range(0, PROJ_PAD, PROJ_TN):
        w = min(PROJ_TN, PROJ_PAD - j)
        dst_ref[:, j:j + w] = _dot(hn, w_ref[:, j:j + w])
        yield


N_PROJ_IN = 4
N_RW_CONST = {True: 15, False: 12}
N_M_CONST = 8
N_S5_CONST = 14


def _fused_kernel(*refs, has_vmix, final):
    it = iter(refs)

    def take(n):
        return [next(it) for _ in range(n)]

    h0_ref, hnext_ref, g_ref, w_ref = take(N_PROJ_IN)
    h_ref, = take(1)
    vfirst_in = take(1) if has_vmix else []
    rw_consts = take(N_RW_CONST[has_vmix])
    m_consts = take(N_M_CONST)
    s5_consts = take(N_S5_CONST)
    o_ref, = take(1)
    vout = take(1) if not has_vmix else []
    s_ref, st_ref, s5st_ref, sre_ref, sim_ref, yrw_ref, ym2_ref, proj_a, proj_b, halo_ref = take(10)
    (wb_ref, wcr_ref, wci_ref, zr_ref, zi_ref, sr_ref, si_ref, car_ref, d_ref, tri5_ref,
     wglu_ref, bglu_ref, wo_ref, fg_ref) = s5_consts
    i = pl.program_id(0)
    first = i == 0

    @pl.when(first)
    def _():
        s_ref[...] = jnp.zeros_like(s_ref)
        st_ref[...] = jnp.zeros_like(st_ref)
        s5st_ref[...] = jnp.zeros_like(s5st_ref)
        halo_ref[...] = jnp.zeros_like(halo_ref)
        for _ in _proj_stages(h0_ref, g_ref, w_ref, proj_a):
            pass

    def cols(ref, col, width):
        return ref.at[:, col:col + width]

    def body(cur, nxt):
        rw_in = [cols(cur, COL_RW, 4 * BW), cols(halo_ref, COL_RW, 4 * BW),
                 cols(cur, COL_WA, LANES), cols(halo_ref, COL_WA, LANES)]
        if has_vmix:
            rw_in += [cols(cur, COL_MISC, LANES), cols(halo_ref, COL_MISC, LANES)] + vfirst_in
        rw_refs = (*rw_in, *rw_consts, yrw_ref, *vout, s_ref)
        m_in = [cols(cur, COL_XBC, M2_XBC), cols(halo_ref, COL_XBC, M2_XBC), cols(cur, COL_Z, BW),
                cols(cur, COL_MISC, LANES)]
        box = {}
        mamba = _mamba_stages(first, *m_in, *m_consts, ym2_ref, st_ref)
        s5 = _s5_stages(box, cols(cur, COL_U, BW), wb_ref, wcr_ref, wci_ref, zr_ref, zi_ref, sr_ref, si_ref,
                        car_ref, d_ref, tri5_ref, s5st_ref, sre_ref, sim_ref)
        proj = _proj_stages(hnext_ref, g_ref, w_ref, nxt)
        calls = [0]

        def tick():
            calls[0] += 1
            next(mamba if calls[0] % 2 == 0 else s5, None)
            if calls[0] % 4 == 1:
                next(proj, None)

        _rwkv_body(first, tick, rw_refs, has_vmix)
        for gen in (mamba, s5, proj):
            for _ in gen:
                pass
        ys = _gelu_tanh(box['ys'])
        ys = ys * _sigmoid(_dot(ys.astype(BF16), wglu_ref[...]) + bglu_ref[...])
        ys = ys * _silu(cur[:, COL_GS5:COL_GS5 + BW])
        acc = _dot(ys.astype(BF16), wo_ref[0:BW, :])
        acc = acc + _dot(yrw_ref[...].astype(BF16), wo_ref[BW:2 * BW, :])
        acc = acc + _dot(ym2_ref[...].astype(BF16), wo_ref[2 * BW:3 * BW, :])
        hn = h_ref[...] + acc
        if final:
            ms = jnp.mean(hn * hn, axis=-1, keepdims=True)
            hn = hn * lax.rsqrt(ms + NORM_EPS) * fg_ref[...]
        o_ref[...] = hn
        halo_ref[...] = cur[MIX_TB - SUBLANES:MIX_TB, :]

    pl.when(i % 2 == 0)(functools.partial(body, proj_a, proj_b))
    pl.when(i % 2 == 1)(functools.partial(body, proj_b, proj_a))


def _fused_layer(h, g_row, w_cat, rp, mp, tb, d_row, w_glu, b_glu, w_out, final_g, v_first, final):
    L = h.shape[0]
    TB = MIX_TB
    nb = L // TB
    NS = S5_GROUPS * S5_STATE
    has_vmix = v_first is not None

    def const(a):
        return pl.BlockSpec(a.shape, (lambda i: (0, 0, 0)) if a.ndim == 3 else (lambda i: (0, 0)),
                            pipeline_mode=pl.Buffered(1))

    hblk = lambda fn: pl.BlockSpec((TB, D_MODEL), fn)
    tok = pl.BlockSpec((TB, BW), lambda i: (i, 0))
    args = [h, h, g_row, w_cat, h]
    specs = [hblk(lambda i: (0, 0)), hblk(lambda i: (jnp.minimum(i + 1, nb - 1), 0)), const(g_row), const(w_cat),
             hblk(lambda i: (i, 0))]
    assert len(args) == N_PROJ_IN + 1
    if has_vmix:
        args.append(v_first)
        specs.append(tok)
        names = ['mu_rw', 'mu_wa', 'mu_misc', 'w_up', 'w0', 'a_up', 'a0', 'k_k', 'k_a', 'r_k',
                 'ln_g', 'ln_b', 'v_up', 'v0']
    else:
        names = ['mu_rw', 'mu_wa', 'w_up', 'w0', 'a_up', 'a0', 'k_k', 'k_a', 'r_k', 'ln_g', 'ln_b']
    pos = np.arange(TB)
    same = (pos[:, None] // RW_T) == (pos[None, :] // RW_T)
    rw_tri = jnp.asarray(np.concatenate([same & (pos[None, :] <= pos[:, None]),
                                         same & (pos[None, :] > pos[:, None])], axis=0), BF16)
    rw_consts = [rp[n] for n in names] + [rw_tri]
    assert len(rw_consts) == N_RW_CONST[has_vmix]
    m_consts = [mp['conv_w'], mp['conv_b'], mp['dt_bias'], mp['a_log'], mp['d_full'], mp['norm_g'],
                mp['expand'], jnp.tril(jnp.ones((M2_T, M2_T), BF16))]
    assert len(m_consts) == N_M_CONST
    s5_consts = [tb['wb'], tb['wc_re'], tb['wc_im'], tb['zr'], tb['zi'], tb['sr'], tb['si'], tb['car'],
                 d_row, jnp.tril(jnp.ones((S5_CT, S5_CT), BF16)), w_glu, b_glu, w_out, final_g]
    assert len(s5_consts) == N_S5_CONST
    for c in rw_consts + m_consts + s5_consts:
        args.append(c)
        specs.append(const(c))

    out_specs = [pl.BlockSpec((TB, D_MODEL), lambda i: (i, 0))]
    out_shape = [jax.ShapeDtypeStruct((L, D_MODEL), F32)]
    if not has_vmix:
        out_specs.append(tok)
        out_shape.append(jax.ShapeDtypeStruct((L, BW), F32))
    res = pl.pallas_call(
        functools.partial(_fused_kernel, has_vmix=has_vmix, final=final),
        grid=(nb,),
        in_specs=specs,
        out_specs=out_specs,
        out_shape=out_shape,
        scratch_shapes=[pltpu.VMEM((N_PAIRS, LANES, LANES), F32), pltpu.VMEM((N_PAIRS, M2_STATE, LANES), F32),
                        pltpu.VMEM((SUBLANES, NS), F32), pltpu.VMEM((TB, NS), BF16), pltpu.VMEM((TB, NS), BF16),
                        pltpu.VMEM((TB, BW), F32), pltpu.VMEM((TB, BW), F32),
                        pltpu.VMEM((TB, PROJ_PAD), F32), pltpu.VMEM((TB, PROJ_PAD), F32),
                        pltpu.VMEM((SUBLANES, PROJ_PAD), F32)],
        compiler_params=pltpu.CompilerParams(
            dimension_semantics=("arbitrary",), vmem_limit_bytes=FUSED_VMEM_LIMIT),
    )(*args)
    return res[0], (v_first if has_vmix else res[1])


def _reorder_w_in(w, w_vmix):
    o = np.cumsum([0, BW, BW, BW, BW, BW, LORA_W, LORA_A, BW, M2_XBC, N_HEADS, BW])
    u, gs5, r, k, v, wd, ad, g, xbc, dt, z = (w[:, o[i]:o[i + 1]] for i in range(11))
    vm = w_vmix if w_vmix is not None else jnp.zeros((w.shape[0], LORA_V), w.dtype)
    pad = jnp.zeros((w.shape[0], LANES - N_HEADS - LORA_V), w.dtype)
    return jnp.concatenate([r, k, v, g, u, gs5, xbc, z, wd, ad, dt, vm, pad], axis=1).astype(BF16)


def _pad_rows(w, start, total=LANES):
    return jnp.pad(w.astype(F32), ((start, total - start - w.shape[0]), (0, 0)))


def _row(a):
    return a.astype(F32).reshape(1, -1)


def kernel(x, norm_g, w_in, w_in_vmix, s5_log_dt, s5_a_re, s5_a_im, s5_b_re, s5_b_im, s5_c_re, s5_c_im, s5_d, s5_w_glu, s5_b_glu, rwkv_mu, rwkv_w_up, rwkv_w0, rwkv_a_up, rwkv_a0, rwkv_k_k, rwkv_k_a, rwkv_r_k, rwkv_ln_g, rwkv_ln_b, rwkv_vmix_mu, rwkv_v_up, rwkv_v0, m2_conv_w, m2_conv_b, m2_dt_bias, m2_a_log, m2_d, m2_norm_g, w_out, final_norm_g):
    bsz, L, d = x.shape
    assert bsz == 1 and d == D_MODEL
    assert L % max(PROJ_TM, OUT_TM, MIX_TB) == 0 and MIX_TB % M2_T == 0 and OUT_TM % S5_CT == 0
    depth = w_in.shape[0]
    h = x.reshape(L, d).astype(F32)
    expand = jnp.pad(jnp.repeat(jnp.eye(N_HEADS, dtype=F32), HEAD, axis=1), ((0, LANES - N_HEADS), (0, 0)))
    v_first = None
    for i in range(depth):
        w_cat = _reorder_w_in(w_in[i], w_in_vmix[i - 1] if i > 0 else None)

        mu = rwkv_mu[i].astype(F32)
        mo = np.cumsum([0, BW, BW, BW, LORA_W, LORA_A, BW])
        mr, mk, mv, mwd, mad, mg = (mu[mo[j]:mo[j + 1]] for j in range(6))
        rp = {
            'mu_rw': jnp.concatenate([mr, mk, mv, mg]).reshape(1, -1),
            'mu_wa': jnp.concatenate([mwd, mad]).reshape(1, -1),
            'w_up': _pad_rows(rwkv_w_up[i], 0).astype(BF16), 'w0': _row(rwkv_w0[i]),
            'a_up': _pad_rows(rwkv_a_up[i], LORA_W).astype(BF16), 'a0': _row(rwkv_a0[i]),
            'k_k': _row(rwkv_k_k[i]), 'k_a': _row(rwkv_k_a[i]), 'r_k': _row(rwkv_r_k[i]),
            'ln_g': _row(rwkv_ln_g[i]), 'ln_b': _row(rwkv_ln_b[i]),
        }
        if i > 0:
            rp['mu_misc'] = jnp.pad(rwkv_vmix_mu[i - 1].astype(F32),
                                    (N_HEADS, LANES - N_HEADS - LORA_V)).reshape(1, -1)
            rp['v_up'] = _pad_rows(rwkv_v_up[i - 1], N_HEADS).astype(BF16)
            rp['v0'] = _row(rwkv_v0[i - 1])
        mp = {
            'conv_w': m2_conv_w[i].astype(F32), 'conv_b': _row(m2_conv_b[i]),
            'dt_bias': jnp.pad(m2_dt_bias[i].astype(F32), (0, LANES - N_HEADS)).reshape(1, -1),
            'a_log': jnp.pad(m2_a_log[i].astype(F32), (0, LANES - N_HEADS)).reshape(1, -1),
            'd_full': jnp.repeat(m2_d[i].astype(F32), HEAD).reshape(1, -1),
            'norm_g': _row(m2_norm_g[i]), 'expand': expand.astype(BF16),
        }
        s5_tb = _s5_state_tables(s5_log_dt[i], s5_a_re[i], s5_a_im[i], s5_b_re[i], s5_b_im[i],
                                 s5_c_re[i], s5_c_im[i])
        h, v_first = _fused_layer(h, _row(norm_g[i]), w_cat, rp, mp, s5_tb, _row(s5_d[i]),
                                  s5_w_glu[i].astype(BF16), _row(s5_b_glu[i]), w_out[i].astype(BF16),
                                  _row(final_norm_g), v_first, final=(i == depth - 1))
    return h.reshape(bsz, L, d).astype(x.dtype)
```

```python
import functools
import math

import jax
import jax.numpy as jnp
import numpy as np
from jax import lax
from jax.experimental import pallas as pl
from jax.experimental.pallas import tpu as pltpu

F32 = jnp.float32
BF16 = jnp.bfloat16
HI = lax.Precision.HIGHEST

D_MODEL = 1024
BW = 512
S5_GROUP = 16
S5_GROUPS = 32
S5_STATE = 64
HEAD = 64
N_HEADS = 8
N_PAIRS = N_HEADS // 2
LORA_W = 64
LORA_A = 64
LORA_V = 32
GN_EPS = 64e-5
M2_GROUPS = 2
M2_STATE = 128
M2_CONV = 4
M2_XBC = 1024
NORM_EPS = 1e-5

LANES = 128
SUBLANES = 8

COL_RW = 0
COL_U = 2048
COL_GS5 = 2560
COL_XBC = 3072
COL_Z = 4096
COL_WA = 4608
COL_MISC = 4736
PROJ_PAD = 4864

PROJ_TM = 256
PROJ_TN = 512
S5_CT = 64
S5_TILE_G = LANES // S5_GROUP
RW_T = 64
RW_NB = 4
RW_GC = 4
MIX_TB = RW_T * RW_NB
M2_T = 128
M2_TICK = 2
RW_TICKS_PER_GROUP = 21
M2_STAGES = 6 + 7 * (MIX_TB // M2_T)
S5_STAGES = 8 + 3 * (MIX_TB // S5_CT)
OUT_TM = 256
VMEM_LIMIT = 48 * 1024 * 1024
FUSED_VMEM_LIMIT = 56 * 1024 * 1024


def _dot(a, b, prec=None):
    return jnp.dot(a, b, precision=prec, preferred_element_type=F32)


def _dot_nt(a, b, prec=None):
    return lax.dot_general(a, b, (((1,), (1,)), ((), ())), precision=prec, preferred_element_type=F32)


def _dot_tn(a, b, prec=None):
    return lax.dot_general(a, b, (((0,), (0,)), ((), ())), precision=prec, preferred_element_type=F32)


def _split3(x):
    hi = x.astype(BF16)
    r1 = x - hi.astype(F32)
    mid = r1.astype(BF16)
    lo = (r1 - mid.astype(F32)).astype(BF16)
    return hi, mid, lo


def _dot_split_rhs(m, x):
    hi, mid, lo = _split3(x)
    return _dot(m, hi) + _dot(m, mid) + _dot(m, lo)


def _dot_split_lhs(x, m):
    hi, mid, lo = _split3(x)
    return _dot(hi, m) + _dot(mid, m) + _dot(lo, m)


def _b16(x):
    return x.astype(BF16)


def _sigmoid(x):
    return 1.0 / (1.0 + jnp.exp(-x))


def _silu(x):
    return x * _sigmoid(x)


def _softplus(x):
    return jnp.maximum(x, 0.0) + jnp.log1p(jnp.exp(-jnp.abs(x)))


def _gelu_tanh(x):
    c = math.sqrt(2.0 / math.pi)
    return 0.5 * x * (1.0 + jnp.tanh(c * (x + 0.044715 * (x * x * x))))


def _prev_rows(x, halo, is_first):
    rolled = pltpu.roll(x, 1, axis=0)
    last = jnp.where(is_first, 0.0, halo[SUBLANES - 1:SUBLANES, :])
    row = lax.broadcasted_iota(jnp.int32, x.shape, 0)
    return jnp.where(row == 0, last, rolled)


def _proj_kernel(h_ref, g_ref, w_ref, o_ref):
    x = h_ref[...]
    ms = jnp.mean(x * x, axis=-1, keepdims=True)
    hn = (x * lax.rsqrt(ms + NORM_EPS) * g_ref[...]).astype(BF16)
    for j in range(0, PROJ_PAD, PROJ_TN):
        w = min(PROJ_TN, PROJ_PAD - j)
        o_ref[:, j:j + w] = _dot(hn, w_ref[:, j:j + w])


def _proj(h, g, w):
    L = h.shape[0]
    return pl.pallas_call(
        _proj_kernel,
        grid=(L // PROJ_TM,),
        in_specs=[
            pl.BlockSpec((PROJ_TM, D_MODEL), lambda i: (i, 0)),
            pl.BlockSpec((1, D_MODEL), lambda i: (0, 0)),
            pl.BlockSpec((D_MODEL, PROJ_PAD), lambda i: (0, 0)),
        ],
        out_specs=pl.BlockSpec((PROJ_TM, PROJ_PAD), lambda i: (i, 0)),
        out_shape=jax.ShapeDtypeStruct((L, PROJ_PAD), F32),
        compiler_params=pltpu.CompilerParams(
            dimension_semantics=("parallel",), vmem_limit_bytes=VMEM_LIMIT),
    )(h, g, w)


def _mamba_stages(first, xbc_ref, xbcp_ref, z_ref, misc_ref, cw_ref, cb_ref, dtb_ref, alog_ref, dfull_ref,
                  ng_ref, expand_ref, tri_ref, y_ref, st_ref):
    T = M2_T
    x = xbc_ref[...]
    halo = jnp.where(first, 0.0, xbcp_ref[...])
    row8 = lax.broadcasted_iota(jnp.int32, halo.shape, 0)
    acc = x * cw_ref[M2_CONV - 1:M2_CONV, :] + cb_ref[...]
    yield
    for j in range(1, M2_CONV):
        xr = pltpu.roll(x, j, axis=0)
        head = jnp.where(row8 < j, pltpu.roll(halo, j, axis=0), xr[:SUBLANES])
        xj = jnp.concatenate([head, xr[SUBLANES:]], axis=0)
        acc = acc + xj * cw_ref[M2_CONV - 1 - j:M2_CONV - j, :]
        yield
    xbc = _silu(acc)
    xm = xbc[:, :BW]
    bm_b = _b16(xbc[:, BW:BW + M2_GROUPS * M2_STATE])
    cm_b = _b16(xbc[:, BW + M2_GROUPS * M2_STATE:])
    yield
    dt = _softplus(misc_ref[...] + dtb_ref[...])
    da = dt * (-jnp.exp(alog_ref[...]))
    expand = expand_ref[...]
    xdt = xm * _dot_split_lhs(dt, expand)
    xdt_b = _b16(xdt)
    gate = _silu(z_ref[...])
    yield

    rr = lax.broadcasted_iota(jnp.int32, (T, T), 0)
    cc = lax.broadcasted_iota(jnp.int32, (T, T), 1)
    causal = cc <= rr
    head0 = lax.broadcasted_iota(jnp.int32, (T, LANES), 1) < HEAD
    states = [st_ref[q] for q in range(N_PAIRS)]
    for ch in range(x.shape[0] // T):
        rows = slice(ch * T, (ch + 1) * T)
        cs = _dot_split_rhs(tri_ref[...], da[rows])
        cs_t = cs.T
        cs_full = _dot_split_lhs(cs, expand)
        cs_last = cs_full[T - 1:T, :]
        ecs = jnp.exp(cs_full)
        xdte_b = _b16(xdt[rows] * jnp.exp(cs_last - cs_full))
        chunk_decay = jnp.exp(cs_last)
        yield
        scores = [_dot_nt(cm_b[rows, gi * M2_STATE:(gi + 1) * M2_STATE], bm_b[rows, gi * M2_STATE:(gi + 1) * M2_STATE])
                  for gi in range(M2_GROUPS)]
        yield
        ys = []
        for q in range(N_PAIRS):
            gi = (2 * q) // (N_HEADS // M2_GROUPS)
            gs = slice(gi * M2_STATE, (gi + 1) * M2_STATE)
            sl = slice(q * LANES, (q + 1) * LANES)
            yd = []
            for h in (2 * q, 2 * q + 1):
                seg = cs[:, h:h + 1] - cs_t[h:h + 1, :]
                dec = jnp.where(causal, jnp.exp(jnp.minimum(seg, 0.0)), 0.0)
                yd.append(_dot(_b16(scores[gi] * dec), xdt_b[rows, sl]))
            st = states[q]
            y_off = _dot(cm_b[rows, gs], _b16(st)) * ecs[:, sl]
            states[q] = st * chunk_decay[:, sl] + _dot_tn(bm_b[rows, gs], xdte_b[:, sl])
            ys.append(jnp.where(head0, yd[0], yd[1]) + y_off)
            yield
        y = jnp.concatenate(ys, axis=1) + xm[rows] * dfull_ref[...]
        y = y * gate[rows]
        ms = jnp.mean(y * y, axis=-1, keepdims=True)
        y_ref[rows, :] = y * lax.rsqrt(ms + NORM_EPS) * ng_ref[...]
        yield
    for q in range(N_PAIRS):
        st_ref[q] = states[q]


def _rwkv_body(first, tick, refs, has_vmix):
    if has_vmix:
        (rw_ref, rwp_ref, wa_ref, wap_ref, misc_ref, miscp_ref, vfirst_ref,
         mu_rw_ref, mu_wa_ref, mu_misc_ref, wup_ref, w0_ref, aup_ref, a0_ref, kk_ref, ka_ref, rk_ref,
         lng_ref, lnb_ref, vup_ref, v0_ref, tri_ref, y_ref, s_ref) = refs
    else:
        (rw_ref, rwp_ref, wa_ref, wap_ref,
         mu_rw_ref, mu_wa_ref, wup_ref, w0_ref, aup_ref, a0_ref, kk_ref, ka_ref, rk_ref,
         lng_ref, lnb_ref, tri_ref, y_ref, vout_ref, s_ref) = refs
    T = RW_T
    TB = MIX_TB
    b16 = _b16

    def lerp(ref, halo_ref, mu_ref):
        x = ref[...]
        return x + (_prev_rows(x, halo_ref[...], first) - x) * mu_ref[...]

    xs = lerp(rw_ref, rwp_ref, mu_rw_ref)
    r, k, v, g = (xs[:, i * BW:(i + 1) * BW] for i in range(4))
    wa = lerp(wa_ref, wap_ref, mu_wa_ref)
    w_log = -_softplus(-(w0_ref[...] + _dot(b16(jnp.tanh(wa)), wup_ref[...]))) - 0.5
    lw = -jnp.exp(w_log)
    a = _sigmoid(a0_ref[...] + _dot(b16(wa), aup_ref[...]))
    if has_vmix:
        vd = lerp(misc_ref, miscp_ref, mu_misc_ref)
        mix = _sigmoid(v0_ref[...] + _dot(b16(vd), vup_ref[...]))
        v = v + (vfirst_ref[...] - v) * mix
    else:
        vout_ref[...] = v
    kk = k * kk_ref[...]
    kmod = k * (1.0 + (a - 1.0) * ka_ref[...])
    cums = _dot_split_rhs(tri_ref[...], lw)
    cl, rv = cums[:TB], cums[TB:]
    g_incl = jnp.exp(cl)
    g_inv = jnp.exp(-cl)
    g_end = jnp.exp(rv)
    rkr = r * kmod * rk_ref[...]

    def per_head_sum(x):
        h0 = lax.broadcasted_iota(jnp.int32, x.shape, 1) < HEAD
        f0 = h0.astype(F32)
        s0 = jnp.sum(jnp.where(h0, x, 0.0), axis=-1, keepdims=True)
        s1 = jnp.sum(jnp.where(h0, 0.0, x), axis=-1, keepdims=True)
        return s0 * f0 + s1 * (1.0 - f0)

    inv, bonus_w = [], []
    for q in range(N_PAIRS):
        sl = slice(q * LANES, (q + 1) * LANES)
        sq = kk[:, sl] * kk[:, sl]
        inv.append(1.0 / jnp.maximum(jnp.sqrt(per_head_sum(sq)), 1e-12))
        bonus_w.append(per_head_sum(rkr[:, sl]))
    kkn = kk * jnp.concatenate(inv, axis=1)
    bonus = jnp.concatenate(bonus_w, axis=1) * v
    kkna = kkn * a
    na_b = b16(-kkn * jnp.exp(cl - lw))
    nb_b = b16(kkna * g_inv)
    nbe_b = b16(kkna * g_end)
    pk_b = b16(kmod * g_inv)
    pke_b = b16(kmod * g_end)
    pr = r * g_incl
    pr_b = b16(pr)
    v_b = b16(v)
    gated = _silu(g)

    m0 = (lax.broadcasted_iota(jnp.int32, (1, LANES), 1) < HEAD).astype(BF16)
    m1 = 1.0 - m0
    row_w = lax.broadcasted_iota(jnp.int32, (T, LANES), 0)
    col_w = lax.broadcasted_iota(jnp.int32, (T, LANES), 1) % HEAD
    strict = col_w < row_w
    incl = col_w <= row_w
    eye_w = (col_w == row_w).astype(F32)
    rr = lax.broadcasted_iota(jnp.int32, (LANES, LANES), 0)
    cc = lax.broadcasted_iota(jnp.int32, (LANES, LANES), 1)
    same_head = (rr < HEAD) == (cc < HEAD)

    def stack(x):
        return jnp.concatenate([x * m0, x * m1], axis=0)

    def rows(c):
        return slice(c * T, (c + 1) * T)

    def lanes(q):
        return slice(q * LANES, (q + 1) * LANES)

    state = [s_ref[q] for q in range(N_PAIRS)]
    for c0 in range(0, RW_NB, RW_GC):
        units = [(c, q) for c in range(c0, c0 + RW_GC) for q in range(N_PAIRS)]
        a_n = [na_b[rows(c), lanes(q)] for c, q in units]
        r_n = [pr_b[rows(c), lanes(q)] for c, q in units]
        v_n = [v_b[rows(c), lanes(q)] for c, q in units]
        be_n = [nbe_b[rows(c), lanes(q)] for c, q in units]
        ke_n = [pke_b[rows(c), lanes(q)] for c, q in units]
        b_st = [stack(nb_b[rows(c), lanes(q)]) for c, q in units]
        k_st = [stack(pk_b[rows(c), lanes(q)]) for c, q in units]
        a_st = [stack(x) for x in a_n]
        v_st = [stack(x) for x in v_n]
        tick()
        aa = [_dot_nt(jnp.concatenate([a_, r_], axis=0), jnp.concatenate([b_, k_], axis=0))
              for a_, r_, b_, k_ in zip(a_n, r_n, b_st, k_st)]
        tick()
        a_ak = [b16(jnp.where(strict, x[:T, LANES:], 0.0)) for x in aa]
        a_rb = [b16(jnp.where(incl, x[T:, :LANES], 0.0)) for x in aa]
        a_rk = [b16(jnp.where(incl, x[T:, LANES:], 0.0)) for x in aa]
        p = [jnp.where(strict, x[:T, :LANES], 0.0) for x in aa]
        minv = [eye_w + x for x in p]
        tick()
        for _ in range(int(math.log2(T)) - 1):
            p_b = [b16(x) for x in p]
            p = [_dot(x, stack(x)) for x in p_b]
            tick()
            minv = [m + _dot(b16(x), stack(b16(m))) for x, m in zip(p, minv)]
            tick()
        akv = [_dot(x, y) for x, y in zip(a_ak, v_st)]
        tick()
        wu_b = [b16(_dot(b16(m), jnp.concatenate([a_, stack(b16(x))], axis=1)))
                for m, a_, x in zip(minv, a_st, akv)]
        tick()
        p_mat = [b16(jnp.where(same_head, _dot_tn(w[:, :LANES], be), 0.0)) for w, be in zip(wu_b, be_n)]
        tick()
        z_mat = [jnp.where(same_head,
                           _dot_tn(jnp.concatenate([w[:, LANES:], vn], axis=0), jnp.concatenate([be, ke], axis=0)),
                           0.0)
                 for w, vn, be, ke in zip(wu_b, v_n, be_n, ke_n)]
        tick()
        q_mat = [b16(pr[rows(c), lanes(q)] + _dot(arb, stack(w[:, :LANES])))
                 for (c, q), arb, w in zip(units, a_rb, wu_b)]
        tick()
        y0 = [_dot(jnp.concatenate([arb, ark], axis=1), jnp.concatenate([stack(w[:, LANES:]), vs], axis=0))
              for arb, ark, w, vs in zip(a_rb, a_rk, wu_b, v_st)]
        tick()
        ys = []
        for i, (c, q) in enumerate(units):
            s = state[q]
            s_b = b16(s)
            ys.append(_dot_nt(q_mat[i], s_b) + y0[i])
            state[q] = s * g_incl[(c + 1) * T - 1:(c + 1) * T, lanes(q)] + _dot(s_b, p_mat[i]) + z_mat[i]
        tick()
        for i, (c, q) in enumerate(units):
            y = ys[i]
            mean = per_head_sum(y) * (1.0 / HEAD)
            cen = y - mean
            var = per_head_sum(cen * cen) * (1.0 / HEAD)
            yn = cen * lax.rsqrt(var + GN_EPS)
            y_ref[rows(c), lanes(q)] = ((yn * lng_ref[:, lanes(q)] + lnb_ref[:, lanes(q)]
                                         + bonus[rows(c), lanes(q)]) * gated[rows(c), lanes(q)])
        tick()
    for q in range(N_PAIRS):
        s_ref[q] = state[q]


N_MAMBA_IN = 12


def _mixers_kernel(*refs, has_vmix):
    n_rw_in = 22 if has_vmix else 16
    rw_in, m_in = refs[:n_rw_in], refs[n_rw_in:n_rw_in + N_MAMBA_IN]
    rest = refs[n_rw_in + N_MAMBA_IN:]
    if has_vmix:
        y_rw_ref, y_m2_ref, s_ref, st_ref = rest
        rw_refs = (*rw_in, y_rw_ref, s_ref)
    else:
        y_rw_ref, vout_ref, y_m2_ref, s_ref, st_ref = rest
        rw_refs = (*rw_in, y_rw_ref, vout_ref, s_ref)
    first = pl.program_id(0) == 0

    @pl.when(first)
    def _():
        s_ref[...] = jnp.zeros_like(s_ref)
        st_ref[...] = jnp.zeros_like(st_ref)

    mamba = _mamba_stages(first, *m_in, y_m2_ref, st_ref)
    calls = [0]

    def tick():
        calls[0] += 1
        if calls[0] % M2_TICK == 0:
            next(mamba, None)

    _rwkv_body(first, tick, rw_refs, has_vmix)
    for _ in mamba:
        pass


def _mixers(proj, rp, mp, v_first):
    L = proj.shape[0]
    TB = MIX_TB
    has_vmix = v_first is not None
    hb = TB // SUBLANES

    def cur(width, col):
        return pl.BlockSpec((TB, width), lambda i: (i, col // width))

    def halo(width, col):
        return pl.BlockSpec((SUBLANES, width), lambda i: (jnp.maximum(i * hb - 1, 0), col // width))

    def full(a):
        return pl.BlockSpec(a.shape, lambda i: (0, 0))

    pos = np.arange(TB)
    same = (pos[:, None] // RW_T) == (pos[None, :] // RW_T)
    rw_tri = jnp.asarray(np.concatenate([same & (pos[None, :] <= pos[:, None]),
                                         same & (pos[None, :] > pos[:, None])], axis=0), BF16)
    args = [proj, proj, proj, proj]
    specs = [cur(4 * BW, COL_RW), halo(4 * BW, COL_RW), cur(LANES, COL_WA), halo(LANES, COL_WA)]
    if has_vmix:
        args += [proj, proj, v_first]
        specs += [cur(LANES, COL_MISC), halo(LANES, COL_MISC), pl.BlockSpec((TB, BW), lambda i: (i, 0))]
        names = ['mu_rw', 'mu_wa', 'mu_misc', 'w_up', 'w0', 'a_up', 'a0', 'k_k', 'k_a', 'r_k',
                 'ln_g', 'ln_b', 'v_up', 'v0']
    else:
        names = ['mu_rw', 'mu_wa', 'w_up', 'w0', 'a_up', 'a0', 'k_k', 'k_a', 'r_k', 'ln_g', 'ln_b']
    consts = [rp[n] for n in names] + [rw_tri]
    args += consts
    specs += [full(c) for c in consts]
    assert len(args) == (22 if has_vmix else 16)

    m_tri = jnp.tril(jnp.ones((M2_T, M2_T), BF16))
    m_consts = [mp['conv_w'], mp['conv_b'], mp['dt_bias'], mp['a_log'], mp['d_full'], mp['norm_g'],
                mp['expand'], m_tri]
    args += [proj, proj, proj, proj] + m_consts
    specs += [cur(M2_XBC, COL_XBC), halo(M2_XBC, COL_XBC), cur(BW, COL_Z), cur(LANES, COL_MISC)]
    specs += [full(c) for c in m_consts]
    assert len(m_consts) + 4 == N_MAMBA_IN

    tok = pl.BlockSpec((TB, BW), lambda i: (i, 0))
    tok_shape = jax.ShapeDtypeStruct((L, BW), F32)
    n_out = 2 if has_vmix else 3
    res = pl.pallas_call(
        functools.partial(_mixers_kernel, has_vmix=has_vmix),
        grid=(L // TB,),
        in_specs=specs,
        out_specs=[tok] * n_out,
        out_shape=[tok_shape] * n_out,
        scratch_shapes=[pltpu.VMEM((N_PAIRS, LANES, LANES), F32), pltpu.VMEM((N_PAIRS, M2_STATE, LANES), F32)],
        compiler_params=pltpu.CompilerParams(
            dimension_semantics=("arbitrary",), vmem_limit_bytes=VMEM_LIMIT),
    )(*args)
    if has_vmix:
        return res[0], res[1], v_first
    return res[0], res[2], res[1]


def _s5_state_tables(log_dt, a_re, a_im, b_re, b_im, c_re, c_im):
    T = S5_CT
    G, P = a_re.shape
    dt = jnp.exp(log_dt.astype(F32))[:, None]
    ar, ai = a_re.astype(F32), a_im.astype(F32)
    mag = jnp.exp(dt * ar)
    abar_re, abar_im = mag * jnp.cos(dt * ai), mag * jnp.sin(dt * ai)
    den = ar * ar + ai * ai
    f_re = ((abar_re - 1.0) * ar + abar_im * ai) / den
    f_im = (abar_im * ar - (abar_re - 1.0) * ai) / den
    br, bi = b_re.astype(F32), b_im.astype(F32)
    bb_re = f_re[..., None] * br - f_im[..., None] * bi
    bb_im = f_re[..., None] * bi + f_im[..., None] * br

    def powers(e):
        e = jnp.asarray(e, F32)[:, None, None]
        m = jnp.exp(e * (dt * ar)[None])
        ang = e * (dt * ai)[None]
        return (m * jnp.cos(ang)).reshape(-1, G * P), (m * jnp.sin(ang)).reshape(-1, G * P)

    pos = np.arange(T, dtype=np.float32)
    half = T / 2
    zr, zi = powers(half - pos)
    sr, si = powers(pos - half)
    cr, ci = powers(np.array([half + 1.0], np.float32))
    tiles = G // S5_TILE_G
    eye = jnp.eye(S5_TILE_G, dtype=F32)

    def b_side(bb):
        t = bb.reshape(tiles, S5_TILE_G, P, S5_GROUP).transpose(0, 1, 3, 2)
        return (t[:, :, :, None, :] * eye[None, :, None, :, None]).reshape(tiles, LANES, S5_TILE_G * P)

    def c_side(cc):
        t = cc.astype(F32).reshape(tiles, S5_TILE_G, S5_GROUP, P).transpose(0, 1, 3, 2)
        return (t[:, :, :, None, :] * eye[None, :, None, :, None]).reshape(tiles, S5_TILE_G * P, LANES)

    wb = jnp.concatenate([b_side(bb_re), b_side(bb_im)], axis=-1).astype(BF16)
    car = jnp.concatenate([cr, ci, sr[T - 1:T], si[T - 1:T]], axis=0)
    return dict(wb=wb, wc_re=c_side(c_re).astype(BF16), wc_im=c_side(-c_im.astype(F32)).astype(BF16),
                zr=zr.astype(BF16), zi=zi.astype(BF16), sr=sr.astype(BF16), si=si.astype(BF16), car=car)


def _s5out_kernel(h_ref, u_ref, gs5_ref, yrw_ref, ym2_ref, wb_ref, wcr_ref, wci_ref, zr_ref, zi_ref, sr_ref,
                  si_ref, car_ref, d_ref, tri_ref, wglu_ref, bglu_ref, wo_ref, fg_ref, o_ref,
                  st_ref, sre_ref, sim_ref, *, final):
    T = S5_CT
    NS = S5_GROUPS * S5_STATE
    tiles = S5_GROUPS // S5_TILE_G
    half = NS // tiles

    @pl.when(pl.program_id(0) == 0)
    def _():
        st_ref[...] = jnp.zeros_like(st_ref)

    u = u_ref[...]
    ub = u.astype(BF16)
    bu = [_dot(ub[:, j * LANES:(j + 1) * LANES], wb_ref[j]) for j in range(tiles)]
    bu_re = jnp.concatenate([b[:, :half] for b in bu], axis=1)
    bu_im = jnp.concatenate([b[:, half:] for b in bu], axis=1)
    zr, zi, sr, si = zr_ref[...], zi_ref[...], sr_ref[...], si_ref[...]
    car_re, car_im = car_ref[0:1, :], car_ref[1:2, :]
    p_re, p_im = st_ref[0:1, :], st_ref[1:2, :]
    tri = tri_ref[...]
    for c in range(OUT_TM // T):
        rows = slice(c * T, (c + 1) * T)
        b_re, b_im = bu_re[rows], bu_im[rows]
        z = jnp.concatenate([zr * b_re - zi * b_im, zr * b_im + zi * b_re], axis=1).astype(BF16)
        w = _dot(tri, z)
        w_re = w[:, :NS] + (car_re * p_re - car_im * p_im)
        w_im = w[:, NS:] + (car_re * p_im + car_im * p_re)
        s_re = sr * w_re - si * w_im
        s_im = sr * w_im + si * w_re
        p_re, p_im = s_re[T - 1:T, :], s_im[T - 1:T, :]
        sre_ref[rows, :] = s_re.astype(BF16)
        sim_ref[rows, :] = s_im.astype(BF16)
    st_ref[0:1, :] = p_re
    st_ref[1:2, :] = p_im
    ys = jnp.concatenate(
        [_dot(sre_ref[:, j * half:(j + 1) * half], wcr_ref[j]) + _dot(sim_ref[:, j * half:(j + 1) * half], wci_ref[j])
         for j in range(tiles)], axis=1) + d_ref[...] * u
    ys = _gelu_tanh(ys)
    ys = ys * _sigmoid(_dot(ys.astype(BF16), wglu_ref[...]) + bglu_ref[...])
    ys = ys * _silu(gs5_ref[...])
    acc = _dot(ys.astype(BF16), wo_ref[0:BW, :])
    acc = acc + _dot(yrw_ref[...].astype(BF16), wo_ref[BW:2 * BW, :])
    acc = acc + _dot(ym2_ref[...].astype(BF16), wo_ref[2 * BW:3 * BW, :])
    hn = h_ref[...] + acc
    if final:
        ms = jnp.mean(hn * hn, axis=-1, keepdims=True)
        hn = hn * lax.rsqrt(ms + NORM_EPS) * fg_ref[...]
    o_ref[...] = hn


def _s5out(h, proj, yrw, ym2, tb, d_row, w_glu, b_glu, w_out, final_g, final):
    L = h.shape[0]
    TM = OUT_TM
    NS = S5_GROUPS * S5_STATE
    tri = jnp.tril(jnp.ones((S5_CT, S5_CT), BF16))
    tok = pl.BlockSpec((TM, BW), lambda i: (i, 0))
    consts = [tb['wb'], tb['wc_re'], tb['wc_im'], tb['zr'], tb['zi'], tb['sr'], tb['si'], tb['car'],
              d_row, tri, w_glu, b_glu, w_out, final_g]
    const_specs = [pl.BlockSpec(c.shape, (lambda i: (0, 0, 0)) if c.ndim == 3 else (lambda i: (0, 0)))
                   for c in consts]
    return pl.pallas_call(
        functools.partial(_s5out_kernel, final=final),
        grid=(L // TM,),
        in_specs=[
            pl.BlockSpec((TM, D_MODEL), lambda i: (i, 0)),
            pl.BlockSpec((TM, BW), lambda i: (i, COL_U // BW)),
            pl.BlockSpec((TM, BW), lambda i: (i, COL_GS5 // BW)),
            tok,
            tok,
        ] + const_specs,
        out_specs=pl.BlockSpec((TM, D_MODEL), lambda i: (i, 0)),
        out_shape=jax.ShapeDtypeStruct((L, D_MODEL), F32),
        scratch_shapes=[pltpu.VMEM((SUBLANES, NS), F32), pltpu.VMEM((TM, NS), BF16), pltpu.VMEM((TM, NS), BF16)],
        compiler_params=pltpu.CompilerParams(
            dimension_semantics=("arbitrary",), vmem_limit_bytes=VMEM_LIMIT),
    )(h, proj, proj, yrw, ym2, *consts)


def _s5_stages(box, u_ref, wb_ref, wcr_ref, wci_ref, zr_ref, zi_ref, sr_ref, si_ref, car_ref, d_ref, tri_ref,
               st_ref, sre_ref, sim_ref):
    T = S5_CT
    NS = S5_GROUPS * S5_STATE
    tiles = S5_GROUPS // S5_TILE_G
    half = NS // tiles
    u = u_ref[...]
    ub = u.astype(BF16)
    bu = []
    for j in range(tiles):
        bu.append(_b16(_dot(ub[:, j * LANES:(j + 1) * LANES], wb_ref[j])))
        yield
    bu_re = jnp.concatenate([b[:, :half] for b in bu], axis=1)
    bu_im = jnp.concatenate([b[:, half:] for b in bu], axis=1)
    zr, zi, sr, si = zr_ref[...], zi_ref[...], sr_ref[...], si_ref[...]
    car_re, car_im = car_ref[0:1, :], car_ref[1:2, :]
    sr_last, si_last = car_ref[2:3, :], car_ref[3:4, :]
    p_re, p_im = st_ref[0:1, :], st_ref[1:2, :]
    tri = tri_ref[...]
    for c in range(u.shape[0] // T):
        rows = slice(c * T, (c + 1) * T)
        b_re, b_im = bu_re[rows], bu_im[rows]
        z = jnp.concatenate([zr * b_re - zi * b_im, zr * b_im + zi * b_re], axis=1)
        yield
        w = _dot(tri, z)
        yield
        w_re = w[:, :NS] + (car_re * p_re - car_im * p_im)
        w_im = w[:, NS:] + (car_re * p_im + car_im * p_re)
        w_re_b, w_im_b = _b16(w_re), _b16(w_im)
        sre_ref[rows, :] = sr * w_re_b - si * w_im_b
        sim_ref[rows, :] = sr * w_im_b + si * w_re_b
        l_re, l_im = w_re[T - 1:T, :], w_im[T - 1:T, :]
        p_re = sr_last * l_re - si_last * l_im
        p_im = sr_last * l_im + si_last * l_re
        yield
    st_ref[0:1, :] = p_re
    st_ref[1:2, :] = p_im
    ys = []
    for j in range(tiles):
        ys.append(_dot(sre_ref[:, j * half:(j + 1) * half], wcr_ref[j])
                  + _dot(sim_ref[:, j * half:(j + 1) * half], wci_ref[j]))
        yield
    box['ys'] = jnp.concatenate(ys, axis=1) + d_ref[...] * u


N_S5_IN = 17


def _layer_kernel(*refs, has_vmix, final):
    n_rw_in = 22 if has_vmix else 16
    rw_in = refs[:n_rw_in]
    m_in = refs[n_rw_in:n_rw_in + N_MAMBA_IN]
    s5_in = refs[n_rw_in + N_MAMBA_IN:n_rw_in + N_MAMBA_IN + N_S5_IN]
    rest = refs[n_rw_in + N_MAMBA_IN + N_S5_IN:]
    (h_ref, u_ref, gs5_ref, wb_ref, wcr_ref, wci_ref, zr_ref, zi_ref, sr_ref, si_ref, car_ref, d_ref, tri5_ref,
     wglu_ref, bglu_ref, wo_ref, fg_ref) = s5_in
    if has_vmix:
        o_ref, s_ref, st_ref, s5st_ref, sre_ref, sim_ref, yrw_ref, ym2_ref = rest
        rw_refs = (*rw_in, yrw_ref, s_ref)
    else:
        o_ref, vout_ref, s_ref, st_ref, s5st_ref, sre_ref, sim_ref, yrw_ref, ym2_ref = rest
        rw_refs = (*rw_in, yrw_ref, vout_ref, s_ref)
    first = pl.program_id(0) == 0

    @pl.when(first)
    def _():
        s_ref[...] = jnp.zeros_like(s_ref)
        st_ref[...] = jnp.zeros_like(st_ref)
        s5st_ref[...] = jnp.zeros_like(s5st_ref)

    box = {}
    mamba = _mamba_stages(first, *m_in, ym2_ref, st_ref)
    s5 = _s5_stages(box, u_ref, wb_ref, wcr_ref, wci_ref, zr_ref, zi_ref, sr_ref, si_ref, car_ref, d_ref,
                    tri5_ref, s5st_ref, sre_ref, sim_ref)
    n_ticks = (RW_NB // RW_GC) * RW_TICKS_PER_GROUP
    streams = [[mamba, M2_STAGES, 0], [s5, S5_STAGES, 0]]
    calls = [0]

    def tick():
        calls[0] += 1
        for st in streams:
            target = min(st[1], -(-calls[0] * st[1] // n_ticks))
            while st[2] < target:
                next(st[0], None)
                st[2] += 1

    _rwkv_body(first, tick, rw_refs, has_vmix)
    for _ in mamba:
        pass
    for _ in s5:
        pass
    ys = _gelu_tanh(box['ys'])
    ys = ys * _sigmoid(_dot(ys.astype(BF16), wglu_ref[...]) + bglu_ref[...])
    ys = ys * _silu(gs5_ref[...])
    acc = _dot(ys.astype(BF16), wo_ref[0:BW, :])
    acc = acc + _dot(yrw_ref[...].astype(BF16), wo_ref[BW:2 * BW, :])
    acc = acc + _dot(ym2_ref[...].astype(BF16), wo_ref[2 * BW:3 * BW, :])
    hn = h_ref[...] + acc
    if final:
        ms = jnp.mean(hn * hn, axis=-1, keepdims=True)
        hn = hn * lax.rsqrt(ms + NORM_EPS) * fg_ref[...]
    o_ref[...] = hn


def _layer(h, proj, rp, mp, tb, d_row, w_glu, b_glu, w_out, final_g, v_first, final):
    L = proj.shape[0]
    TB = MIX_TB
    NS = S5_GROUPS * S5_STATE
    has_vmix = v_first is not None
    hb = TB // SUBLANES

    def cur(width, col):
        return pl.BlockSpec((TB, width), lambda i: (i, col // width))

    def halo(width, col):
        return pl.BlockSpec((SUBLANES, width), lambda i: (jnp.maximum(i * hb - 1, 0), col // width))

    def full(a):
        return pl.BlockSpec(a.shape, (lambda i: (0, 0, 0)) if a.ndim == 3 else (lambda i: (0, 0)))

    pos = np.arange(TB)
    same = (pos[:, None] // RW_T) == (pos[None, :] // RW_T)
    rw_tri = jnp.asarray(np.concatenate([same & (pos[None, :] <= pos[:, None]),
                                         same & (pos[None, :] > pos[:, None])], axis=0), BF16)
    args = [proj, proj, proj, proj]
    specs = [cur(4 * BW, COL_RW), halo(4 * BW, COL_RW), cur(LANES, COL_WA), halo(LANES, COL_WA)]
    if has_vmix:
        args += [proj, proj, v_first]
        specs += [cur(LANES, COL_MISC), halo(LANES, COL_MISC), pl.BlockSpec((TB, BW), lambda i: (i, 0))]
        names = ['mu_rw', 'mu_wa', 'mu_misc', 'w_up', 'w0', 'a_up', 'a0', 'k_k', 'k_a', 'r_k',
                 'ln_g', 'ln_b', 'v_up', 'v0']
    else:
        names = ['mu_rw', 'mu_wa', 'w_up', 'w0', 'a_up', 'a0', 'k_k', 'k_a', 'r_k', 'ln_g', 'ln_b']
    consts = [rp[n] for n in names] + [rw_tri]
    args += consts
    specs += [full(c) for c in consts]
    assert len(args) == (22 if has_vmix else 16)

    m_tri = jnp.tril(jnp.ones((M2_T, M2_T), BF16))
    m_consts = [mp['conv_w'], mp['conv_b'], mp['dt_bias'], mp['a_log'], mp['d_full'], mp['norm_g'],
                mp['expand'], m_tri]
    args += [proj, proj, proj, proj] + m_consts
    specs += [cur(M2_XBC, COL_XBC), halo(M2_XBC, COL_XBC), cur(BW, COL_Z), cur(LANES, COL_MISC)]
    specs += [full(c) for c in m_consts]

    s5_tri = jnp.tril(jnp.ones((S5_CT, S5_CT), BF16))
    s5_consts = [tb['wb'], tb['wc_re'], tb['wc_im'], tb['zr'], tb['zi'], tb['sr'], tb['si'], tb['car'],
                 d_row, s5_tri, w_glu, b_glu, w_out, final_g]
    args += [h, proj, proj] + s5_consts
    specs += [pl.BlockSpec((TB, D_MODEL), lambda i: (i, 0)), cur(BW, COL_U), cur(BW, COL_GS5)]
    specs += [full(c) for c in s5_consts]
    assert len(s5_consts) + 3 == N_S5_IN

    tok = pl.BlockSpec((TB, BW), lambda i: (i, 0))
    out_specs = [pl.BlockSpec((TB, D_MODEL), lambda i: (i, 0))]
    out_shape = [jax.ShapeDtypeStruct((L, D_MODEL), F32)]
    if not has_vmix:
        out_specs.append(tok)
        out_shape.append(jax.ShapeDtypeStruct((L, BW), F32))
    res = pl.pallas_call(
        functools.partial(_layer_kernel, has_vmix=has_vmix, final=final),
        grid=(L // TB,),
        in_specs=specs,
        out_specs=out_specs,
        out_shape=out_shape,
        scratch_shapes=[pltpu.VMEM((N_PAIRS, LANES, LANES), F32), pltpu.VMEM((N_PAIRS, M2_STATE, LANES), F32),
                        pltpu.VMEM((SUBLANES, NS), F32), pltpu.VMEM((TB, NS), BF16), pltpu.VMEM((TB, NS), BF16),
                        pltpu.VMEM((TB, BW), F32), pltpu.VMEM((TB, BW), F32)],
        compiler_params=pltpu.CompilerParams(
            dimension_semantics=("arbitrary",), vmem_limit_bytes=VMEM_LIMIT),
    )(*args)
    return res[0], (v_first if has_vmix else res[1])


def _proj_stages(h_ref, g_ref, w_ref, dst_ref):
    x = h_ref[...]
    ms = jnp.mean(x * x, axis=-1, keepdims=True)
    hn = (x * lax.rsqrt(ms + NORM_EPS) * g_ref[...]).astype(BF16)
    yield
    for j in range(0, PROJ_PAD, PROJ_TN):
        w = min(PROJ_TN, PROJ_PAD - j)
        dst_ref[:, j:j + w] = _dot(hn, w_ref[:, j:j + w])
        yield


N_PROJ_IN = 4
N_RW_CONST = {True: 15, False: 12}
N_M_CONST = 8
N_S5_CONST = 14


def _fused_kernel(*refs, has_vmix, final):
    it = iter(refs)

    def take(n):
        return [next(it) for _ in range(n)]

    h0_ref, hnext_ref, g_ref, w_ref = take(N_PROJ_IN)
    h_ref, = take(1)
    vfirst_in = take(1) if has_vmix else []
    rw_consts = take(N_RW_CONST[has_vmix])
    m_consts = take(N_M_CONST)
    s5_consts = take(N_S5_CONST)
    o_ref, = take(1)
    vout = take(1) if not has_vmix else []
    s_ref, st_ref, s5st_ref, sre_ref, sim_ref, yrw_ref, ym2_ref, proj_a, proj_b, halo_ref = take(10)
    (wb_ref, wcr_ref, wci_ref, zr_ref, zi_ref, sr_ref, si_ref, car_ref, d_ref, tri5_ref,
     wglu_ref, bglu_ref, wo_ref, fg_ref) = s5_consts
    i = pl.program_id(0)
    first = i == 0

    @pl.when(first)
    def _():
        s_ref[...] = jnp.zeros_like(s_ref)
        st_ref[...] = jnp.zeros_like(st_ref)
        s5st_ref[...] = jnp.zeros_like(s5st_ref)
        halo_ref[...] = jnp.zeros_like(halo_ref)
        for _ in _proj_stages(h0_ref, g_ref, w_ref, proj_a):
            pass

    def cols(ref, col, width):
        return ref.at[:, col:col + width]

    def body(cur, nxt):
        rw_in = [cols(cur, COL_RW, 4 * BW), cols(halo_ref, COL_RW, 4 * BW),
                 cols(cur, COL_WA, LANES), cols(halo_ref, COL_WA, LANES)]
        if has_vmix:
            rw_in += [cols(cur, COL_MISC, LANES), cols(halo_ref, COL_MISC, LANES)] + vfirst_in
        rw_refs = (*rw_in, *rw_consts, yrw_ref, *vout, s_ref)
        m_in = [cols(cur, COL_XBC, M2_XBC), cols(halo_ref, COL_XBC, M2_XBC), cols(cur, COL_Z, BW),
                cols(cur, COL_MISC, LANES)]
        box = {}
        mamba = _mamba_stages(first, *m_in, *m_consts, ym2_ref, st_ref)
        s5 = _s5_stages(box, cols(cur, COL_U, BW), wb_ref, wcr_ref, wci_ref, zr_ref, zi_ref, sr_ref, si_ref,
                        car_ref, d_ref, tri5_ref, s5st_ref, sre_ref, sim_ref)
        proj = _proj_stages(hnext_ref, g_ref, w_ref, nxt)
        calls = [0]

        def tick():
            calls[0] += 1
            next(mamba if calls[0] % 2 == 0 else s5, None)
            if calls[0] % 4 == 1:
                next(proj, None)

        _rwkv_body(first, tick, rw_refs, has_vmix)
        for gen in (mamba, s5, proj):
            for _ in gen:
                pass
        ys = _gelu_tanh(box['ys'])
        ys = ys * _sigmoid(_dot(ys.astype(BF16), wglu_ref[...]) + bglu_ref[...])
        ys = ys * _silu(cur[:, COL_GS5:COL_GS5 + BW])
        acc = _dot(ys.astype(BF16), wo_ref[0:BW, :])
        acc = acc + _dot(yrw_ref[...].astype(BF16), wo_ref[BW:2 * BW, :])
        acc = acc + _dot(ym2_ref[...].astype(BF16), wo_ref[2 * BW:3 * BW, :])
        hn = h_ref[...] + acc
        if final:
            ms = jnp.mean(hn * hn, axis=-1, keepdims=True)
            hn = hn * lax.rsqrt(ms + NORM_EPS) * fg_ref[...]
        o_ref[...] = hn
        halo_ref[...] = cur[MIX_TB - SUBLANES:MIX_TB, :]

    pl.when(i % 2 == 0)(functools.partial(body, proj_a, proj_b))
    pl.when(i % 2 == 1)(functools.partial(body, proj_b, proj_a))


def _fused_layer(h, g_row, w_cat, rp, mp, tb, d_row, w_glu, b_glu, w_out, final_g, v_first, final):
    L = h.shape[0]
    TB = MIX_TB
    nb = L // TB
    NS = S5_GROUPS * S5_STATE
    has_vmix = v_first is not None

    def const(a):
        return pl.BlockSpec(a.shape, (lambda i: (0, 0, 0)) if a.ndim == 3 else (lambda i: (0, 0)),
                            pipeline_mode=pl.Buffered(1))

    hblk = lambda fn: pl.BlockSpec((TB, D_MODEL), fn)
    tok = pl.BlockSpec((TB, BW), lambda i: (i, 0))
    args = [h, h, g_row, w_cat, h]
    specs = [hblk(lambda i: (0, 0)), hblk(lambda i: (jnp.minimum(i + 1, nb - 1), 0)), const(g_row), const(w_cat),
             hblk(lambda i: (i, 0))]
    assert len(args) == N_PROJ_IN + 1
    if has_vmix:
        args.append(v_first)
        specs.append(tok)
        names = ['mu_rw', 'mu_wa', 'mu_misc', 'w_up', 'w0', 'a_up', 'a0', 'k_k', 'k_a', 'r_k',
                 'ln_g', 'ln_b', 'v_up', 'v0']
    else:
        names = ['mu_rw', 'mu_wa', 'w_up', 'w0', 'a_up', 'a0', 'k_k', 'k_a', 'r_k', 'ln_g', 'ln_b']
    pos = np.arange(TB)
    same = (pos[:, None] // RW_T) == (pos[None, :] // RW_T)
    rw_tri = jnp.asarray(np.concatenate([same & (pos[None, :] <= pos[:, None]),
                                         same & (pos[None, :] > pos[:, None])], axis=0), BF16)
    rw_consts = [rp[n] for n in names] + [rw_tri]
    assert len(rw_consts) == N_RW_CONST[has_vmix]
    m_consts = [mp['conv_w'], mp['conv_b'], mp['dt_bias'], mp['a_log'], mp['d_full'], mp['norm_g'],
                mp['expand'], jnp.tril(jnp.ones((M2_T, M2_T), BF16))]
    assert len(m_consts) == N_M_CONST
    s5_consts = [tb['wb'], tb['wc_re'], tb['wc_im'], tb['zr'], tb['zi'], tb['sr'], tb['si'], tb['car'],
                 d_row, jnp.tril(jnp.ones((S5_CT, S5_CT), BF16)), w_glu, b_glu, w_out, final_g]
    assert len(s5_consts) == N_S5_CONST
    for c in rw_consts + m_consts + s5_consts:
        args.append(c)
        specs.append(const(c))

    out_specs = [pl.BlockSpec((TB, D_MODEL), lambda i: (i, 0))]
    out_shape = [jax.ShapeDtypeStruct((L, D_MODEL), F32)]
    if not has_vmix:
        out_specs.append(tok)
        out_shape.append(jax.ShapeDtypeStruct((L, BW), F32))
    res = pl.pallas_call(
        functools.partial(_fused_kernel, has_vmix=has_vmix, final=final),
        grid=(nb,),
        in_specs=specs,
        out_specs=out_specs,
        out_shape=out_shape,
        scratch_shapes=[pltpu.VMEM((N_PAIRS, LANES, LANES), F32), pltpu.VMEM((N_PAIRS, M2_STATE, LANES), F32),
                        pltpu.VMEM((SUBLANES, NS), F32), pltpu.VMEM((TB, NS), BF16), pltpu.VMEM((TB, NS), BF16),
                        pltpu.VMEM((TB, BW), F32), pltpu.VMEM((TB, BW), F32),
                        pltpu.VMEM((TB, PROJ_PAD), F32), pltpu.VMEM((TB, PROJ_PAD), F32),
                        pltpu.VMEM((SUBLANES, PROJ_PAD), F32)],
        compiler_params=pltpu.CompilerParams(
            dimension_semantics=("arbitrary",), vmem_limit_bytes=FUSED_VMEM_LIMIT),
    )(*args)
    return res[0], (v_first if has_vmix else res[1])


def _reorder_w_in(w, w_vmix):
    o = np.cumsum([0, BW, BW, BW, BW, BW, LORA_W, LORA_A, BW, M2_XBC, N_HEADS, BW])
    u, gs5, r, k, v, wd, ad, g, xbc, dt, z = (w[:, o[i]:o[i + 1]] for i in range(11))
    vm = w_vmix if w_vmix is not None else jnp.zeros((w.shape[0], LORA_V), w.dtype)
    pad = jnp.zeros((w.shape[0], LANES - N_HEADS - LORA_V), w.dtype)
    return jnp.concatenate([r, k, v, g, u, gs5, xbc, z, wd, ad, dt, vm, pad], axis=1).astype(BF16)


def _pad_rows(w, start, total=LANES):
    return jnp.pad(w.astype(F32), ((start, total - start - w.shape[0]), (0, 0)))


def _row(a):
    return a.astype(F32).reshape(1, -1)


def kernel(x, norm_g, w_in, w_in_vmix, s5_log_dt, s5_a_re, s5_a_im, s5_b_re, s5_b_im, s5_c_re, s5_c_im, s5_d, s5_w_glu, s5_b_glu, rwkv_mu, rwkv_w_up, rwkv_w0, rwkv_a_up, rwkv_a0, rwkv_k_k, rwkv_k_a, rwkv_r_k, rwkv_ln_g, rwkv_ln_b, rwkv_vmix_mu, rwkv_v_up, rwkv_v0, m2_conv_w, m2_conv_b, m2_dt_bias, m2_a_log, m2_d, m2_norm_g, w_out, final_norm_g):
    bsz, L, d = x.shape
    assert bsz == 1 and d == D_MODEL
    assert L % max(PROJ_TM, OUT_TM, MIX_TB) == 0 and MIX_TB % M2_T == 0 and OUT_TM % S5_CT == 0
    depth = w_in.shape[0]
    h = x.reshape(L, d).astype(F32)
    expand = jnp.pad(jnp.repeat(jnp.eye(N_HEADS, dtype=F32), HEAD, axis=1), ((0, LANES - N_HEADS), (0, 0)))
    v_first = None
    for i in range(depth):
        w_cat = _reorder_w_in(w_in[i], w_in_vmix[i - 1] if i > 0 else None)

        mu = rwkv_mu[i].astype(F32)
        mo = np.cumsum([0, BW, BW, BW, LORA_W, LORA_A, BW])
        mr, mk, mv, mwd, mad, mg = (mu[mo[j]:mo[j + 1]] for j in range(6))
        rp = {
            'mu_rw': jnp.concatenate([mr, mk, mv, mg]).reshape(1, -1),
            'mu_wa': jnp.concatenate([mwd, mad]).reshape(1, -1),
            'w_up': _pad_rows(rwkv_w_up[i], 0).astype(BF16), 'w0': _row(rwkv_w0[i]),
            'a_up': _pad_rows(rwkv_a_up[i], LORA_W).astype(BF16), 'a0': _row(rwkv_a0[i]),
            'k_k': _row(rwkv_k_k[i]), 'k_a': _row(rwkv_k_a[i]), 'r_k': _row(rwkv_r_k[i]),
            'ln_g': _row(rwkv_ln_g[i]), 'ln_b': _row(rwkv_ln_b[i]),
        }
        if i > 0:
            rp['mu_misc'] = jnp.pad(rwkv_vmix_mu[i - 1].astype(F32),
                                    (N_HEADS, LANES - N_HEADS - LORA_V)).reshape(1, -1)
            rp['v_up'] = _pad_rows(rwkv_v_up[i - 1], N_HEADS).astype(BF16)
            rp['v0'] = _row(rwkv_v0[i - 1])
        mp = {
            'conv_w': m2_conv_w[i].astype(F32), 'conv_b': _row(m2_conv_b[i]),
            'dt_bias': jnp.pad(m2_dt_bias[i].astype(F32), (0, LANES - N_HEADS)).reshape(1, -1),
            'a_log': jnp.pad(m2_a_log[i].astype(F32), (0, LANES - N_HEADS)).reshape(1, -1),
            'd_full': jnp.repeat(m2_d[i].astype(F32), HEAD).reshape(1, -1),
            'norm_g': _row(m2_norm_g[i]), 'expand': expand.astype(BF16),
        }
        s5_tb = _s5_state_tables(s5_log_dt[i], s5_a_re[i], s5_a_im[i], s5_b_re[i], s5_b_im[i],
                                 s5_c_re[i], s5_c_im[i])
        proj = _proj(h, _row(norm_g[i]), w_cat)
        h, v_first = _layer(h, proj, rp, mp, s5_tb, _row(s5_d[i]), s5_w_glu[i].astype(BF16),
                            _row(s5_b_glu[i]), w_out[i].astype(BF16), _row(final_norm_g), v_first,
                            final=(i == depth - 1))
    return h.reshape(bsz, L, d).astype(x.dtype)
```

```python
import functools
import math

import jax
import jax.numpy as jnp
import numpy as np
from jax import lax
from jax.experimental import pallas as pl
from jax.experimental.pallas import tpu as pltpu

F32 = jnp.float32
BF16 = jnp.bfloat16

D_MODEL = 1024
BW = 512
S5_GROUP = 16
S5_GROUPS = 32
S5_STATE = 64
HEAD = 64
N_HEADS = 8
N_PAIRS = N_HEADS // 2
LORA_W = 64
LORA_A = 64
LORA_V = 32
GN_EPS = 64e-5
M2_GROUPS = 2
M2_STATE = 128
M2_CONV = 4
M2_XBC = 1024
NORM_EPS = 1e-5

LANES = 128
SUBLANES = 8

COL_RW = 0
COL_U = 2048
COL_GS5 = 2560
COL_XBC = 3072
COL_Z = 4096
COL_WA = 4608
COL_MISC = 4736
PROJ_PAD = 4864

PROJ_TM = 256
PROJ_TN = 512
S5_CT = 64
S5_TILE_G = LANES // S5_GROUP
RW_T = 64
RW_NB = 4
LAYER_TB = RW_T * RW_NB
M2_T = 128
RW_TICKS = 21
M2_STAGES = 6 + 7 * (LAYER_TB // M2_T)
S5_STAGES = 8 + 3 * (LAYER_TB // S5_CT)
VMEM_LIMIT = 48 * 1024 * 1024


def _dot(a, b):
    return jnp.dot(a, b, preferred_element_type=F32)


def _dot_nt(a, b):
    return lax.dot_general(a, b, (((1,), (1,)), ((), ())), preferred_element_type=F32)


def _dot_tn(a, b):
    return lax.dot_general(a, b, (((0,), (0,)), ((), ())), preferred_element_type=F32)


def _split3(x):
    hi = x.astype(BF16)
    r1 = x - hi.astype(F32)
    mid = r1.astype(BF16)
    lo = (r1 - mid.astype(F32)).astype(BF16)
    return hi, mid, lo


def _dot_split_rhs(m, x):
    hi, mid, lo = _split3(x)
    return _dot(m, hi) + _dot(m, mid) + _dot(m, lo)


def _dot_split_lhs(x, m):
    hi, mid, lo = _split3(x)
    return _dot(hi, m) + _dot(mid, m) + _dot(lo, m)


def _b16(x):
    return x.astype(BF16)


def _sigmoid(x):
    return 1.0 / (1.0 + jnp.exp(-x))


def _silu(x):
    return x * _sigmoid(x)


def _softplus(x):
    return jnp.maximum(x, 0.0) + jnp.log1p(jnp.exp(-jnp.abs(x)))


def _gelu_tanh(x):
    c = math.sqrt(2.0 / math.pi)
    return 0.5 * x * (1.0 + jnp.tanh(c * (x + 0.044715 * (x * x * x))))


def _prev_rows(x, halo, is_first):
    rolled = pltpu.roll(x, 1, axis=0)
    last = jnp.where(is_first, 0.0, halo[SUBLANES - 1:SUBLANES, :])
    row = lax.broadcasted_iota(jnp.int32, (SUBLANES, x.shape[1]), 0)
    head = jnp.where(row == 0, last, rolled[:SUBLANES])
    return jnp.concatenate([head, rolled[SUBLANES:]], axis=0)


def _proj_kernel(h_ref, g_ref, w_ref, o_ref):
    x = h_ref[...]
    ms = jnp.mean(x * x, axis=-1, keepdims=True)
    hn = (x * lax.rsqrt(ms + NORM_EPS) * g_ref[...]).astype(BF16)
    for j in range(0, PROJ_PAD, PROJ_TN):
        w = min(PROJ_TN, PROJ_PAD - j)
        o_ref[:, j:j + w] = _dot(hn, w_ref[:, j:j + w])


def _proj(h, g, w):
    L = h.shape[0]
    return pl.pallas_call(
        _proj_kernel,
        grid=(L // PROJ_TM,),
        in_specs=[
            pl.BlockSpec((PROJ_TM, D_MODEL), lambda i: (i, 0)),
            pl.BlockSpec((1, D_MODEL), lambda i: (0, 0)),
            pl.BlockSpec((D_MODEL, PROJ_PAD), lambda i: (0, 0)),
        ],
        out_specs=pl.BlockSpec((PROJ_TM, PROJ_PAD), lambda i: (i, 0)),
        out_shape=jax.ShapeDtypeStruct((L, PROJ_PAD), F32),
        compiler_params=pltpu.CompilerParams(
            dimension_semantics=("parallel",), vmem_limit_bytes=VMEM_LIMIT),
    )(h, g, w)


def _mamba_stages(first, xbc_ref, xbcp_ref, z_ref, misc_ref, cw_ref, cb_ref, dtb_ref, alog_ref, dfull_ref,
                  ng_ref, expand_ref, tri_ref, y_ref, st_ref):
    T = M2_T
    x = xbc_ref[...]
    halo = jnp.where(first, 0.0, xbcp_ref[...])
    row8 = lax.broadcasted_iota(jnp.int32, halo.shape, 0)
    acc = x * cw_ref[M2_CONV - 1:M2_CONV, :] + cb_ref[...]
    yield
    for j in range(1, M2_CONV):
        xr = pltpu.roll(x, j, axis=0)
        head = jnp.where(row8 < j, pltpu.roll(halo, j, axis=0), xr[:SUBLANES])
        xj = jnp.concatenate([head, xr[SUBLANES:]], axis=0)
        acc = acc + xj * cw_ref[M2_CONV - 1 - j:M2_CONV - j, :]
        yield
    xbc = _silu(acc)
    xm = xbc[:, :BW]
    bm_b = _b16(xbc[:, BW:BW + M2_GROUPS * M2_STATE])
    cm_b = _b16(xbc[:, BW + M2_GROUPS * M2_STATE:])
    yield
    dt = _softplus(misc_ref[...] + dtb_ref[...])
    da = dt * (-jnp.exp(alog_ref[...]))
    expand = expand_ref[...]
    xdt = xm * _dot_split_lhs(dt, expand)
    xdt_b = _b16(xdt)
    gate = _silu(z_ref[...])
    yield

    rr = lax.broadcasted_iota(jnp.int32, (T, T), 0)
    cc = lax.broadcasted_iota(jnp.int32, (T, T), 1)
    causal = cc <= rr
    head0 = lax.broadcasted_iota(jnp.int32, (T, LANES), 1) < HEAD
    states = [st_ref[q] for q in range(N_PAIRS)]
    for ch in range(x.shape[0] // T):
        rows = slice(ch * T, (ch + 1) * T)
        cs = _dot_split_rhs(tri_ref[...], da[rows])
        cs_t = cs.T
        cs_full = _dot_split_lhs(cs, expand)
        cs_last = cs_full[T - 1:T, :]
        ecs = jnp.exp(cs_full)
        xdte_b = _b16(xdt[rows] * jnp.exp(cs_last - cs_full))
        chunk_decay = jnp.exp(cs_last)
        yield
        scores = [_dot_nt(cm_b[rows, gi * M2_STATE:(gi + 1) * M2_STATE], bm_b[rows, gi * M2_STATE:(gi + 1) * M2_STATE])
                  for gi in range(M2_GROUPS)]
        yield
        ys = []
        for q in range(N_PAIRS):
            gi = (2 * q) // (N_HEADS // M2_GROUPS)
            gs = slice(gi * M2_STATE, (gi + 1) * M2_STATE)
            sl = slice(q * LANES, (q + 1) * LANES)
            yd = []
            for h in (2 * q, 2 * q + 1):
                seg = cs[:, h:h + 1] - cs_t[h:h + 1, :]
                dec = jnp.where(causal, jnp.exp(jnp.minimum(seg, 0.0)), 0.0)
                yd.append(_dot(_b16(scores[gi] * dec), xdt_b[rows, sl]))
            st = states[q]
            y_off = _dot(cm_b[rows, gs], _b16(st)) * ecs[:, sl]
            states[q] = st * chunk_decay[:, sl] + _dot_tn(bm_b[rows, gs], xdte_b[:, sl])
            ys.append(jnp.where(head0, yd[0], yd[1]) + y_off)
            yield
        y = jnp.concatenate(ys, axis=1) + xm[rows] * dfull_ref[...]
        y = y * gate[rows]
        ms = jnp.mean(y * y, axis=-1, keepdims=True)
        y_ref[rows, :] = y * lax.rsqrt(ms + NORM_EPS) * ng_ref[...]
        yield
    for q in range(N_PAIRS):
        st_ref[q] = states[q]


def _rwkv_body(first, tick, refs, has_vmix):
    if has_vmix:
        (rw_ref, rwp_ref, wa_ref, wap_ref, misc_ref, miscp_ref, vfirst_ref,
         mu_rw_ref, mu_wa_ref, mu_misc_ref, wup_ref, w0_ref, aup_ref, a0_ref, kk_ref, ka_ref, rk_ref,
         lng_ref, lnb_ref, vup_ref, v0_ref, tri_ref, y_ref, s_ref) = refs
    else:
        (rw_ref, rwp_ref, wa_ref, wap_ref,
         mu_rw_ref, mu_wa_ref, wup_ref, w0_ref, aup_ref, a0_ref, kk_ref, ka_ref, rk_ref,
         lng_ref, lnb_ref, tri_ref, y_ref, vout_ref, s_ref) = refs
    T = RW_T
    TB = LAYER_TB
    b16 = _b16

    def lerp(ref, halo_ref, mu_ref):
        x = ref[...]
        return x + (_prev_rows(x, halo_ref[...], first) - x) * mu_ref[...]

    xs = lerp(rw_ref, rwp_ref, mu_rw_ref)
    r, k, v, g = (xs[:, i * BW:(i + 1) * BW] for i in range(4))
    wa = lerp(wa_ref, wap_ref, mu_wa_ref)
    w_log = -_softplus(-(w0_ref[...] + _dot(b16(jnp.tanh(wa)), wup_ref[...]))) - 0.5
    lw = -jnp.exp(w_log)
    a = _sigmoid(a0_ref[...] + _dot(b16(wa), aup_ref[...]))
    if has_vmix:
        vd = lerp(misc_ref, miscp_ref, mu_misc_ref)
        mix = _sigmoid(v0_ref[...] + _dot(b16(vd), vup_ref[...]))
        v = v + (vfirst_ref[...] - v) * mix
    else:
        vout_ref[...] = v
    kk = k * kk_ref[...]
    kmod = k * (1.0 + (a - 1.0) * ka_ref[...])
    cums = _dot_split_rhs(tri_ref[...], lw)
    cl, rv = cums[:TB], cums[TB:]
    g_incl = jnp.exp(cl)
    g_inv = jnp.exp(-cl)
    g_end = jnp.exp(rv)
    rkr = r * kmod * rk_ref[...]

    def per_head_sum(x):
        h0 = lax.broadcasted_iota(jnp.int32, x.shape, 1) < HEAD
        f0 = h0.astype(F32)
        s0 = jnp.sum(jnp.where(h0, x, 0.0), axis=-1, keepdims=True)
        s1 = jnp.sum(jnp.where(h0, 0.0, x), axis=-1, keepdims=True)
        return s0 * f0 + s1 * (1.0 - f0)

    inv, bonus_w = [], []
    for q in range(N_PAIRS):
        sl = slice(q * LANES, (q + 1) * LANES)
        sq = kk[:, sl] * kk[:, sl]
        inv.append(1.0 / jnp.maximum(jnp.sqrt(per_head_sum(sq)), 1e-12))
        bonus_w.append(per_head_sum(rkr[:, sl]))
    kkn = kk * jnp.concatenate(inv, axis=1)
    bonus = jnp.concatenate(bonus_w, axis=1) * v
    kkna = kkn * a
    na_b = b16(-kkn * jnp.exp(cl - lw))
    nb_b = b16(kkna * g_inv)
    nbe_b = b16(kkna * g_end)
    pk_b = b16(kmod * g_inv)
    pke_b = b16(kmod * g_end)
    pr = r * g_incl
    pr_b = b16(pr)
    v_b = b16(v)
    gated = _silu(g)

    m0 = (lax.broadcasted_iota(jnp.int32, (1, LANES), 1) < HEAD).astype(BF16)
    m1 = 1.0 - m0
    row_w = lax.broadcasted_iota(jnp.int32, (T, LANES), 0)
    col_w = lax.broadcasted_iota(jnp.int32, (T, LANES), 1) % HEAD
    strict = col_w < row_w
    incl = col_w <= row_w
    eye_w = (col_w == row_w).astype(F32)
    rr = lax.broadcasted_iota(jnp.int32, (LANES, LANES), 0)
    cc = lax.broadcasted_iota(jnp.int32, (LANES, LANES), 1)
    same_head = (rr < HEAD) == (cc < HEAD)

    def stack(x):
        return jnp.concatenate([x * m0, x * m1], axis=0)

    def rows(c):
        return slice(c * T, (c + 1) * T)

    def lanes(q):
        return slice(q * LANES, (q + 1) * LANES)

    state = [s_ref[q] for q in range(N_PAIRS)]
    units = [(c, q) for c in range(RW_NB) for q in range(N_PAIRS)]
    a_n = [na_b[rows(c), lanes(q)] for c, q in units]
    r_n = [pr_b[rows(c), lanes(q)] for c, q in units]
    v_n = [v_b[rows(c), lanes(q)] for c, q in units]
    be_n = [nbe_b[rows(c), lanes(q)] for c, q in units]
    ke_n = [pke_b[rows(c), lanes(q)] for c, q in units]
    b_st = [stack(nb_b[rows(c), lanes(q)]) for c, q in units]
    k_st = [stack(pk_b[rows(c), lanes(q)]) for c, q in units]
    a_st = [stack(x) for x in a_n]
    v_st = [stack(x) for x in v_n]
    tick()
    aa = [_dot_nt(jnp.concatenate([a_, r_], axis=0), jnp.concatenate([b_, k_], axis=0))
          for a_, r_, b_, k_ in zip(a_n, r_n, b_st, k_st)]
    tick()
    a_ak = [b16(jnp.where(strict, x[:T, LANES:], 0.0)) for x in aa]
    a_rb = [b16(jnp.where(incl, x[T:, :LANES], 0.0)) for x in aa]
    a_rk = [b16(jnp.where(incl, x[T:, LANES:], 0.0)) for x in aa]
    p = [jnp.where(strict, x[:T, :LANES], 0.0) for x in aa]
    minv = [eye_w + x for x in p]
    tick()
    for _ in range(int(math.log2(T)) - 1):
        p_b = [b16(x) for x in p]
        p = [_dot(x, stack(x)) for x in p_b]
        tick()
        minv = [m + _dot(b16(x), stack(b16(m))) for x, m in zip(p, minv)]
        tick()
    akv = [_dot(x, y) for x, y in zip(a_ak, v_st)]
    tick()
    wu_b = [b16(_dot(b16(m), jnp.concatenate([a_, stack(b16(x))], axis=1)))
            for m, a_, x in zip(minv, a_st, akv)]
    tick()
    p_mat = [b16(jnp.where(same_head, _dot_tn(w[:, :LANES], be), 0.0)) for w, be in zip(wu_b, be_n)]
    tick()
    z_mat = [jnp.where(same_head,
                       _dot_tn(jnp.concatenate([w[:, LANES:], vn], axis=0), jnp.concatenate([be, ke], axis=0)),
                       0.0)
             for w, vn, be, ke in zip(wu_b, v_n, be_n, ke_n)]
    tick()
    q_mat = [b16(pr[rows(c), lanes(q)] + _dot(arb, stack(w[:, :LANES])))
             for (c, q), arb, w in zip(units, a_rb, wu_b)]
    tick()
    y0 = [_dot(jnp.concatenate([arb, ark], axis=1), jnp.concatenate([stack(w[:, LANES:]), vs], axis=0))
          for arb, ark, w, vs in zip(a_rb, a_rk, wu_b, v_st)]
    tick()
    ys = []
    for i, (c, q) in enumerate(units):
        s = state[q]
        s_b = b16(s)
        ys.append(_dot_nt(q_mat[i], s_b) + y0[i])
        state[q] = s * g_incl[(c + 1) * T - 1:(c + 1) * T, lanes(q)] + _dot(s_b, p_mat[i]) + z_mat[i]
    tick()
    for i, (c, q) in enumerate(units):
        y = ys[i]
        mean = per_head_sum(y) * (1.0 / HEAD)
        cen = y - mean
        var = per_head_sum(cen * cen) * (1.0 / HEAD)
        yn = cen * lax.rsqrt(var + GN_EPS)
        y_ref[rows(c), lanes(q)] = ((yn * lng_ref[:, lanes(q)] + lnb_ref[:, lanes(q)]
                                     + bonus[rows(c), lanes(q)]) * gated[rows(c), lanes(q)])
    tick()
    for q in range(N_PAIRS):
        s_ref[q] = state[q]


def _s5_state_tables(log_dt, a_re, a_im, b_re, b_im, c_re, c_im):
    T = S5_CT
    G, P = a_re.shape
    dt = jnp.exp(log_dt.astype(F32))[:, None]
    ar, ai = a_re.astype(F32), a_im.astype(F32)
    mag = jnp.exp(dt * ar)
    abar_re, abar_im = mag * jnp.cos(dt * ai), mag * jnp.sin(dt * ai)
    den = ar * ar + ai * ai
    f_re = ((abar_re - 1.0) * ar + abar_im * ai) / den
    f_im = (abar_im * ar - (abar_re - 1.0) * ai) / den
    br, bi = b_re.astype(F32), b_im.astype(F32)
    bb_re = f_re[..., None] * br - f_im[..., None] * bi
    bb_im = f_re[..., None] * bi + f_im[..., None] * br

    def powers(e):
        e = jnp.asarray(e, F32)[:, None, None]
        m = jnp.exp(e * (dt * ar)[None])
        ang = e * (dt * ai)[None]
        return (m * jnp.cos(ang)).reshape(-1, G * P), (m * jnp.sin(ang)).reshape(-1, G * P)

    pos = np.arange(T, dtype=np.float32)
    half = T / 2
    zr, zi = powers(half - pos)
    sr, si = powers(pos - half)
    cr, ci = powers(np.array([half + 1.0], np.float32))
    tiles = G // S5_TILE_G
    eye = jnp.eye(S5_TILE_G, dtype=F32)

    def b_side(bb):
        t = bb.reshape(tiles, S5_TILE_G, P, S5_GROUP).transpose(0, 1, 3, 2)
        return (t[:, :, :, None, :] * eye[None, :, None, :, None]).reshape(tiles, LANES, S5_TILE_G * P)

    def c_side(cc):
        t = cc.astype(F32).reshape(tiles, S5_TILE_G, S5_GROUP, P).transpose(0, 1, 3, 2)
        return (t[:, :, :, None, :] * eye[None, :, None, :, None]).reshape(tiles, S5_TILE_G * P, LANES)

    wb = jnp.concatenate([b_side(bb_re), b_side(bb_im)], axis=-1).astype(BF16)
    car = jnp.concatenate([cr, ci, sr[T - 1:T], si[T - 1:T]], axis=0)
    return dict(wb=wb, wc_re=c_side(c_re).astype(BF16), wc_im=c_side(-c_im.astype(F32)).astype(BF16),
                zr=zr.astype(BF16), zi=zi.astype(BF16), sr=sr.astype(BF16), si=si.astype(BF16), car=car)


def _s5_stages(box, u_ref, wb_ref, wcr_ref, wci_ref, zr_ref, zi_ref, sr_ref, si_ref, car_ref, d_ref, tri_ref,
               st_ref, sre_ref, sim_ref):
    T = S5_CT
    NS = S5_GROUPS * S5_STATE
    tiles = S5_GROUPS // S5_TILE_G
    half = NS // tiles
    u = u_ref[...]
    ub = u.astype(BF16)
    bu = []
    for j in range(tiles):
        bu.append(_b16(_dot(ub[:, j * LANES:(j + 1) * LANES], wb_ref[j])))
        yield
    bu_re = jnp.concatenate([b[:, :half] for b in bu], axis=1)
    bu_im = jnp.concatenate([b[:, half:] for b in bu], axis=1)
    zr, zi, sr, si = zr_ref[...], zi_ref[...], sr_ref[...], si_ref[...]
    car_re, car_im = car_ref[0:1, :], car_ref[1:2, :]
    sr_last, si_last = car_ref[2:3, :], car_ref[3:4, :]
    p_re, p_im = st_ref[0:1, :], st_ref[1:2, :]
    tri = tri_ref[...]
    for c in range(u.shape[0] // T):
        rows = slice(c * T, (c + 1) * T)
        b_re, b_im = bu_re[rows], bu_im[rows]
        z = jnp.concatenate([zr * b_re - zi * b_im, zr * b_im + zi * b_re], axis=1)
        yield
        w = _dot(tri, z)
        yield
        w_re = w[:, :NS] + (car_re * p_re - car_im * p_im)
        w_im = w[:, NS:] + (car_re * p_im + car_im * p_re)
        w_re_b, w_im_b = _b16(w_re), _b16(w_im)
        sre_ref[rows, :] = sr * w_re_b - si * w_im_b
        sim_ref[rows, :] = sr * w_im_b + si * w_re_b
        l_re, l_im = w_re[T - 1:T, :], w_im[T - 1:T, :]
        p_re = sr_last * l_re - si_last * l_im
        p_im = sr_last * l_im + si_last * l_re
        yield
    st_ref[0:1, :] = p_re
    st_ref[1:2, :] = p_im
    ys = []
    for j in range(tiles):
        ys.append(_dot(sre_ref[:, j * half:(j + 1) * half], wcr_ref[j])
                  + _dot(sim_ref[:, j * half:(j + 1) * half], wci_ref[j]))
        yield
    box['ys'] = jnp.concatenate(ys, axis=1) + d_ref[...] * u


N_RW_IN = {True: 22, False: 16}
N_MAMBA_IN = 12
N_S5_IN = 17


def _layer_kernel(*refs, has_vmix, final):
    n_rw_in = N_RW_IN[has_vmix]
    rw_in = refs[:n_rw_in]
    m_in = refs[n_rw_in:n_rw_in + N_MAMBA_IN]
    s5_in = refs[n_rw_in + N_MAMBA_IN:n_rw_in + N_MAMBA_IN + N_S5_IN]
    rest = refs[n_rw_in + N_MAMBA_IN + N_S5_IN:]
    (h_ref, u_ref, gs5_ref, wb_ref, wcr_ref, wci_ref, zr_ref, zi_ref, sr_ref, si_ref, car_ref, d_ref, tri5_ref,
     wglu_ref, bglu_ref, wo_ref, fg_ref) = s5_in
    if has_vmix:
        o_ref, s_ref, st_ref, s5st_ref, sre_ref, sim_ref, yrw_ref, ym2_ref = rest
        rw_refs = (*rw_in, yrw_ref, s_ref)
    else:
        o_ref, vout_ref, s_ref, st_ref, s5st_ref, sre_ref, sim_ref, yrw_ref, ym2_ref = rest
        rw_refs = (*rw_in, yrw_ref, vout_ref, s_ref)
    first = pl.program_id(0) == 0

    @pl.when(first)
    def _():
        s_ref[...] = jnp.zeros_like(s_ref)
        st_ref[...] = jnp.zeros_like(st_ref)
        s5st_ref[...] = jnp.zeros_like(s5st_ref)

    box = {}
    mamba = _mamba_stages(first, *m_in, ym2_ref, st_ref)
    s5 = _s5_stages(box, u_ref, wb_ref, wcr_ref, wci_ref, zr_ref, zi_ref, sr_ref, si_ref, car_ref, d_ref,
                    tri5_ref, s5st_ref, sre_ref, sim_ref)
    streams = [[mamba, M2_STAGES, 0], [s5, S5_STAGES, 0]]
    calls = [0]

    def tick():
        calls[0] += 1
        for st in streams:
            target = min(st[1], -(-calls[0] * st[1] // RW_TICKS))
            while st[2] < target:
                next(st[0], None)
                st[2] += 1

    _rwkv_body(first, tick, rw_refs, has_vmix)
    for _ in mamba:
        pass
    for _ in s5:
        pass
    ys = _gelu_tanh(box['ys'])
    ys = ys * _sigmoid(_dot(ys.astype(BF16), wglu_ref[...]) + bglu_ref[...])
    ys = ys * _silu(gs5_ref[...])
    acc = _dot(ys.astype(BF16), wo_ref[0:BW, :])
    acc = acc + _dot(yrw_ref[...].astype(BF16), wo_ref[BW:2 * BW, :])
    acc = acc + _dot(ym2_ref[...].astype(BF16), wo_ref[2 * BW:3 * BW, :])
    hn = h_ref[...] + acc
    if final:
        ms = jnp.mean(hn * hn, axis=-1, keepdims=True)
        hn = hn * lax.rsqrt(ms + NORM_EPS) * fg_ref[...]
    o_ref[...] = hn


def _layer(h, proj, rp, mp, tb, d_row, w_glu, b_glu, w_out, final_g, v_first, final):
    L = proj.shape[0]
    TB = LAYER_TB
    NS = S5_GROUPS * S5_STATE
    has_vmix = v_first is not None
    hb = TB // SUBLANES

    def cur(width, col):
        return pl.BlockSpec((TB, width), lambda i: (i, col // width))

    def halo(width, col):
        return pl.BlockSpec((SUBLANES, width), lambda i: (jnp.maximum(i * hb - 1, 0), col // width))

    def full(a):
        return pl.BlockSpec(a.shape, (lambda i: (0, 0, 0)) if a.ndim == 3 else (lambda i: (0, 0)),
                            pipeline_mode=pl.Buffered(1))

    pos = np.arange(TB)
    same = (pos[:, None] // RW_T) == (pos[None, :] // RW_T)
    rw_tri = jnp.asarray(np.concatenate([same & (pos[None, :] <= pos[:, None]),
                                         same & (pos[None, :] > pos[:, None])], axis=0), BF16)
    args = [proj, proj, proj, proj]
    specs = [cur(4 * BW, COL_RW), halo(4 * BW, COL_RW), cur(LANES, COL_WA), halo(LANES, COL_WA)]
    if has_vmix:
        args += [proj, proj, v_first]
        specs += [cur(LANES, COL_MISC), halo(LANES, COL_MISC), pl.BlockSpec((TB, BW), lambda i: (i, 0))]
        names = ['mu_rw', 'mu_wa', 'mu_misc', 'w_up', 'w0', 'a_up', 'a0', 'k_k', 'k_a', 'r_k',
                 'ln_g', 'ln_b', 'v_up', 'v0']
    else:
        names = ['mu_rw', 'mu_wa', 'w_up', 'w0', 'a_up', 'a0', 'k_k', 'k_a', 'r_k', 'ln_g', 'ln_b']
    consts = [rp[n] for n in names] + [rw_tri]
    args += consts
    specs += [full(c) for c in consts]
    assert len(args) == N_RW_IN[has_vmix]

    m_tri = jnp.tril(jnp.ones((M2_T, M2_T), BF16))
    m_consts = [mp['conv_w'], mp['conv_b'], mp['dt_bias'], mp['a_log'], mp['d_full'], mp['norm_g'],
                mp['expand'], m_tri]
    args += [proj, proj, proj, proj] + m_consts
    specs += [cur(M2_XBC, COL_XBC), halo(M2_XBC, COL_XBC), cur(BW, COL_Z), cur(LANES, COL_MISC)]
    specs += [full(c) for c in m_consts]
    assert len(m_consts) + 4 == N_MAMBA_IN

    s5_tri = jnp.tril(jnp.ones((S5_CT, S5_CT), BF16))
    s5_consts = [tb['wb'], tb['wc_re'], tb['wc_im'], tb['zr'], tb['zi'], tb['sr'], tb['si'], tb['car'],
                 d_row, s5_tri, w_glu, b_glu, w_out, final_g]
    args += [h, proj, proj] + s5_consts
    specs += [pl.BlockSpec((TB, D_MODEL), lambda i: (i, 0)), cur(BW, COL_U), cur(BW, COL_GS5)]
    specs += [full(c) for c in s5_consts]
    assert len(s5_consts) + 3 == N_S5_IN

    tok = pl.BlockSpec((TB, BW), lambda i: (i, 0))
    out_specs = [pl.BlockSpec((TB, D_MODEL), lambda i: (i, 0))]
    out_shape = [jax.ShapeDtypeStruct((L, D_MODEL), F32)]
    if not has_vmix:
        out_specs.append(tok)
        out_shape.append(jax.ShapeDtypeStruct((L, BW), F32))
    res = pl.pallas_call(
        functools.partial(_layer_kernel, has_vmix=has_vmix, final=final),
        grid=(L // TB,),
        in_specs=specs,
        out_specs=out_specs,
        out_shape=out_shape,
        scratch_shapes=[pltpu.VMEM((N_PAIRS, LANES, LANES), F32), pltpu.VMEM((N_PAIRS, M2_STATE, LANES), F32),
                        pltpu.VMEM((SUBLANES, NS), F32), pltpu.VMEM((TB, NS), BF16), pltpu.VMEM((TB, NS), BF16),
                        pltpu.VMEM((TB, BW), F32), pltpu.VMEM((TB, BW), F32)],
        compiler_params=pltpu.CompilerParams(
            dimension_semantics=("arbitrary",), vmem_limit_bytes=VMEM_LIMIT),
    )(*args)
    return res[0], (v_first if has_vmix else res[1])


def _reorder_w_in(w, w_vmix):
    o = np.cumsum([0, BW, BW, BW, BW, BW, LORA_W, LORA_A, BW, M2_XBC, N_HEADS, BW])
    u, gs5, r, k, v, wd, ad, g, xbc, dt, z = (w[:, o[i]:o[i + 1]] for i in range(11))
    vm = w_vmix if w_vmix is not None else jnp.zeros((w.shape[0], LORA_V), w.dtype)
    pad = jnp.zeros((w.shape[0], LANES - N_HEADS - LORA_V), w.dtype)
    return jnp.concatenate([r, k, v, g, u, gs5, xbc, z, wd, ad, dt, vm, pad], axis=1).astype(BF16)


def _pad_rows(w, start, total=LANES):
    return jnp.pad(w.astype(F32), ((start, total - start - w.shape[0]), (0, 0)))


def _row(a):
    return a.astype(F32).reshape(1, -1)


def kernel(x, norm_g, w_in, w_in_vmix, s5_log_dt, s5_a_re, s5_a_im, s5_b_re, s5_b_im, s5_c_re, s5_c_im, s5_d, s5_w_glu, s5_b_glu, rwkv_mu, rwkv_w_up, rwkv_w0, rwkv_a_up, rwkv_a0, rwkv_k_k, rwkv_k_a, rwkv_r_k, rwkv_ln_g, rwkv_ln_b, rwkv_vmix_mu, rwkv_v_up, rwkv_v0, m2_conv_w, m2_conv_b, m2_dt_bias, m2_a_log, m2_d, m2_norm_g, w_out, final_norm_g):
    bsz, L, d = x.shape
    assert bsz == 1 and d == D_MODEL
    assert L % max(PROJ_TM, LAYER_TB) == 0 and LAYER_TB % M2_T == 0 and LAYER_TB % S5_CT == 0
    depth = w_in.shape[0]
    h = x.reshape(L, d).astype(F32)
    expand = jnp.pad(jnp.repeat(jnp.eye(N_HEADS, dtype=F32), HEAD, axis=1), ((0, LANES - N_HEADS), (0, 0)))
    v_first = None
    for i in range(depth):
        w_cat = _reorder_w_in(w_in[i], w_in_vmix[i - 1] if i > 0 else None)
        proj = _proj(h, _row(norm_g[i]), w_cat)

        mu = rwkv_mu[i].astype(F32)
        mo = np.cumsum([0, BW, BW, BW, LORA_W, LORA_A, BW])
        mr, mk, mv, mwd, mad, mg = (mu[mo[j]:mo[j + 1]] for j in range(6))
        rp = {
            'mu_rw': jnp.concatenate([mr, mk, mv, mg]).reshape(1, -1),
            'mu_wa': jnp.concatenate([mwd, mad]).reshape(1, -1),
            'w_up': _pad_rows(rwkv_w_up[i], 0).astype(BF16), 'w0': _row(rwkv_w0[i]),
            'a_up': _pad_rows(rwkv_a_up[i], LORA_W).astype(BF16), 'a0': _row(rwkv_a0[i]),
            'k_k': _row(rwkv_k_k[i]), 'k_a': _row(rwkv_k_a[i]), 'r_k': _row(rwkv_r_k[i]),
            'ln_g': _row(rwkv_ln_g[i]), 'ln_b': _row(rwkv_ln_b[i]),
        }
        if i > 0:
            rp['mu_misc'] = jnp.pad(rwkv_vmix_mu[i - 1].astype(F32),
                                    (N_HEADS, LANES - N_HEADS - LORA_V)).reshape(1, -1)
            rp['v_up'] = _pad_rows(rwkv_v_up[i - 1], N_HEADS).astype(BF16)
            rp['v0'] = _row(rwkv_v0[i - 1])
        mp = {
            'conv_w': m2_conv_w[i].astype(F32), 'conv_b': _row(m2_conv_b[i]),
            'dt_bias': jnp.pad(m2_dt_bias[i].astype(F32), (0, LANES - N_HEADS)).reshape(1, -1),
            'a_log': jnp.pad(m2_a_log[i].astype(F32), (0, LANES - N_HEADS)).reshape(1, -1),
            'd_full': jnp.repeat(m2_d[i].astype(F32), HEAD).reshape(1, -1),
            'norm_g': _row(m2_norm_g[i]), 'expand': expand.astype(BF16),
        }
        s5_tb = _s5_state_tables(s5_log_dt[i], s5_a_re[i], s5_a_im[i], s5_b_re[i], s5_b_im[i],
                                 s5_c_re[i], s5_c_im[i])
        h, v_first = _layer(h, proj, rp, mp, s5_tb, _row(s5_d[i]), s5_w_glu[i].astype(BF16),
                            _row(s5_b_glu[i]), w_out[i].astype(BF16), _row(final_norm_g), v_first,
                            final=(i == depth - 1))
    return h.reshape(bsz, L, d).astype(x.dtype)
```

```python
import functools
import math

import jax
import jax.numpy as jnp
import numpy as np
from jax import lax
from jax.experimental import pallas as pl
from jax.experimental.pallas import tpu as pltpu

F32 = jnp.float32
BF16 = jnp.bfloat16

D_MODEL = 1024
BW = 512
S5_GROUP = 16
S5_GROUPS = 32
S5_STATE = 64
HEAD = 64
N_HEADS = 8
N_PAIRS = N_HEADS // 2
LORA_W = 64
LORA_A = 64
LORA_V = 32
GN_EPS = 64e-5
M2_GROUPS = 2
M2_STATE = 128
M2_CONV = 4
M2_XBC = 1024
NORM_EPS = 1e-5

LANES = 128
SUBLANES = 8

COL_RW = 0
COL_U = 2048
COL_GS5 = 2560
COL_XBC = 3072
COL_Z = 4096
COL_WA = 4608
COL_MISC = 4736
PROJ_PAD = 4864

PROJ_TM = 512
PROJ_TN = 512
S5_CT = 64
S5_TILE_G = LANES // S5_GROUP
RW_T = 64
RW_NB = 4
LAYER_TB = RW_T * RW_NB
M2_T = 128
RW_TICKS = 21
M2_STAGES = 6 + 7 * (LAYER_TB // M2_T)
S5_STAGES = 8 + 3 * (LAYER_TB // S5_CT)
VMEM_LIMIT = 48 * 1024 * 1024


def _dot(a, b):
    return jnp.dot(a, b, preferred_element_type=F32)


def _dot_nt(a, b):
    return lax.dot_general(a, b, (((1,), (1,)), ((), ())), preferred_element_type=F32)


def _dot_tn(a, b):
    return lax.dot_general(a, b, (((0,), (0,)), ((), ())), preferred_element_type=F32)


def _split3(x):
    hi = x.astype(BF16)
    r1 = x - hi.astype(F32)
    mid = r1.astype(BF16)
    lo = (r1 - mid.astype(F32)).astype(BF16)
    return hi, mid, lo


def _dot_split_rhs(m, x):
    hi, mid, lo = _split3(x)
    return _dot(m, hi) + _dot(m, mid) + _dot(m, lo)


def _dot_split_lhs(x, m):
    hi, mid, lo = _split3(x)
    return _dot(hi, m) + _dot(mid, m) + _dot(lo, m)


def _b16(x):
    return x.astype(BF16)


def _sigmoid(x):
    return 1.0 / (1.0 + jnp.exp(-x))


def _silu(x):
    return x * _sigmoid(x)


def _softplus(x):
    return jnp.maximum(x, 0.0) + jnp.log(1.0 + jnp.exp(-jnp.abs(x)))


def _gelu_tanh(x):
    c = math.sqrt(2.0 / math.pi)
    return 0.5 * x * (1.0 + jnp.tanh(c * (x + 0.044715 * (x * x * x))))


def _prev_rows(x, halo, is_first):
    rolled = pltpu.roll(x, 1, axis=0)
    last = jnp.where(is_first, 0.0, halo[SUBLANES - 1:SUBLANES, :])
    row = lax.broadcasted_iota(jnp.int32, (SUBLANES, x.shape[1]), 0)
    head = jnp.where(row == 0, last, rolled[:SUBLANES])
    return jnp.concatenate([head, rolled[SUBLANES:]], axis=0)


def _proj_kernel(h_ref, g_ref, w_ref, o_ref):
    x = h_ref[...]
    ms = jnp.mean(x * x, axis=-1, keepdims=True)
    hn = (x * lax.rsqrt(ms + NORM_EPS) * g_ref[...]).astype(BF16)
    for j in range(0, PROJ_PAD, PROJ_TN):
        w = min(PROJ_TN, PROJ_PAD - j)
        o_ref[:, j:j + w] = _dot(hn, w_ref[:, j:j + w])


def _proj(h, g, w):
    L = h.shape[0]
    return pl.pallas_call(
        _proj_kernel,
        grid=(L // PROJ_TM,),
        in_specs=[
            pl.BlockSpec((PROJ_TM, D_MODEL), lambda i: (i, 0)),
            pl.BlockSpec((1, D_MODEL), lambda i: (0, 0)),
            pl.BlockSpec((D_MODEL, PROJ_PAD), lambda i: (0, 0), pipeline_mode=pl.Buffered(1)),
        ],
        out_specs=pl.BlockSpec((PROJ_TM, PROJ_PAD), lambda i: (i, 0)),
        out_shape=jax.ShapeDtypeStruct((L, PROJ_PAD), F32),
        compiler_params=pltpu.CompilerParams(
            dimension_semantics=("parallel",), vmem_limit_bytes=VMEM_LIMIT),
    )(h, g, w)


def _mamba_stages(first, xbc_ref, xbcp_ref, z_ref, misc_ref, cw_ref, cb_ref, dtb_ref, alog_ref, dfull_ref,
                  ng_ref, expand_ref, tri_ref, y_ref, st_ref):
    T = M2_T
    x = xbc_ref[...]
    halo = jnp.where(first, 0.0, xbcp_ref[...])
    row8 = lax.broadcasted_iota(jnp.int32, halo.shape, 0)
    acc = x * cw_ref[M2_CONV - 1:M2_CONV, :] + cb_ref[...]
    yield
    for j in range(1, M2_CONV):
        xr = pltpu.roll(x, j, axis=0)
        head = jnp.where(row8 < j, pltpu.roll(halo, j, axis=0), xr[:SUBLANES])
        xj = jnp.concatenate([head, xr[SUBLANES:]], axis=0)
        acc = acc + xj * cw_ref[M2_CONV - 1 - j:M2_CONV - j, :]
        yield
    xbc = _silu(acc)
    xm = xbc[:, :BW]
    bm_b = _b16(xbc[:, BW:BW + M2_GROUPS * M2_STATE])
    cm_b = _b16(xbc[:, BW + M2_GROUPS * M2_STATE:])
    yield
    dt = _softplus(misc_ref[...] + dtb_ref[...])
    da = dt * (-jnp.exp(alog_ref[...]))
    expand = expand_ref[...]
    xdt = xm * _dot_split_lhs(dt, expand)
    xdt_b = _b16(xdt)
    gate = _silu(z_ref[...])
    yield

    rr = lax.broadcasted_iota(jnp.int32, (T, T), 0)
    cc = lax.broadcasted_iota(jnp.int32, (T, T), 1)
    causal = cc <= rr
    head0 = lax.broadcasted_iota(jnp.int32, (T, LANES), 1) < HEAD
    states = [st_ref[q] for q in range(N_PAIRS)]
    for ch in range(x.shape[0] // T):
        rows = slice(ch * T, (ch + 1) * T)
        cs = _dot_split_rhs(tri_ref[...], da[rows])
        cs_t = cs.T
        cs_full = _dot_split_lhs(cs, expand)
        cs_last = cs_full[T - 1:T, :]
        ecs = jnp.exp(cs_full)
        xdte_b = _b16(xdt[rows] * jnp.exp(cs_last - cs_full))
        chunk_decay = jnp.exp(cs_last)
        yield
        scores = [_dot_nt(cm_b[rows, gi * M2_STATE:(gi + 1) * M2_STATE], bm_b[rows, gi * M2_STATE:(gi + 1) * M2_STATE])
                  for gi in range(M2_GROUPS)]
        yield
        ys = []
        for q in range(N_PAIRS):
            gi = (2 * q) // (N_HEADS // M2_GROUPS)
            gs = slice(gi * M2_STATE, (gi + 1) * M2_STATE)
            sl = slice(q * LANES, (q + 1) * LANES)
            yd = []
            for h in (2 * q, 2 * q + 1):
                seg = cs[:, h:h + 1] - cs_t[h:h + 1, :]
                dec = jnp.where(causal, jnp.exp(jnp.minimum(seg, 0.0)), 0.0)
                yd.append(_dot(_b16(scores[gi] * dec), xdt_b[rows, sl]))
            st = states[q]
            y_off = _dot(cm_b[rows, gs], _b16(st)) * ecs[:, sl]
            states[q] = st * chunk_decay[:, sl] + _dot_tn(bm_b[rows, gs], xdte_b[:, sl])
            ys.append(jnp.where(head0, yd[0], yd[1]) + y_off)
            yield
        y = jnp.concatenate(ys, axis=1) + xm[rows] * dfull_ref[...]
        y = y * gate[rows]
        ms = jnp.mean(y * y, axis=-1, keepdims=True)
        y_ref[rows, :] = y * lax.rsqrt(ms + NORM_EPS) * ng_ref[...]
        yield
    for q in range(N_PAIRS):
        st_ref[q] = states[q]


def _rwkv_body(first, tick, refs, has_vmix):
    if has_vmix:
        (rw_ref, rwp_ref, wa_ref, wap_ref, misc_ref, miscp_ref, vfirst_ref,
         mu_rw_ref, mu_wa_ref, mu_misc_ref, wup_ref, w0_ref, aup_ref, a0_ref, kk_ref, ka_ref, rk_ref,
         lng_ref, lnb_ref, vup_ref, v0_ref, tri_ref, y_ref, s_ref) = refs
    else:
        (rw_ref, rwp_ref, wa_ref, wap_ref,
         mu_rw_ref, mu_wa_ref, wup_ref, w0_ref, aup_ref, a0_ref, kk_ref, ka_ref, rk_ref,
         lng_ref, lnb_ref, tri_ref, y_ref, vout_ref, s_ref) = refs
    T = RW_T
    TB = LAYER_TB
    b16 = _b16

    def lerp(ref, halo_ref, mu_ref):
        x = ref[...]
        return x + (_prev_rows(x, halo_ref[...], first) - x) * mu_ref[...]

    xs = lerp(rw_ref, rwp_ref, mu_rw_ref)
    r, k, v, g = (xs[:, i * BW:(i + 1) * BW] for i in range(4))
    wa = lerp(wa_ref, wap_ref, mu_wa_ref)
    w_log = -_softplus(-(w0_ref[...] + _dot(b16(jnp.tanh(wa)), wup_ref[...]))) - 0.5
    lw = -jnp.exp(w_log)
    a = _sigmoid(a0_ref[...] + _dot(b16(wa), aup_ref[...]))
    if has_vmix:
        vd = lerp(misc_ref, miscp_ref, mu_misc_ref)
        mix = _sigmoid(v0_ref[...] + _dot(b16(vd), vup_ref[...]))
        v = v + (vfirst_ref[...] - v) * mix
    else:
        vout_ref[...] = v
    kk = k * kk_ref[...]
    kmod = k * (1.0 + (a - 1.0) * ka_ref[...])
    cums = _dot_split_rhs(tri_ref[...], lw)
    cl, rv = cums[:TB], cums[TB:]
    g_incl = jnp.exp(cl)
    g_inv = jnp.exp(-cl)
    g_end = jnp.exp(rv)
    rkr = r * kmod * rk_ref[...]

    def per_head_sum(x):
        h0 = lax.broadcasted_iota(jnp.int32, x.shape, 1) < HEAD
        f0 = h0.astype(F32)
        s0 = jnp.sum(jnp.where(h0, x, 0.0), axis=-1, keepdims=True)
        s1 = jnp.sum(jnp.where(h0, 0.0, x), axis=-1, keepdims=True)
        return s0 * f0 + s1 * (1.0 - f0)

    inv, bonus_w = [], []
    for q in range(N_PAIRS):
        sl = slice(q * LANES, (q + 1) * LANES)
        sq = kk[:, sl] * kk[:, sl]
        inv.append(1.0 / jnp.maximum(jnp.sqrt(per_head_sum(sq)), 1e-12))
        bonus_w.append(per_head_sum(rkr[:, sl]))
    kkn = kk * jnp.concatenate(inv, axis=1)
    bonus = jnp.concatenate(bonus_w, axis=1) * v
    kkna = kkn * a
    na_b = b16(-kkn * jnp.exp(cl - lw))
    nb_b = b16(kkna * g_inv)
    nbe_b = b16(kkna * g_end)
    pk_b = b16(kmod * g_inv)
    pke_b = b16(kmod * g_end)
    pr = r * g_incl
    pr_b = b16(pr)
    v_b = b16(v)
    gated = _silu(g)

    m0 = (lax.broadcasted_iota(jnp.int32, (1, LANES), 1) < HEAD).astype(BF16)
    m1 = 1.0 - m0
    row_w = lax.broadcasted_iota(jnp.int32, (T, LANES), 0)
    col_w = lax.broadcasted_iota(jnp.int32, (T, LANES), 1) % HEAD
    strict = col_w < row_w
    incl = col_w <= row_w
    eye_w = (col_w == row_w).astype(F32)
    rr = lax.broadcasted_iota(jnp.int32, (LANES, LANES), 0)
    cc = lax.broadcasted_iota(jnp.int32, (LANES, LANES), 1)
    same_head = (rr < HEAD) == (cc < HEAD)

    def stack(x):
        return jnp.concatenate([x * m0, x * m1], axis=0)

    def rows(c):
        return slice(c * T, (c + 1) * T)

    def lanes(q):
        return slice(q * LANES, (q + 1) * LANES)

    state = [s_ref[q] for q in range(N_PAIRS)]
    units = [(c, q) for c in range(RW_NB) for q in range(N_PAIRS)]
    a_n = [na_b[rows(c), lanes(q)] for c, q in units]
    r_n = [pr_b[rows(c), lanes(q)] for c, q in units]
    v_n = [v_b[rows(c), lanes(q)] for c, q in units]
    be_n = [nbe_b[rows(c), lanes(q)] for c, q in units]
    ke_n = [pke_b[rows(c), lanes(q)] for c, q in units]
    b_st = [stack(nb_b[rows(c), lanes(q)]) for c, q in units]
    k_st = [stack(pk_b[rows(c), lanes(q)]) for c, q in units]
    a_st = [stack(x) for x in a_n]
    v_st = [stack(x) for x in v_n]
    tick()
    aa = [_dot_nt(jnp.concatenate([a_, r_], axis=0), jnp.concatenate([b_, k_], axis=0))
          for a_, r_, b_, k_ in zip(a_n, r_n, b_st, k_st)]
    tick()
    a_ak = [b16(jnp.where(strict, x[:T, LANES:], 0.0)) for x in aa]
    a_rb = [b16(jnp.where(incl, x[T:, :LANES], 0.0)) for x in aa]
    a_rk = [b16(jnp.where(incl, x[T:, LANES:], 0.0)) for x in aa]
    p = [jnp.where(strict, x[:T, :LANES], 0.0) for x in aa]
    minv = [eye_w + x for x in p]
    tick()
    for _ in range(int(math.log2(T)) - 1):
        p_b = [b16(x) for x in p]
        p = [_dot(x, stack(x)) for x in p_b]
        tick()
        minv = [m + _dot(b16(x), stack(b16(m))) for x, m in zip(p, minv)]
        tick()
    akv = [_dot(x, y) for x, y in zip(a_ak, v_st)]
    tick()
    wu_b = [b16(_dot(b16(m), jnp.concatenate([a_, stack(b16(x))], axis=1)))
            for m, a_, x in zip(minv, a_st, akv)]
    tick()
    p_mat = [b16(jnp.where(same_head, _dot_tn(w[:, :LANES], be), 0.0)) for w, be in zip(wu_b, be_n)]
    tick()
    z_mat = [jnp.where(same_head,
                       _dot_tn(jnp.concatenate([w[:, LANES:], vn], axis=0), jnp.concatenate([be, ke], axis=0)),
                       0.0)
             for w, vn, be, ke in zip(wu_b, v_n, be_n, ke_n)]
    tick()
    q_mat = [b16(pr[rows(c), lanes(q)] + _dot(arb, stack(w[:, :LANES])))
             for (c, q), arb, w in zip(units, a_rb, wu_b)]
    tick()
    y0 = [_dot(jnp.concatenate([arb, ark], axis=1), jnp.concatenate([stack(w[:, LANES:]), vs], axis=0))
          for arb, ark, w, vs in zip(a_rb, a_rk, wu_b, v_st)]
    tick()
    ys = []
    for i, (c, q) in enumerate(units):
        s = state[q]
        s_b = b16(s)
        ys.append(_dot_nt(q_mat[i], s_b) + y0[i])
        state[q] = s * g_incl[(c + 1) * T - 1:(c + 1) * T, lanes(q)] + _dot(s_b, p_mat[i]) + z_mat[i]
    tick()
    for i, (c, q) in enumerate(units):
        y = ys[i]
        mean = per_head_sum(y) * (1.0 / HEAD)
        cen = y - mean
        var = per_head_sum(cen * cen) * (1.0 / HEAD)
        yn = cen * lax.rsqrt(var + GN_EPS)
        y_ref[rows(c), lanes(q)] = ((yn * lng_ref[:, lanes(q)] + lnb_ref[:, lanes(q)]
                                     + bonus[rows(c), lanes(q)]) * gated[rows(c), lanes(q)])
    tick()
    for q in range(N_PAIRS):
        s_ref[q] = state[q]


def _s5_state_tables(log_dt, a_re, a_im, b_re, b_im, c_re, c_im):
    T = S5_CT
    G, P = a_re.shape
    dt = jnp.exp(log_dt.astype(F32))[:, None]
    ar, ai = a_re.astype(F32), a_im.astype(F32)
    mag = jnp.exp(dt * ar)
    abar_re, abar_im = mag * jnp.cos(dt * ai), mag * jnp.sin(dt * ai)
    den = ar * ar + ai * ai
    f_re = ((abar_re - 1.0) * ar + abar_im * ai) / den
    f_im = (abar_im * ar - (abar_re - 1.0) * ai) / den
    br, bi = b_re.astype(F32), b_im.astype(F32)
    bb_re = f_re[..., None] * br - f_im[..., None] * bi
    bb_im = f_re[..., None] * bi + f_im[..., None] * br

    def powers(e):
        e = jnp.asarray(e, F32)[:, None, None]
        m = jnp.exp(e * (dt * ar)[None])
        ang = e * (dt * ai)[None]
        return (m * jnp.cos(ang)).reshape(-1, G * P), (m * jnp.sin(ang)).reshape(-1, G * P)

    pos = np.arange(T, dtype=np.float32)
    half = T / 2
    zr, zi = powers(half - pos)
    sr, si = powers(pos - half)
    cr, ci = powers(np.array([half + 1.0], np.float32))
    tiles = G // S5_TILE_G
    eye = jnp.eye(S5_TILE_G, dtype=F32)

    def b_side(bb):
        t = bb.reshape(tiles, S5_TILE_G, P, S5_GROUP).transpose(0, 1, 3, 2)
        return (t[:, :, :, None, :] * eye[None, :, None, :, None]).reshape(tiles, LANES, S5_TILE_G * P)

    def c_side(cc):
        t = cc.astype(F32).reshape(tiles, S5_TILE_G, S5_GROUP, P).transpose(0, 1, 3, 2)
        return (t[:, :, :, None, :] * eye[None, :, None, :, None]).reshape(tiles, S5_TILE_G * P, LANES)

    wb = jnp.concatenate([b_side(bb_re), b_side(bb_im)], axis=-1).astype(BF16)
    car = jnp.concatenate([cr, ci, sr[T - 1:T], si[T - 1:T]], axis=0)
    return dict(wb=wb, wc_re=c_side(c_re).astype(BF16), wc_im=c_side(-c_im.astype(F32)).astype(BF16),
                zr=zr.astype(BF16), zi=zi.astype(BF16), sr=sr.astype(BF16), si=si.astype(BF16), car=car)


def _s5_stages(box, u_ref, wb_ref, wcr_ref, wci_ref, zr_ref, zi_ref, sr_ref, si_ref, car_ref, d_ref, tri_ref,
               st_ref, sre_ref, sim_ref):
    T = S5_CT
    NS = S5_GROUPS * S5_STATE
    tiles = S5_GROUPS // S5_TILE_G
    half = NS // tiles
    u = u_ref[...]
    ub = u.astype(BF16)
    bu = []
    for j in range(tiles):
        bu.append(_b16(_dot(ub[:, j * LANES:(j + 1) * LANES], wb_ref[j])))
        yield
    bu_re = jnp.concatenate([b[:, :half] for b in bu], axis=1)
    bu_im = jnp.concatenate([b[:, half:] for b in bu], axis=1)
    zr, zi, sr, si = zr_ref[...], zi_ref[...], sr_ref[...], si_ref[...]
    car_re, car_im = car_ref[0:1, :], car_ref[1:2, :]
    sr_last, si_last = car_ref[2:3, :], car_ref[3:4, :]
    p_re, p_im = st_ref[0:1, :], st_ref[1:2, :]
    tri = tri_ref[...]
    for c in range(u.shape[0] // T):
        rows = slice(c * T, (c + 1) * T)
        b_re, b_im = bu_re[rows], bu_im[rows]
        z = jnp.concatenate([zr * b_re - zi * b_im, zr * b_im + zi * b_re], axis=1)
        yield
        w = _dot(tri, z)
        yield
        w_re = w[:, :NS] + (car_re * p_re - car_im * p_im)
        w_im = w[:, NS:] + (car_re * p_im + car_im * p_re)
        w_re_b, w_im_b = _b16(w_re), _b16(w_im)
        sre_ref[rows, :] = sr * w_re_b - si * w_im_b
        sim_ref[rows, :] = sr * w_im_b + si * w_re_b
        l_re, l_im = w_re[T - 1:T, :], w_im[T - 1:T, :]
        p_re = sr_last * l_re - si_last * l_im
        p_im = sr_last * l_im + si_last * l_re
        yield
    st_ref[0:1, :] = p_re
    st_ref[1:2, :] = p_im
    ys = []
    for j in range(tiles):
        ys.append(_dot(sre_ref[:, j * half:(j + 1) * half], wcr_ref[j])
                  + _dot(sim_ref[:, j * half:(j + 1) * half], wci_ref[j]))
        yield
    box['ys'] = jnp.concatenate(ys, axis=1) + d_ref[...] * u


N_RW_IN = {True: 22, False: 16}
N_MAMBA_IN = 12
N_S5_IN = 17


def _layer_kernel(*refs, has_vmix, final):
    n_rw_in = N_RW_IN[has_vmix]
    rw_in = refs[:n_rw_in]
    m_in = refs[n_rw_in:n_rw_in + N_MAMBA_IN]
    s5_in = refs[n_rw_in + N_MAMBA_IN:n_rw_in + N_MAMBA_IN + N_S5_IN]
    rest = refs[n_rw_in + N_MAMBA_IN + N_S5_IN:]
    (h_ref, u_ref, gs5_ref, wb_ref, wcr_ref, wci_ref, zr_ref, zi_ref, sr_ref, si_ref, car_ref, d_ref, tri5_ref,
     wglu_ref, bglu_ref, wo_ref, fg_ref) = s5_in
    if has_vmix:
        o_ref, s_ref, st_ref, s5st_ref, sre_ref, sim_ref, yrw_ref, ym2_ref = rest
        rw_refs = (*rw_in, yrw_ref, s_ref)
    else:
        o_ref, vout_ref, s_ref, st_ref, s5st_ref, sre_ref, sim_ref, yrw_ref, ym2_ref = rest
        rw_refs = (*rw_in, yrw_ref, vout_ref, s_ref)
    first = pl.program_id(0) == 0

    @pl.when(first)
    def _():
        s_ref[...] = jnp.zeros_like(s_ref)
        st_ref[...] = jnp.zeros_like(st_ref)
        s5st_ref[...] = jnp.zeros_like(s5st_ref)

    box = {}
    mamba = _mamba_stages(first, *m_in, ym2_ref, st_ref)
    s5 = _s5_stages(box, u_ref, wb_ref, wcr_ref, wci_ref, zr_ref, zi_ref, sr_ref, si_ref, car_ref, d_ref,
                    tri5_ref, s5st_ref, sre_ref, sim_ref)
    streams = [[mamba, M2_STAGES, 0], [s5, S5_STAGES, 0]]
    calls = [0]

    def tick():
        calls[0] += 1
        for st in streams:
            target = min(st[1], -(-calls[0] * st[1] // RW_TICKS))
            while st[2] < target:
                next(st[0], None)
                st[2] += 1

    _rwkv_body(first, tick, rw_refs, has_vmix)
    for _ in mamba:
        pass
    for _ in s5:
        pass
    ys = _gelu_tanh(box['ys'])
    ys = ys * _sigmoid(_dot(ys.astype(BF16), wglu_ref[...]) + bglu_ref[...])
    ys = ys * _silu(gs5_ref[...])
    acc = _dot(ys.astype(BF16), wo_ref[0:BW, :])
    acc = acc + _dot(yrw_ref[...].astype(BF16), wo_ref[BW:2 * BW, :])
    acc = acc + _dot(ym2_ref[...].astype(BF16), wo_ref[2 * BW:3 * BW, :])
    hn = h_ref[...] + acc
    if final:
        ms = jnp.mean(hn * hn, axis=-1, keepdims=True)
        hn = hn * lax.rsqrt(ms + NORM_EPS) * fg_ref[...]
    o_ref[...] = hn


def _layer(h, proj, rp, mp, tb, d_row, w_glu, b_glu, w_out, final_g, v_first, final):
    L = proj.shape[0]
    TB = LAYER_TB
    NS = S5_GROUPS * S5_STATE
    has_vmix = v_first is not None
    hb = TB // SUBLANES

    def cur(width, col):
        return pl.BlockSpec((TB, width), lambda i: (i, col // width))

    def halo(width, col):
        return pl.BlockSpec((SUBLANES, width), lambda i: (jnp.maximum(i * hb - 1, 0), col // width))

    def full(a):
        return pl.BlockSpec(a.shape, (lambda i: (0, 0, 0)) if a.ndim == 3 else (lambda i: (0, 0)),
                            pipeline_mode=pl.Buffered(1))

    pos = np.arange(TB)
    same = (pos[:, None] // RW_T) == (pos[None, :] // RW_T)
    rw_tri = jnp.asarray(np.concatenate([same & (pos[None, :] <= pos[:, None]),
                                         same & (pos[None, :] > pos[:, None])], axis=0), BF16)
    args = [proj, proj, proj, proj]
    specs = [cur(4 * BW, COL_RW), halo(4 * BW, COL_RW), cur(LANES, COL_WA), halo(LANES, COL_WA)]
    if has_vmix:
        args += [proj, proj, v_first]
        specs += [cur(LANES, COL_MISC), halo(LANES, COL_MISC), pl.BlockSpec((TB, BW), lambda i: (i, 0))]
        names = ['mu_rw', 'mu_wa', 'mu_misc', 'w_up', 'w0', 'a_up', 'a0', 'k_k', 'k_a', 'r_k',
                 'ln_g', 'ln_b', 'v_up', 'v0']
    else:
        names = ['mu_rw', 'mu_wa', 'w_up', 'w0', 'a_up', 'a0', 'k_k', 'k_a', 'r_k', 'ln_g', 'ln_b']
    consts = [rp[n] for n in names] + [rw_tri]
    args += consts
    specs += [full(c) for c in consts]
    assert len(args) == N_RW_IN[has_vmix]

    m_tri = jnp.tril(jnp.ones((M2_T, M2_T), BF16))
    m_consts = [mp['conv_w'], mp['conv_b'], mp['dt_bias'], mp['a_log'], mp['d_full'], mp['norm_g'],
                mp['expand'], m_tri]
    args += [proj, proj, proj, proj] + m_consts
    specs += [cur(M2_XBC, COL_XBC), halo(M2_XBC, COL_XBC), cur(BW, COL_Z), cur(LANES, COL_MISC)]
    specs += [full(c) for c in m_consts]
    assert len(m_consts) + 4 == N_MAMBA_IN

    s5_tri = jnp.tril(jnp.ones((S5_CT, S5_CT), BF16))
    s5_consts = [tb['wb'], tb['wc_re'], tb['wc_im'], tb['zr'], tb['zi'], tb['sr'], tb['si'], tb['car'],
                 d_row, s5_tri, w_glu, b_glu, w_out, final_g]
    args += [h, proj, proj] + s5_consts
    specs += [pl.BlockSpec((TB, D_MODEL), lambda i: (i, 0)), cur(BW, COL_U), cur(BW, COL_GS5)]
    specs += [full(c) for c in s5_consts]
    assert len(s5_consts) + 3 == N_S5_IN

    tok = pl.BlockSpec((TB, BW), lambda i: (i, 0))
    out_specs = [pl.BlockSpec((TB, D_MODEL), lambda i: (i, 0))]
    out_shape = [jax.ShapeDtypeStruct((L, D_MODEL), F32)]
    if not has_vmix:
        out_specs.append(tok)
        out_shape.append(jax.ShapeDtypeStruct((L, BW), F32))
    res = pl.pallas_call(
        functools.partial(_layer_kernel, has_vmix=has_vmix, final=final),
        grid=(L // TB,),
        in_specs=specs,
        out_specs=out_specs,
        out_shape=out_shape,
        scratch_shapes=[pltpu.VMEM((N_PAIRS, LANES, LANES), F32), pltpu.VMEM((N_PAIRS, M2_STATE, LANES), F32),
                        pltpu.VMEM((SUBLANES, NS), F32), pltpu.VMEM((TB, NS), BF16), pltpu.VMEM((TB, NS), BF16),
                        pltpu.VMEM((TB, BW), F32), pltpu.VMEM((TB, BW), F32)],
        compiler_params=pltpu.CompilerParams(
            dimension_semantics=("arbitrary",), vmem_limit_bytes=VMEM_LIMIT),
    )(*args)
    return res[0], (v_first if has_vmix else res[1])


def _reorder_w_in(w, w_vmix):
    o = np.cumsum([0, BW, BW, BW, BW, BW, LORA_W, LORA_A, BW, M2_XBC, N_HEADS, BW])
    u, gs5, r, k, v, wd, ad, g, xbc, dt, z = (w[:, o[i]:o[i + 1]] for i in range(11))
    vm = w_vmix if w_vmix is not None else jnp.zeros((w.shape[0], LORA_V), w.dtype)
    pad = jnp.zeros((w.shape[0], LANES - N_HEADS - LORA_V), w.dtype)
    return jnp.concatenate([r, k, v, g, u, gs5, xbc, z, wd, ad, dt, vm, pad], axis=1).astype(BF16)


def _pad_rows(w, start, total=LANES):
    return jnp.pad(w.astype(F32), ((start, total - start - w.shape[0]), (0, 0)))


def _row(a):
    return a.astype(F32).reshape(1, -1)


def kernel(x, norm_g, w_in, w_in_vmix, s5_log_dt, s5_a_re, s5_a_im, s5_b_re, s5_b_im, s5_c_re, s5_c_im, s5_d, s5_w_glu, s5_b_glu, rwkv_mu, rwkv_w_up, rwkv_w0, rwkv_a_up, rwkv_a0, rwkv_k_k, rwkv_k_a, rwkv_r_k, rwkv_ln_g, rwkv_ln_b, rwkv_vmix_mu, rwkv_v_up, rwkv_v0, m2_conv_w, m2_conv_b, m2_dt_bias, m2_a_log, m2_d, m2_norm_g, w_out, final_norm_g):
    bsz, L, d = x.shape
    assert bsz == 1 and d == D_MODEL
    assert L % max(PROJ_TM, LAYER_TB) == 0 and LAYER_TB % M2_T == 0 and LAYER_TB % S5_CT == 0
    depth = w_in.shape[0]
    h = x.reshape(L, d).astype(F32)
    expand = jnp.pad(jnp.repeat(jnp.eye(N_HEADS, dtype=F32), HEAD, axis=1), ((0, LANES - N_HEADS), (0, 0)))
    v_first = None
    for i in range(depth):
        w_cat = _reorder_w_in(w_in[i], w_in_vmix[i - 1] if i > 0 else None)
        proj = _proj(h, _row(norm_g[i]), w_cat)

        mu = rwkv_mu[i].astype(F32)
        mo = np.cumsum([0, BW, BW, BW, LORA_W, LORA_A, BW])
        mr, mk, mv, mwd, mad, mg = (mu[mo[j]:mo[j + 1]] for j in range(6))
        rp = {
            'mu_rw': jnp.concatenate([mr, mk, mv, mg]).reshape(1, -1),
            'mu_wa': jnp.concatenate([mwd, mad]).reshape(1, -1),
            'w_up': _pad_rows(rwkv_w_up[i], 0).astype(BF16), 'w0': _row(rwkv_w0[i]),
            'a_up': _pad_rows(rwkv_a_up[i], LORA_W).astype(BF16), 'a0': _row(rwkv_a0[i]),
            'k_k': _row(rwkv_k_k[i]), 'k_a': _row(rwkv_k_a[i]), 'r_k': _row(rwkv_r_k[i]),
            'ln_g': _row(rwkv_ln_g[i]), 'ln_b': _row(rwkv_ln_b[i]),
        }
        if i > 0:
            rp['mu_misc'] = jnp.pad(rwkv_vmix_mu[i - 1].astype(F32),
                                    (N_HEADS, LANES - N_HEADS - LORA_V)).reshape(1, -1)
            rp['v_up'] = _pad_rows(rwkv_v_up[i - 1], N_HEADS).astype(BF16)
            rp['v0'] = _row(rwkv_v0[i - 1])
        mp = {
            'conv_w': m2_conv_w[i].astype(F32), 'conv_b': _row(m2_conv_b[i]),
            'dt_bias': jnp.pad(m2_dt_bias[i].astype(F32), (0, LANES - N_HEADS)).reshape(1, -1),
            'a_log': jnp.pad(m2_a_log[i].astype(F32), (0, LANES - N_HEADS)).reshape(1, -1),
            'd_full': jnp.repeat(m2_d[i].astype(F32), HEAD).reshape(1, -1),
            'norm_g': _row(m2_norm_g[i]), 'expand': expand.astype(BF16),
        }
        s5_tb = _s5_state_tables(s5_log_dt[i], s5_a_re[i], s5_a_im[i], s5_b_re[i], s5_b_im[i],
                                 s5_c_re[i], s5_c_im[i])
        h, v_first = _layer(h, proj, rp, mp, s5_tb, _row(s5_d[i]), s5_w_glu[i].astype(BF16),
                            _row(s5_b_glu[i]), w_out[i].astype(BF16), _row(final_norm_g), v_first,
                            final=(i == depth - 1))
    return h.reshape(bsz, L, d).astype(x.dtype)
```

```python
import functools
import math

import jax
import jax.numpy as jnp
import numpy as np
from jax import lax
from jax.experimental import pallas as pl
from jax.experimental.pallas import tpu as pltpu

F32 = jnp.float32
BF16 = jnp.bfloat16

D_MODEL = 1024
BW = 512
S5_GROUP = 16
S5_GROUPS = 32
S5_STATE = 64
HEAD = 64
N_HEADS = 8
N_PAIRS = N_HEADS // 2
LORA_W = 64
LORA_A = 64
LORA_V = 32
GN_EPS = 64e-5
M2_GROUPS = 2
M2_STATE = 128
M2_CONV = 4
M2_XBC = 1024
NORM_EPS = 1e-5

LANES = 128
SUBLANES = 8

COL_RW = 0
COL_U = 2048
COL_GS5 = 2560
COL_XBC = 3072
COL_Z = 4096
COL_WA = 4608
COL_MISC = 4736
PROJ_PAD = 4864

PROJ_TM = 512
PROJ_TN = 512
S5_CT = 64
S5_TILE_G = LANES // S5_GROUP
RW_T = 64
RW_NB = 4
LAYER_TB = RW_T * RW_NB
M2_T = 128
RW_TICKS = 21
M2_STAGES = 6 + 7 * (LAYER_TB // M2_T)
S5_STAGES = 8 + 3 * (LAYER_TB // S5_CT)
S5_PRE = 4
VMEM_LIMIT = 48 * 1024 * 1024


def _dot(a, b):
    return jnp.dot(a, b, preferred_element_type=F32)


def _dot_nt(a, b):
    return lax.dot_general(a, b, (((1,), (1,)), ((), ())), preferred_element_type=F32)


def _dot_tn(a, b):
    return lax.dot_general(a, b, (((0,), (0,)), ((), ())), preferred_element_type=F32)


def _split3(x):
    hi = x.astype(BF16)
    r1 = x - hi.astype(F32)
    mid = r1.astype(BF16)
    lo = (r1 - mid.astype(F32)).astype(BF16)
    return hi, mid, lo


def _dot_split_rhs(m, x):
    hi, mid, lo = _split3(x)
    return _dot(m, hi) + _dot(m, mid) + _dot(m, lo)


def _dot_split_lhs(x, m):
    hi, mid, lo = _split3(x)
    return _dot(hi, m) + _dot(mid, m) + _dot(lo, m)


def _b16(x):
    return x.astype(BF16)


def _sigmoid(x):
    return 1.0 / (1.0 + jnp.exp(-x))


def _silu(x):
    return x * _sigmoid(x)


def _softplus(x):
    return jnp.maximum(x, 0.0) + jnp.log(1.0 + jnp.exp(-jnp.abs(x)))


def _gelu_tanh(x):
    c = math.sqrt(2.0 / math.pi)
    return 0.5 * x * (1.0 + jnp.tanh(c * (x + 0.044715 * (x * x * x))))


def _prev_rows(x, halo, is_first):
    rolled = pltpu.roll(x, 1, axis=0)
    last = jnp.where(is_first, 0.0, halo[SUBLANES - 1:SUBLANES, :])
    row = lax.broadcasted_iota(jnp.int32, (SUBLANES, x.shape[1]), 0)
    head = jnp.where(row == 0, last, rolled[:SUBLANES])
    return jnp.concatenate([head, rolled[SUBLANES:]], axis=0)


def _proj_kernel(h_ref, g_ref, w_ref, o_ref):
    x = h_ref[...]
    ms = jnp.mean(x * x, axis=-1, keepdims=True)
    hn = (x * lax.rsqrt(ms + NORM_EPS) * g_ref[...]).astype(BF16)
    for j in range(0, PROJ_PAD, PROJ_TN):
        w = min(PROJ_TN, PROJ_PAD - j)
        o_ref[:, j:j + w] = _dot(hn, w_ref[:, j:j + w])


def _proj(h, g, w):
    L = h.shape[0]
    return pl.pallas_call(
        _proj_kernel,
        grid=(L // PROJ_TM,),
        in_specs=[
            pl.BlockSpec((PROJ_TM, D_MODEL), lambda i: (i, 0)),
            pl.BlockSpec((1, D_MODEL), lambda i: (0, 0)),
            pl.BlockSpec((D_MODEL, PROJ_PAD), lambda i: (0, 0), pipeline_mode=pl.Buffered(1)),
        ],
        out_specs=pl.BlockSpec((PROJ_TM, PROJ_PAD), lambda i: (i, 0)),
        out_shape=jax.ShapeDtypeStruct((L, PROJ_PAD), F32),
        compiler_params=pltpu.CompilerParams(
            dimension_semantics=("parallel",), vmem_limit_bytes=VMEM_LIMIT),
    )(h, g, w)


def _mamba_stages(first, xbc_ref, xbcp_ref, z_ref, misc_ref, cw_ref, cb_ref, dtb_ref, alog_ref, dfull_ref,
                  ng_ref, expand_ref, tri_ref, y_ref, st_ref):
    T = M2_T
    x = xbc_ref[...]
    halo = jnp.where(first, 0.0, xbcp_ref[...])
    row8 = lax.broadcasted_iota(jnp.int32, halo.shape, 0)
    acc = x * cw_ref[M2_CONV - 1:M2_CONV, :] + cb_ref[...]
    yield
    for j in range(1, M2_CONV):
        xr = pltpu.roll(x, j, axis=0)
        head = jnp.where(row8 < j, pltpu.roll(halo, j, axis=0), xr[:SUBLANES])
        xj = jnp.concatenate([head, xr[SUBLANES:]], axis=0)
        acc = acc + xj * cw_ref[M2_CONV - 1 - j:M2_CONV - j, :]
        yield
    xbc = _silu(acc)
    xm = xbc[:, :BW]
    bm_b = _b16(xbc[:, BW:BW + M2_GROUPS * M2_STATE])
    cm_b = _b16(xbc[:, BW + M2_GROUPS * M2_STATE:])
    yield
    dt = _softplus(misc_ref[...] + dtb_ref[...])
    da = dt * (-jnp.exp(alog_ref[...]))
    expand = expand_ref[...]
    xdt = xm * _dot_split_lhs(dt, expand)
    xdt_b = _b16(xdt)
    gate = _silu(z_ref[...])
    yield

    rr = lax.broadcasted_iota(jnp.int32, (T, T), 0)
    cc = lax.broadcasted_iota(jnp.int32, (T, T), 1)
    causal = cc <= rr
    head0 = lax.broadcasted_iota(jnp.int32, (T, LANES), 1) < HEAD
    states = [st_ref[q] for q in range(N_PAIRS)]
    for ch in range(x.shape[0] // T):
        rows = slice(ch * T, (ch + 1) * T)
        cs = _dot_split_rhs(tri_ref[...], da[rows])
        cs_t = cs.T
        cs_full = _dot_split_lhs(cs, expand)
        cs_last = cs_full[T - 1:T, :]
        ecs = jnp.exp(cs_full)
        xdte_b = _b16(xdt[rows] * jnp.exp(cs_last - cs_full))
        chunk_decay = jnp.exp(cs_last)
        yield
        scores = [_dot_nt(cm_b[rows, gi * M2_STATE:(gi + 1) * M2_STATE], bm_b[rows, gi * M2_STATE:(gi + 1) * M2_STATE])
                  for gi in range(M2_GROUPS)]
        yield
        ys = []
        for q in range(N_PAIRS):
            gi = (2 * q) // (N_HEADS // M2_GROUPS)
            gs = slice(gi * M2_STATE, (gi + 1) * M2_STATE)
            sl = slice(q * LANES, (q + 1) * LANES)
            yd = []
            for h in (2 * q, 2 * q + 1):
                seg = cs[:, h:h + 1] - cs_t[h:h + 1, :]
                dec = jnp.where(causal, jnp.exp(jnp.minimum(seg, 0.0)), 0.0)
                yd.append(_dot(_b16(scores[gi] * dec), xdt_b[rows, sl]))
            st = states[q]
            y_off = _dot(cm_b[rows, gs], _b16(st)) * ecs[:, sl]
            states[q] = st * chunk_decay[:, sl] + _dot_tn(bm_b[rows, gs], xdte_b[:, sl])
            ys.append(jnp.where(head0, yd[0], yd[1]) + y_off)
            yield
        y = jnp.concatenate(ys, axis=1) + xm[rows] * dfull_ref[...]
        y = y * gate[rows]
        ms = jnp.mean(y * y, axis=-1, keepdims=True)
        y_ref[rows, :] = y * lax.rsqrt(ms + NORM_EPS) * ng_ref[...]
        yield
    for q in range(N_PAIRS):
        st_ref[q] = states[q]


def _rwkv_body(first, tick, prep_tick, refs, has_vmix):
    if has_vmix:
        (rw_ref, rwp_ref, wa_ref, wap_ref, misc_ref, miscp_ref, vfirst_ref,
         mu_rw_ref, mu_wa_ref, mu_misc_ref, wup_ref, w0_ref, aup_ref, a0_ref, kk_ref, ka_ref, rk_ref,
         lng_ref, lnb_ref, vup_ref, v0_ref, tri_ref, y_ref, s_ref) = refs
    else:
        (rw_ref, rwp_ref, wa_ref, wap_ref,
         mu_rw_ref, mu_wa_ref, wup_ref, w0_ref, aup_ref, a0_ref, kk_ref, ka_ref, rk_ref,
         lng_ref, lnb_ref, tri_ref, y_ref, vout_ref, s_ref) = refs
    T = RW_T
    TB = LAYER_TB
    b16 = _b16

    def lerp(ref, halo_ref, mu_ref):
        x = ref[...]
        return x + (_prev_rows(x, halo_ref[...], first) - x) * mu_ref[...]

    xs = lerp(rw_ref, rwp_ref, mu_rw_ref)
    r, k, v, g = (xs[:, i * BW:(i + 1) * BW] for i in range(4))
    prep_tick()
    wa = lerp(wa_ref, wap_ref, mu_wa_ref)
    w_log = -_softplus(-(w0_ref[...] + _dot(b16(jnp.tanh(wa)), wup_ref[...]))) - 0.5
    lw = -jnp.exp(w_log)
    a = _sigmoid(a0_ref[...] + _dot(b16(wa), aup_ref[...]))
    if has_vmix:
        vd = lerp(misc_ref, miscp_ref, mu_misc_ref)
        mix = _sigmoid(v0_ref[...] + _dot(b16(vd), vup_ref[...]))
        v = v + (vfirst_ref[...] - v) * mix
    else:
        vout_ref[...] = v
    prep_tick()
    kk = k * kk_ref[...]
    kmod = k * (1.0 + (a - 1.0) * ka_ref[...])
    cums = _dot_split_rhs(tri_ref[...], lw)
    cl, rv = cums[:TB], cums[TB:]
    g_incl = jnp.exp(cl)
    g_inv = jnp.exp(-cl)
    g_end = jnp.exp(rv)
    rkr = r * kmod * rk_ref[...]
    prep_tick()

    def per_head_sum(x):
        h0 = lax.broadcasted_iota(jnp.int32, x.shape, 1) < HEAD
        f0 = h0.astype(F32)
        s0 = jnp.sum(jnp.where(h0, x, 0.0), axis=-1, keepdims=True)
        s1 = jnp.sum(jnp.where(h0, 0.0, x), axis=-1, keepdims=True)
        return s0 * f0 + s1 * (1.0 - f0)

    inv, bonus_w = [], []
    for q in range(N_PAIRS):
        sl = slice(q * LANES, (q + 1) * LANES)
        sq = kk[:, sl] * kk[:, sl]
        inv.append(1.0 / jnp.maximum(jnp.sqrt(per_head_sum(sq)), 1e-12))
        bonus_w.append(per_head_sum(rkr[:, sl]))
    kkn = kk * jnp.concatenate(inv, axis=1)
    bonus = jnp.concatenate(bonus_w, axis=1) * v
    prep_tick()
    kkna = kkn * a
    na_b = b16(-kkn * jnp.exp(cl - lw))
    nb_b = b16(kkna * g_inv)
    nbe_b = b16(kkna * g_end)
    pk_b = b16(kmod * g_inv)
    pke_b = b16(kmod * g_end)
    pr = r * g_incl
    pr_b = b16(pr)
    v_b = b16(v)
    gated = _silu(g)

    m0 = (lax.broadcasted_iota(jnp.int32, (1, LANES), 1) < HEAD).astype(BF16)
    m1 = 1.0 - m0
    row_w = lax.broadcasted_iota(jnp.int32, (T, LANES), 0)
    col_w = lax.broadcasted_iota(jnp.int32, (T, LANES), 1) % HEAD
    strict = col_w < row_w
    incl = col_w <= row_w
    eye_w = (col_w == row_w).astype(F32)
    rr = lax.broadcasted_iota(jnp.int32, (LANES, LANES), 0)
    cc = lax.broadcasted_iota(jnp.int32, (LANES, LANES), 1)
    same_head = (rr < HEAD) == (cc < HEAD)

    def stack(x):
        return jnp.concatenate([x * m0, x * m1], axis=0)

    def rows(c):
        return slice(c * T, (c + 1) * T)

    def lanes(q):
        return slice(q * LANES, (q + 1) * LANES)

    state = [s_ref[q] for q in range(N_PAIRS)]
    units = [(c, q) for c in range(RW_NB) for q in range(N_PAIRS)]
    a_n = [na_b[rows(c), lanes(q)] for c, q in units]
    r_n = [pr_b[rows(c), lanes(q)] for c, q in units]
    v_n = [v_b[rows(c), lanes(q)] for c, q in units]
    be_n = [nbe_b[rows(c), lanes(q)] for c, q in units]
    ke_n = [pke_b[rows(c), lanes(q)] for c, q in units]
    b_st = [stack(nb_b[rows(c), lanes(q)]) for c, q in units]
    k_st = [stack(pk_b[rows(c), lanes(q)]) for c, q in units]
    a_st = [stack(x) for x in a_n]
    v_st = [stack(x) for x in v_n]
    tick()
    aa = [_dot_nt(jnp.concatenate([a_, r_], axis=0), jnp.concatenate([b_, k_], axis=0))
          for a_, r_, b_, k_ in zip(a_n, r_n, b_st, k_st)]
    tick()
    a_ak = [b16(jnp.where(strict, x[:T, LANES:], 0.0)) for x in aa]
    a_rb = [b16(jnp.where(incl, x[T:, :LANES], 0.0)) for x in aa]
    a_rk = [b16(jnp.where(incl, x[T:, LANES:], 0.0)) for x in aa]
    p = [jnp.where(strict, x[:T, :LANES], 0.0) for x in aa]
    minv = [eye_w + x for x in p]
    tick()
    for _ in range(int(math.log2(T)) - 1):
        p_b = [b16(x) for x in p]
        p = [_dot(x, stack(x)) for x in p_b]
        tick()
        minv = [m + _dot(b16(x), stack(b16(m))) for x, m in zip(p, minv)]
        tick()
    akv = [_dot(x, y) for x, y in zip(a_ak, v_st)]
    tick()
    wu_b = [b16(_dot(b16(m), jnp.concatenate([a_, stack(b16(x))], axis=1)))
            for m, a_, x in zip(minv, a_st, akv)]
    tick()
    p_mat = [b16(jnp.where(same_head, _dot_tn(w[:, :LANES], be), 0.0)) for w, be in zip(wu_b, be_n)]
    tick()
    z_mat = [jnp.where(same_head,
                       _dot_tn(jnp.concatenate([w[:, LANES:], vn], axis=0), jnp.concatenate([be, ke], axis=0)),
                       0.0)
             for w, vn, be, ke in zip(wu_b, v_n, be_n, ke_n)]
    tick()
    q_mat = [b16(pr[rows(c), lanes(q)] + _dot(arb, stack(w[:, :LANES])))
             for (c, q), arb, w in zip(units, a_rb, wu_b)]
    tick()
    y0 = [_dot(jnp.concatenate([arb, ark], axis=1), jnp.concatenate([stack(w[:, LANES:]), vs], axis=0))
          for arb, ark, w, vs in zip(a_rb, a_rk, wu_b, v_st)]
    tick()
    ys = []
    for i, (c, q) in enumerate(units):
        s = state[q]
        s_b = b16(s)
        ys.append(_dot_nt(q_mat[i], s_b) + y0[i])
        state[q] = s * g_incl[(c + 1) * T - 1:(c + 1) * T, lanes(q)] + _dot(s_b, p_mat[i]) + z_mat[i]
    tick()
    for i, (c, q) in enumerate(units):
        y = ys[i]
        mean = per_head_sum(y) * (1.0 / HEAD)
        cen = y - mean
        var = per_head_sum(cen * cen) * (1.0 / HEAD)
        yn = cen * lax.rsqrt(var + GN_EPS)
        y_ref[rows(c), lanes(q)] = ((yn * lng_ref[:, lanes(q)] + lnb_ref[:, lanes(q)]
                                     + bonus[rows(c), lanes(q)]) * gated[rows(c), lanes(q)])
    tick()
    for q in range(N_PAIRS):
        s_ref[q] = state[q]


def _s5_state_tables(log_dt, a_re, a_im, b_re, b_im, c_re, c_im):
    T = S5_CT
    G, P = a_re.shape
    dt = jnp.exp(log_dt.astype(F32))[:, None]
    ar, ai = a_re.astype(F32), a_im.astype(F32)
    mag = jnp.exp(dt * ar)
    abar_re, abar_im = mag * jnp.cos(dt * ai), mag * jnp.sin(dt * ai)
    den = ar * ar + ai * ai
    f_re = ((abar_re - 1.0) * ar + abar_im * ai) / den
    f_im = (abar_im * ar - (abar_re - 1.0) * ai) / den
    br, bi = b_re.astype(F32), b_im.astype(F32)
    bb_re = f_re[..., None] * br - f_im[..., None] * bi
    bb_im = f_re[..., None] * bi + f_im[..., None] * br

    def powers(e):
        e = jnp.asarray(e, F32)[:, None, None]
        m = jnp.exp(e * (dt * ar)[None])
        ang = e * (dt * ai)[None]
        return (m * jnp.cos(ang)).reshape(-1, G * P), (m * jnp.sin(ang)).reshape(-1, G * P)

    pos = np.arange(T, dtype=np.float32)
    half = T / 2
    zr, zi = powers(half - pos)
    sr, si = powers(pos - half)
    cr, ci = powers(np.array([half + 1.0], np.float32))
    tiles = G // S5_TILE_G
    eye = jnp.eye(S5_TILE_G, dtype=F32)

    def b_side(bb):
        t = bb.reshape(tiles, S5_TILE_G, P, S5_GROUP).transpose(0, 1, 3, 2)
        return (t[:, :, :, None, :] * eye[None, :, None, :, None]).reshape(tiles, LANES, S5_TILE_G * P)

    def c_side(cc):
        t = cc.astype(F32).reshape(tiles, S5_TILE_G, S5_GROUP, P).transpose(0, 1, 3, 2)
        return (t[:, :, :, None, :] * eye[None, :, None, :, None]).reshape(tiles, S5_TILE_G * P, LANES)

    wb = jnp.concatenate([b_side(bb_re), b_side(bb_im)], axis=-1).astype(BF16)
    car = jnp.concatenate([cr, ci, sr[T - 1:T], si[T - 1:T]], axis=0)
    return dict(wb=wb, wc_re=c_side(c_re).astype(BF16), wc_im=c_side(-c_im.astype(F32)).astype(BF16),
                zr=zr.astype(BF16), zi=zi.astype(BF16), sr=sr.astype(BF16), si=si.astype(BF16), car=car)


def _s5_stages(box, u_ref, wb_ref, wcr_ref, wci_ref, zr_ref, zi_ref, sr_ref, si_ref, car_ref, d_ref, tri_ref,
               st_ref, sre_ref, sim_ref):
    T = S5_CT
    NS = S5_GROUPS * S5_STATE
    tiles = S5_GROUPS // S5_TILE_G
    half = NS // tiles
    u = u_ref[...]
    ub = u.astype(BF16)
    bu = []
    for j in range(tiles):
        bu.append(_b16(_dot(ub[:, j * LANES:(j + 1) * LANES], wb_ref[j])))
        yield
    bu_re = jnp.concatenate([b[:, :half] for b in bu], axis=1)
    bu_im = jnp.concatenate([b[:, half:] for b in bu], axis=1)
    zr, zi, sr, si = zr_ref[...], zi_ref[...], sr_ref[...], si_ref[...]
    car_re, car_im = car_ref[0:1, :], car_ref[1:2, :]
    sr_last, si_last = car_ref[2:3, :], car_ref[3:4, :]
    p_re, p_im = st_ref[0:1, :], st_ref[1:2, :]
    tri = tri_ref[...]
    for c in range(u.shape[0] // T):
        rows = slice(c * T, (c + 1) * T)
        b_re, b_im = bu_re[rows], bu_im[rows]
        z = jnp.concatenate([zr * b_re - zi * b_im, zr * b_im + zi * b_re], axis=1)
        yield
        w = _dot(tri, z)
        yield
        w_re = w[:, :NS] + (car_re * p_re - car_im * p_im)
        w_im = w[:, NS:] + (car_re * p_im + car_im * p_re)
        w_re_b, w_im_b = _b16(w_re), _b16(w_im)
        sre_ref[rows, :] = sr * w_re_b - si * w_im_b
        sim_ref[rows, :] = sr * w_im_b + si * w_re_b
        l_re, l_im = w_re[T - 1:T, :], w_im[T - 1:T, :]
        p_re = sr_last * l_re - si_last * l_im
        p_im = sr_last * l_im + si_last * l_re
        yield
    st_ref[0:1, :] = p_re
    st_ref[1:2, :] = p_im
    ys = []
    for j in range(tiles):
        ys.append(_dot(sre_ref[:, j * half:(j + 1) * half], wcr_ref[j])
                  + _dot(sim_ref[:, j * half:(j + 1) * half], wci_ref[j]))
        yield
    box['ys'] = jnp.concatenate(ys, axis=1) + d_ref[...] * u


N_RW_IN = {True: 22, False: 16}
N_MAMBA_IN = 12
N_S5_IN = 17


def _layer_kernel(*refs, has_vmix, final):
    n_rw_in = N_RW_IN[has_vmix]
    rw_in = refs[:n_rw_in]
    m_in = refs[n_rw_in:n_rw_in + N_MAMBA_IN]
    s5_in = refs[n_rw_in + N_MAMBA_IN:n_rw_in + N_MAMBA_IN + N_S5_IN]
    rest = refs[n_rw_in + N_MAMBA_IN + N_S5_IN:]
    (h_ref, u_ref, gs5_ref, wb_ref, wcr_ref, wci_ref, zr_ref, zi_ref, sr_ref, si_ref, car_ref, d_ref, tri5_ref,
     wglu_ref, bglu_ref, wo_ref, fg_ref) = s5_in
    if has_vmix:
        o_ref, s_ref, st_ref, s5st_ref, sre_ref, sim_ref, yrw_ref, ym2_ref = rest
        rw_refs = (*rw_in, yrw_ref, s_ref)
    else:
        o_ref, vout_ref, s_ref, st_ref, s5st_ref, sre_ref, sim_ref, yrw_ref, ym2_ref = rest
        rw_refs = (*rw_in, yrw_ref, vout_ref, s_ref)
    first = pl.program_id(0) == 0

    @pl.when(first)
    def _():
        s_ref[...] = jnp.zeros_like(s_ref)
        st_ref[...] = jnp.zeros_like(st_ref)
        s5st_ref[...] = jnp.zeros_like(s5st_ref)

    box = {}
    mamba = _mamba_stages(first, *m_in, ym2_ref, st_ref)
    s5 = _s5_stages(box, u_ref, wb_ref, wcr_ref, wci_ref, zr_ref, zi_ref, sr_ref, si_ref, car_ref, d_ref,
                    tri5_ref, s5st_ref, sre_ref, sim_ref)
    streams = [[mamba, M2_STAGES, 0], [s5, S5_STAGES, S5_PRE]]
    calls = [0]

    def tick():
        calls[0] += 1
        for st in streams:
            target = min(st[1], -(-calls[0] * st[1] // (RW_TICKS - 1)))
            while st[2] < target:
                next(st[0], None)
                st[2] += 1
        if calls[0] == RW_TICKS - 1:
            for st in streams:
                for _ in st[0]:
                    pass
            ys = _gelu_tanh(box['ys'])
            ys = ys * _sigmoid(_dot(ys.astype(BF16), wglu_ref[...]) + bglu_ref[...])
            ys = ys * _silu(gs5_ref[...])
            box['acc'] = (_dot(ys.astype(BF16), wo_ref[0:BW, :])
                          + _dot(ym2_ref[...].astype(BF16), wo_ref[2 * BW:3 * BW, :]))

    _rwkv_body(first, tick, lambda: next(s5), rw_refs, has_vmix)
    acc = box['acc'] + _dot(yrw_ref[...].astype(BF16), wo_ref[BW:2 * BW, :])
    hn = h_ref[...] + acc
    if final:
        ms = jnp.mean(hn * hn, axis=-1, keepdims=True)
        hn = hn * lax.rsqrt(ms + NORM_EPS) * fg_ref[...]
    o_ref[...] = hn


def _layer(h, proj, rp, mp, tb, d_row, w_glu, b_glu, w_out, final_g, v_first, final):
    L = proj.shape[0]
    TB = LAYER_TB
    NS = S5_GROUPS * S5_STATE
    has_vmix = v_first is not None
    hb = TB // SUBLANES

    def cur(width, col):
        return pl.BlockSpec((TB, width), lambda i: (i, col // width))

    def halo(width, col):
        return pl.BlockSpec((SUBLANES, width), lambda i: (jnp.maximum(i * hb - 1, 0), col // width))

    def full(a):
        return pl.BlockSpec(a.shape, (lambda i: (0, 0, 0)) if a.ndim == 3 else (lambda i: (0, 0)),
                            pipeline_mode=pl.Buffered(1))

    pos = np.arange(TB)
    same = (pos[:, None] // RW_T) == (pos[None, :] // RW_T)
    rw_tri = jnp.asarray(np.concatenate([same & (pos[None, :] <= pos[:, None]),
                                         same & (pos[None, :] > pos[:, None])], axis=0), BF16)
    args = [proj, proj, proj, proj]
    specs = [cur(4 * BW, COL_RW), halo(4 * BW, COL_RW), cur(LANES, COL_WA), halo(LANES, COL_WA)]
    if has_vmix:
        args += [proj, proj, v_first]
        specs += [cur(LANES, COL_MISC), halo(LANES, COL_MISC), pl.BlockSpec((TB, BW), lambda i: (i, 0))]
        names = ['mu_rw', 'mu_wa', 'mu_misc', 'w_up', 'w0', 'a_up', 'a0', 'k_k', 'k_a', 'r_k',
                 'ln_g', 'ln_b', 'v_up', 'v0']
    else:
        names = ['mu_rw', 'mu_wa', 'w_up', 'w0', 'a_up', 'a0', 'k_k', 'k_a', 'r_k', 'ln_g', 'ln_b']
    consts = [rp[n] for n in names] + [rw_tri]
    args += consts
    specs += [full(c) for c in consts]
    assert len(args) == N_RW_IN[has_vmix]

    m_tri = jnp.tril(jnp.ones((M2_T, M2_T), BF16))
    m_consts = [mp['conv_w'], mp['conv_b'], mp['dt_bias'], mp['a_log'], mp['d_full'], mp['norm_g'],
                mp['expand'], m_tri]
    args += [proj, proj, proj, proj] + m_consts
    specs += [cur(M2_XBC, COL_XBC), halo(M2_XBC, COL_XBC), cur(BW, COL_Z), cur(LANES, COL_MISC)]
    specs += [full(c) for c in m_consts]
    assert len(m_consts) + 4 == N_MAMBA_IN

    s5_tri = jnp.tril(jnp.ones((S5_CT, S5_CT), BF16))
    s5_consts = [tb['wb'], tb['wc_re'], tb['wc_im'], tb['zr'], tb['zi'], tb['sr'], tb['si'], tb['car'],
                 d_row, s5_tri, w_glu, b_glu, w_out, final_g]
    args += [h, proj, proj] + s5_consts
    specs += [pl.BlockSpec((TB, D_MODEL), lambda i: (i, 0)), cur(BW, COL_U), cur(BW, COL_GS5)]
    specs += [full(c) for c in s5_consts]
    assert len(s5_consts) + 3 == N_S5_IN

    tok = pl.BlockSpec((TB, BW), lambda i: (i, 0))
    out_specs = [pl.BlockSpec((TB, D_MODEL), lambda i: (i, 0))]
    out_shape = [jax.ShapeDtypeStruct((L, D_MODEL), F32)]
    if not has_vmix:
        out_specs.append(tok)
        out_shape.append(jax.ShapeDtypeStruct((L, BW), F32))
    res = pl.pallas_call(
        functools.partial(_layer_kernel, has_vmix=has_vmix, final=final),
        grid=(L // TB,),
        in_specs=specs,
        out_specs=out_specs,
        out_shape=out_shape,
        scratch_shapes=[pltpu.VMEM((N_PAIRS, LANES, LANES), F32), pltpu.VMEM((N_PAIRS, M2_STATE, LANES), F32),
                        pltpu.VMEM((SUBLANES, NS), F32), pltpu.VMEM((TB, NS), BF16), pltpu.VMEM((TB, NS), BF16),
                        pltpu.VMEM((TB, BW), F32), pltpu.VMEM((TB, BW), F32)],
        compiler_params=pltpu.CompilerParams(
            dimension_semantics=("arbitrary",), vmem_limit_bytes=VMEM_LIMIT),
    )(*args)
    return res[0], (v_first if has_vmix else res[1])


def _reorder_w_in(w, w_vmix):
    o = np.cumsum([0, BW, BW, BW, BW, BW, LORA_W, LORA_A, BW, M2_XBC, N_HEADS, BW])
    u, gs5, r, k, v, wd, ad, g, xbc, dt, z = (w[:, o[i]:o[i + 1]] for i in range(11))
    vm = w_vmix if w_vmix is not None else jnp.zeros((w.shape[0], LORA_V), w.dtype)
    pad = jnp.zeros((w.shape[0], LANES - N_HEADS - LORA_V), w.dtype)
    return jnp.concatenate([r, k, v, g, u, gs5, xbc, z, wd, ad, dt, vm, pad], axis=1).astype(BF16)


def _pad_rows(w, start, total=LANES):
    return jnp.pad(w.astype(F32), ((start, total - start - w.shape[0]), (0, 0)))


def _row(a):
    return a.astype(F32).reshape(1, -1)


def kernel(x, norm_g, w_in, w_in_vmix, s5_log_dt, s5_a_re, s5_a_im, s5_b_re, s5_b_im, s5_c_re, s5_c_im, s5_d, s5_w_glu, s5_b_glu, rwkv_mu, rwkv_w_up, rwkv_w0, rwkv_a_up, rwkv_a0, rwkv_k_k, rwkv_k_a, rwkv_r_k, rwkv_ln_g, rwkv_ln_b, rwkv_vmix_mu, rwkv_v_up, rwkv_v0, m2_conv_w, m2_conv_b, m2_dt_bias, m2_a_log, m2_d, m2_norm_g, w_out, final_norm_g):
    bsz, L, d = x.shape
    assert bsz == 1 and d == D_MODEL
    assert L % max(PROJ_TM, LAYER_TB) == 0 and LAYER_TB % M2_T == 0 and LAYER_TB % S5_CT == 0
    depth = w_in.shape[0]
    h = x.reshape(L, d).astype(F32)
    expand = jnp.pad(jnp.repeat(jnp.eye(N_HEADS, dtype=F32), HEAD, axis=1), ((0, LANES - N_HEADS), (0, 0)))
    v_first = None
    for i in range(depth):
        w_cat = _reorder_w_in(w_in[i], w_in_vmix[i - 1] if i > 0 else None)
        proj = _proj(h, _row(norm_g[i]), w_cat)

        mu = rwkv_mu[i].astype(F32)
        mo = np.cumsum([0, BW, BW, BW, LORA_W, LORA_A, BW])
        mr, mk, mv, mwd, mad, mg = (mu[mo[j]:mo[j + 1]] for j in range(6))
        rp = {
            'mu_rw': jnp.concatenate([mr, mk, mv, mg]).reshape(1, -1),
            'mu_wa': jnp.concatenate([mwd, mad]).reshape(1, -1),
            'w_up': _pad_rows(rwkv_w_up[i], 0).astype(BF16), 'w0': _row(rwkv_w0[i]),
            'a_up': _pad_rows(rwkv_a_up[i], LORA_W).astype(BF16), 'a0': _row(rwkv_a0[i]),
            'k_k': _row(rwkv_k_k[i]), 'k_a': _row(rwkv_k_a[i]), 'r_k': _row(rwkv_r_k[i]),
            'ln_g': _row(rwkv_ln_g[i]), 'ln_b': _row(rwkv_ln_b[i]),
        }
        if i > 0:
            rp['mu_misc'] = jnp.pad(rwkv_vmix_mu[i - 1].astype(F32),
                                    (N_HEADS, LANES - N_HEADS - LORA_V)).reshape(1, -1)
            rp['v_up'] = _pad_rows(rwkv_v_up[i - 1], N_HEADS).astype(BF16)
            rp['v0'] = _row(rwkv_v0[i - 1])
        mp = {
            'conv_w': m2_conv_w[i].astype(F32), 'conv_b': _row(m2_conv_b[i]),
            'dt_bias': jnp.pad(m2_dt_bias[i].astype(F32), (0, LANES - N_HEADS)).reshape(1, -1),
            'a_log': jnp.pad(m2_a_log[i].astype(F32), (0, LANES - N_HEADS)).reshape(1, -1),
            'd_full': jnp.repeat(m2_d[i].astype(F32), HEAD).reshape(1, -1),
            'norm_g': _row(m2_norm_g[i]), 'expand': expand.astype(BF16),
        }
        s5_tb = _s5_state_tables(s5_log_dt[i], s5_a_re[i], s5_a_im[i], s5_b_re[i], s5_b_im[i],
                                 s5_c_re[i], s5_c_im[i])
        h, v_first = _layer(h, proj, rp, mp, s5_tb, _row(s5_d[i]), s5_w_glu[i].astype(BF16),
                            _row(s5_b_glu[i]), w_out[i].astype(BF16), _row(final_norm_g), v_first,
                            final=(i == depth - 1))
    return h.reshape(bsz, L, d).astype(x.dtype)
```

```python
import functools
import math

import jax
import jax.numpy as jnp
import numpy as np
from jax import lax
from jax.experimental import pallas as pl
from jax.experimental.pallas import tpu as pltpu

F32 = jnp.float32
BF16 = jnp.bfloat16

D_MODEL = 1024
BW = 512
S5_GROUP = 16
S5_GROUPS = 32
S5_STATE = 64
HEAD = 64
N_HEADS = 8
N_PAIRS = N_HEADS // 2
LORA_W = 64
LORA_A = 64
LORA_V = 32
GN_EPS = 64e-5
M2_GROUPS = 2
M2_STATE = 128
M2_CONV = 4
M2_XBC = 1024
NORM_EPS = 1e-5

LANES = 128
SUBLANES = 8

COL_RW = 0
COL_U = 2048
COL_GS5 = 2560
COL_XBC = 3072
COL_Z = 4096
COL_WA = 4608
COL_MISC = 4736
PROJ_PAD = 4864

PROJ_TM = 512
PROJ_TN = 512
S5_CT = 64
S5_TILE_G = LANES // S5_GROUP
RW_T = 64
RW_NB = 4
LAYER_TB = RW_T * RW_NB
M2_T = 128
RW_TICKS = 21
M2_STAGES = 6 + 7 * (LAYER_TB // M2_T)
S5_STAGES = 8 + 3 * (LAYER_TB // S5_CT)
S5_PRE = 4
VMEM_LIMIT = 48 * 1024 * 1024


def _dot(a, b):
    return jnp.dot(a, b, preferred_element_type=F32)


def _dot_nt(a, b):
    return lax.dot_general(a, b, (((1,), (1,)), ((), ())), preferred_element_type=F32)


def _dot_tn(a, b):
    return lax.dot_general(a, b, (((0,), (0,)), ((), ())), preferred_element_type=F32)


def _split3(x):
    hi = x.astype(BF16)
    r1 = x - hi.astype(F32)
    mid = r1.astype(BF16)
    lo = (r1 - mid.astype(F32)).astype(BF16)
    return hi, mid, lo


def _dot_split_rhs(m, x):
    hi, mid, lo = _split3(x)
    return _dot(m, hi) + _dot(m, mid) + _dot(m, lo)


def _dot_split_lhs(x, m):
    hi, mid, lo = _split3(x)
    return _dot(hi, m) + _dot(mid, m) + _dot(lo, m)


def _b16(x):
    return x.astype(BF16)


def _sigmoid(x):
    return 1.0 / (1.0 + jnp.exp(-x))


def _silu(x):
    return x * _sigmoid(x)


def _softplus(x):
    return jnp.maximum(x, 0.0) + jnp.log(1.0 + jnp.exp(-jnp.abs(x)))


def _gelu_tanh(x):
    c = math.sqrt(2.0 / math.pi)
    return 0.5 * x * (1.0 + jnp.tanh(c * (x + 0.044715 * (x * x * x))))


def _prev_rows(x, halo, is_first):
    rolled = pltpu.roll(x, 1, axis=0)
    last = jnp.where(is_first, 0.0, halo[SUBLANES - 1:SUBLANES, :])
    row = lax.broadcasted_iota(jnp.int32, (SUBLANES, x.shape[1]), 0)
    head = jnp.where(row == 0, last, rolled[:SUBLANES])
    return jnp.concatenate([head, rolled[SUBLANES:]], axis=0)


def _proj_kernel(h_ref, g_ref, w_ref, o_ref):
    x = h_ref[...]
    ms = jnp.mean(x * x, axis=-1, keepdims=True)
    hn = (x * lax.rsqrt(ms + NORM_EPS) * g_ref[...]).astype(BF16)
    for j in range(0, PROJ_PAD, PROJ_TN):
        w = min(PROJ_TN, PROJ_PAD - j)
        o_ref[:, j:j + w] = _dot(hn, w_ref[:, j:j + w])


def _proj(h, g, w):
    L = h.shape[0]
    return pl.pallas_call(
        _proj_kernel,
        grid=(L // PROJ_TM,),
        in_specs=[
            pl.BlockSpec((PROJ_TM, D_MODEL), lambda i: (i, 0)),
            pl.BlockSpec((1, D_MODEL), lambda i: (0, 0)),
            pl.BlockSpec((D_MODEL, PROJ_PAD), lambda i: (0, 0), pipeline_mode=pl.Buffered(1)),
        ],
        out_specs=pl.BlockSpec((PROJ_TM, PROJ_PAD), lambda i: (i, 0)),
        out_shape=jax.ShapeDtypeStruct((L, PROJ_PAD), F32),
        compiler_params=pltpu.CompilerParams(
            dimension_semantics=("parallel",), vmem_limit_bytes=VMEM_LIMIT),
    )(h, g, w)


def _mamba_stages(first, xbc_ref, xbcp_ref, z_ref, misc_ref, cw_ref, cb_ref, dtb_ref, alog_ref, dfull_ref,
                  ng_ref, expand_ref, tri_ref, y_ref, st_ref):
    T = M2_T
    x = xbc_ref[...]
    halo = jnp.where(first, 0.0, xbcp_ref[...])
    row8 = lax.broadcasted_iota(jnp.int32, halo.shape, 0)
    acc = x * cw_ref[M2_CONV - 1:M2_CONV, :] + cb_ref[...]
    yield
    for j in range(1, M2_CONV):
        xr = pltpu.roll(x, j, axis=0)
        head = jnp.where(row8 < j, pltpu.roll(halo, j, axis=0), xr[:SUBLANES])
        xj = jnp.concatenate([head, xr[SUBLANES:]], axis=0)
        acc = acc + xj * cw_ref[M2_CONV - 1 - j:M2_CONV - j, :]
        yield
    xbc = _silu(acc)
    xm = xbc[:, :BW]
    bm_b = _b16(xbc[:, BW:BW + M2_GROUPS * M2_STATE])
    cm_b = _b16(xbc[:, BW + M2_GROUPS * M2_STATE:])
    yield
    dt = _softplus(misc_ref[...] + dtb_ref[...])
    da = dt * (-jnp.exp(alog_ref[...]))
    expand = expand_ref[...]
    xdt = xm * _dot_split_lhs(dt, expand)
    xdt_b = _b16(xdt)
    gate = _silu(z_ref[...])
    yield

    rr = lax.broadcasted_iota(jnp.int32, (T, T), 0)
    cc = lax.broadcasted_iota(jnp.int32, (T, T), 1)
    causal = cc <= rr
    head0 = lax.broadcasted_iota(jnp.int32, (T, LANES), 1) < HEAD
    states = [st_ref[q] for q in range(N_PAIRS)]
    for ch in range(x.shape[0] // T):
        rows = slice(ch * T, (ch + 1) * T)
        cs = _dot_split_rhs(tri_ref[...], da[rows])
        cs_t = cs.T
        cs_full = _dot_split_lhs(cs, expand)
        cs_last = cs_full[T - 1:T, :]
        ecs = jnp.exp(cs_full)
        xdte_b = _b16(xdt[rows] * jnp.exp(cs_last - cs_full))
        chunk_decay = jnp.exp(cs_last)
        yield
        scores = [_dot_nt(cm_b[rows, gi * M2_STATE:(gi + 1) * M2_STATE], bm_b[rows, gi * M2_STATE:(gi + 1) * M2_STATE])
                  for gi in range(M2_GROUPS)]
        yield
        ys = []
        for q in range(N_PAIRS):
            gi = (2 * q) // (N_HEADS // M2_GROUPS)
            gs = slice(gi * M2_STATE, (gi + 1) * M2_STATE)
            sl = slice(q * LANES, (q + 1) * LANES)
            yd = []
            for h in (2 * q, 2 * q + 1):
                seg = cs[:, h:h + 1] - cs_t[h:h + 1, :]
                dec = jnp.where(causal, jnp.exp(jnp.minimum(seg, 0.0)), 0.0)
                yd.append(_dot(_b16(scores[gi] * dec), xdt_b[rows, sl]))
            st = states[q]
            y_off = _dot(cm_b[rows, gs], _b16(st)) * ecs[:, sl]
            states[q] = st * chunk_decay[:, sl] + _dot_tn(bm_b[rows, gs], xdte_b[:, sl])
            ys.append(jnp.where(head0, yd[0], yd[1]) + y_off)
            yield
        y = jnp.concatenate(ys, axis=1) + xm[rows] * dfull_ref[...]
        y = y * gate[rows]
        ms = jnp.mean(y * y, axis=-1, keepdims=True)
        y_ref[rows, :] = y * lax.rsqrt(ms + NORM_EPS) * ng_ref[...]
        yield
    for q in range(N_PAIRS):
        st_ref[q] = states[q]


def _rwkv_body(first, tick, prep_tick, refs, has_vmix):
    if has_vmix:
        (rw_ref, rwp_ref, wa_ref, wap_ref, misc_ref, miscp_ref, vfirst_ref,
         mu_rw_ref, mu_wa_ref, mu_misc_ref, wup_ref, w0_ref, aup_ref, a0_ref, kk_ref, ka_ref, rk_ref,
         lng_ref, lnb_ref, vup_ref, v0_ref, tri_ref, y_ref, s_ref, pb_ref, pf_ref) = refs
    else:
        (rw_ref, rwp_ref, wa_ref, wap_ref,
         mu_rw_ref, mu_wa_ref, wup_ref, w0_ref, aup_ref, a0_ref, kk_ref, ka_ref, rk_ref,
         lng_ref, lnb_ref, tri_ref, y_ref, vout_ref, s_ref, pb_ref, pf_ref) = refs
    T = RW_T
    TB = LAYER_TB
    b16 = _b16

    def lerp(ref, halo_ref, mu_ref):
        x = ref[...]
        return x + (_prev_rows(x, halo_ref[...], first) - x) * mu_ref[...]

    xs = lerp(rw_ref, rwp_ref, mu_rw_ref)
    r, k, v, g = (xs[:, i * BW:(i + 1) * BW] for i in range(4))
    prep_tick()
    wa = lerp(wa_ref, wap_ref, mu_wa_ref)
    w_log = -_softplus(-(w0_ref[...] + _dot(b16(jnp.tanh(wa)), wup_ref[...]))) - 0.5
    lw = -jnp.exp(w_log)
    a = _sigmoid(a0_ref[...] + _dot(b16(wa), aup_ref[...]))
    if has_vmix:
        vd = lerp(misc_ref, miscp_ref, mu_misc_ref)
        mix = _sigmoid(v0_ref[...] + _dot(b16(vd), vup_ref[...]))
        v = v + (vfirst_ref[...] - v) * mix
    else:
        vout_ref[...] = v
    prep_tick()
    kk = k * kk_ref[...]
    kmod = k * (1.0 + (a - 1.0) * ka_ref[...])
    cums = _dot_split_rhs(tri_ref[...], lw)
    cl, rv = cums[:TB], cums[TB:]
    g_incl = jnp.exp(cl)
    g_inv = jnp.exp(-cl)
    g_end = jnp.exp(rv)
    rkr = r * kmod * rk_ref[...]
    prep_tick()

    def per_head_sum(x):
        h0 = lax.broadcasted_iota(jnp.int32, x.shape, 1) < HEAD
        f0 = h0.astype(F32)
        s0 = jnp.sum(jnp.where(h0, x, 0.0), axis=-1, keepdims=True)
        s1 = jnp.sum(jnp.where(h0, 0.0, x), axis=-1, keepdims=True)
        return s0 * f0 + s1 * (1.0 - f0)

    inv, bonus_w = [], []
    for q in range(N_PAIRS):
        sl = slice(q * LANES, (q + 1) * LANES)
        sq = kk[:, sl] * kk[:, sl]
        inv.append(1.0 / jnp.maximum(jnp.sqrt(per_head_sum(sq)), 1e-12))
        bonus_w.append(per_head_sum(rkr[:, sl]))
    kkn = kk * jnp.concatenate(inv, axis=1)
    bonus = jnp.concatenate(bonus_w, axis=1) * v
    prep_tick()
    kkna = kkn * a
    na_b = b16(-kkn * jnp.exp(cl - lw))
    nb_b = b16(kkna * g_inv)
    nbe_b = b16(kkna * g_end)
    pk_b = b16(kmod * g_inv)
    pke_b = b16(kmod * g_end)
    pr = r * g_incl
    for slot, val in enumerate((na_b, nb_b, nbe_b, pk_b, pke_b, b16(pr), b16(v))):
        pb_ref[slot] = val
    na_b, nb_b, nbe_b, pk_b, pke_b, pr_b, v_b = (pb_ref.at[slot] for slot in range(7))
    for slot, val in enumerate((pr, bonus, _silu(g))):
        pf_ref[slot] = val
    pr, bonus, gated = (pf_ref.at[slot] for slot in range(3))

    m0 = (lax.broadcasted_iota(jnp.int32, (1, LANES), 1) < HEAD).astype(BF16)
    m1 = 1.0 - m0
    row_w = lax.broadcasted_iota(jnp.int32, (T, LANES), 0)
    col_w = lax.broadcasted_iota(jnp.int32, (T, LANES), 1) % HEAD
    strict = col_w < row_w
    incl = col_w <= row_w
    eye_w = (col_w == row_w).astype(F32)
    rr = lax.broadcasted_iota(jnp.int32, (LANES, LANES), 0)
    cc = lax.broadcasted_iota(jnp.int32, (LANES, LANES), 1)
    same_head = (rr < HEAD) == (cc < HEAD)

    def stack(x):
        return jnp.concatenate([x * m0, x * m1], axis=0)

    def rows(c):
        return slice(c * T, (c + 1) * T)

    def lanes(q):
        return slice(q * LANES, (q + 1) * LANES)

    state = [s_ref[q] for q in range(N_PAIRS)]
    units = [(c, q) for c in range(RW_NB) for q in range(N_PAIRS)]
    a_n = [na_b[rows(c), lanes(q)] for c, q in units]
    r_n = [pr_b[rows(c), lanes(q)] for c, q in units]
    v_n = [v_b[rows(c), lanes(q)] for c, q in units]
    be_n = [nbe_b[rows(c), lanes(q)] for c, q in units]
    ke_n = [pke_b[rows(c), lanes(q)] for c, q in units]
    b_st = [stack(nb_b[rows(c), lanes(q)]) for c, q in units]
    k_st = [stack(pk_b[rows(c), lanes(q)]) for c, q in units]
    a_st = [stack(x) for x in a_n]
    v_st = [stack(x) for x in v_n]
    tick()
    aa = [_dot_nt(jnp.concatenate([a_, r_], axis=0), jnp.concatenate([b_, k_], axis=0))
          for a_, r_, b_, k_ in zip(a_n, r_n, b_st, k_st)]
    tick()
    a_ak = [b16(jnp.where(strict, x[:T, LANES:], 0.0)) for x in aa]
    a_rb = [b16(jnp.where(incl, x[T:, :LANES], 0.0)) for x in aa]
    a_rk = [b16(jnp.where(incl, x[T:, LANES:], 0.0)) for x in aa]
    p = [jnp.where(strict, x[:T, :LANES], 0.0) for x in aa]
    minv = [eye_w + x for x in p]
    tick()
    for _ in range(int(math.log2(T)) - 1):
        p_b = [b16(x) for x in p]
        p = [_dot(x, stack(x)) for x in p_b]
        tick()
        minv = [m + _dot(b16(x), stack(b16(m))) for x, m in zip(p, minv)]
        tick()
    akv = [_dot(x, y) for x, y in zip(a_ak, v_st)]
    tick()
    wu_b = [b16(_dot(b16(m), jnp.concatenate([a_, stack(b16(x))], axis=1)))
            for m, a_, x in zip(minv, a_st, akv)]
    tick()
    p_mat = [b16(jnp.where(same_head, _dot_tn(w[:, :LANES], be), 0.0)) for w, be in zip(wu_b, be_n)]
    tick()
    z_mat = [jnp.where(same_head,
                       _dot_tn(jnp.concatenate([w[:, LANES:], vn], axis=0), jnp.concatenate([be, ke], axis=0)),
                       0.0)
             for w, vn, be, ke in zip(wu_b, v_n, be_n, ke_n)]
    tick()
    q_mat = [b16(pr[rows(c), lanes(q)] + _dot(arb, stack(w[:, :LANES])))
             for (c, q), arb, w in zip(units, a_rb, wu_b)]
    tick()
    y0 = [_dot(jnp.concatenate([arb, ark], axis=1), jnp.concatenate([stack(w[:, LANES:]), vs], axis=0))
          for arb, ark, w, vs in zip(a_rb, a_rk, wu_b, v_st)]
    tick()
    ys = []
    for i, (c, q) in enumerate(units):
        s = state[q]
        s_b = b16(s)
        ys.append(_dot_nt(q_mat[i], s_b) + y0[i])
        state[q] = s * g_incl[(c + 1) * T - 1:(c + 1) * T, lanes(q)] + _dot(s_b, p_mat[i]) + z_mat[i]
    tick()
    for i, (c, q) in enumerate(units):
        y = ys[i]
        mean = per_head_sum(y) * (1.0 / HEAD)
        cen = y - mean
        var = per_head_sum(cen * cen) * (1.0 / HEAD)
        yn = cen * lax.rsqrt(var + GN_EPS)
        y_ref[rows(c), lanes(q)] = ((yn * lng_ref[:, lanes(q)] + lnb_ref[:, lanes(q)]
                                     + bonus[rows(c), lanes(q)]) * gated[rows(c), lanes(q)])
    tick()
    for q in range(N_PAIRS):
        s_ref[q] = state[q]


def _s5_state_tables(log_dt, a_re, a_im, b_re, b_im, c_re, c_im):
    T = S5_CT
    G, P = a_re.shape
    dt = jnp.exp(log_dt.astype(F32))[:, None]
    ar, ai = a_re.astype(F32), a_im.astype(F32)
    mag = jnp.exp(dt * ar)
    abar_re, abar_im = mag * jnp.cos(dt * ai), mag * jnp.sin(dt * ai)
    den = ar * ar + ai * ai
    f_re = ((abar_re - 1.0) * ar + abar_im * ai) / den
    f_im = (abar_im * ar - (abar_re - 1.0) * ai) / den
    br, bi = b_re.astype(F32), b_im.astype(F32)
    bb_re = f_re[..., None] * br - f_im[..., None] * bi
    bb_im = f_re[..., None] * bi + f_im[..., None] * br

    def powers(e):
        e = jnp.asarray(e, F32)[:, None, None]
        m = jnp.exp(e * (dt * ar)[None])
        ang = e * (dt * ai)[None]
        return (m * jnp.cos(ang)).reshape(-1, G * P), (m * jnp.sin(ang)).reshape(-1, G * P)

    pos = np.arange(T, dtype=np.float32)
    half = T / 2
    zr, zi = powers(half - pos)
    sr, si = powers(pos - half)
    cr, ci = powers(np.array([half + 1.0], np.float32))
    tiles = G // S5_TILE_G
    eye = jnp.eye(S5_TILE_G, dtype=F32)

    def b_side(bb):
        t = bb.reshape(tiles, S5_TILE_G, P, S5_GROUP).transpose(0, 1, 3, 2)
        return (t[:, :, :, None, :] * eye[None, :, None, :, None]).reshape(tiles, LANES, S5_TILE_G * P)

    def c_side(cc):
        t = cc.astype(F32).reshape(tiles, S5_TILE_G, S5_GROUP, P).transpose(0, 1, 3, 2)
        return (t[:, :, :, None, :] * eye[None, :, None, :, None]).reshape(tiles, S5_TILE_G * P, LANES)

    wb = jnp.concatenate([b_side(bb_re), b_side(bb_im)], axis=-1).astype(BF16)
    car = jnp.concatenate([cr, ci, sr[T - 1:T], si[T - 1:T]], axis=0)
    return dict(wb=wb, wc_re=c_side(c_re).astype(BF16), wc_im=c_side(-c_im.astype(F32)).astype(BF16),
                zr=zr.astype(BF16), zi=zi.astype(BF16), sr=sr.astype(BF16), si=si.astype(BF16), car=car)


def _s5_stages(box, u_ref, wb_ref, wcr_ref, wci_ref, zr_ref, zi_ref, sr_ref, si_ref, car_ref, d_ref, tri_ref,
               st_ref, sre_ref, sim_ref):
    T = S5_CT
    NS = S5_GROUPS * S5_STATE
    tiles = S5_GROUPS // S5_TILE_G
    half = NS // tiles
    u = u_ref[...]
    ub = u.astype(BF16)
    bu = []
    for j in range(tiles):
        bu.append(_b16(_dot(ub[:, j * LANES:(j + 1) * LANES], wb_ref[j])))
        yield
    bu_re = jnp.concatenate([b[:, :half] for b in bu], axis=1)
    bu_im = jnp.concatenate([b[:, half:] for b in bu], axis=1)
    zr, zi, sr, si = zr_ref[...], zi_ref[...], sr_ref[...], si_ref[...]
    car_re, car_im = car_ref[0:1, :], car_ref[1:2, :]
    sr_last, si_last = car_ref[2:3, :], car_ref[3:4, :]
    p_re, p_im = st_ref[0:1, :], st_ref[1:2, :]
    tri = tri_ref[...]
    for c in range(u.shape[0] // T):
        rows = slice(c * T, (c + 1) * T)
        b_re, b_im = bu_re[rows], bu_im[rows]
        z = jnp.concatenate([zr * b_re - zi * b_im, zr * b_im + zi * b_re], axis=1)
        yield
        w = _dot(tri, z)
        yield
        w_re = w[:, :NS] + (car_re * p_re - car_im * p_im)
        w_im = w[:, NS:] + (car_re * p_im + car_im * p_re)
        w_re_b, w_im_b = _b16(w_re), _b16(w_im)
        sre_ref[rows, :] = sr * w_re_b - si * w_im_b
        sim_ref[rows, :] = sr * w_im_b + si * w_re_b
        l_re, l_im = w_re[T - 1:T, :], w_im[T - 1:T, :]
        p_re = sr_last * l_re - si_last * l_im
        p_im = sr_last * l_im + si_last * l_re
        yield
    st_ref[0:1, :] = p_re
    st_ref[1:2, :] = p_im
    ys = []
    for j in range(tiles):
        ys.append(_dot(sre_ref[:, j * half:(j + 1) * half], wcr_ref[j])
                  + _dot(sim_ref[:, j * half:(j + 1) * half], wci_ref[j]))
        yield
    box['ys'] = jnp.concatenate(ys, axis=1) + d_ref[...] * u


N_RW_IN = {True: 22, False: 16}
N_MAMBA_IN = 12
N_S5_IN = 17


def _layer_kernel(*refs, has_vmix, final):
    n_rw_in = N_RW_IN[has_vmix]
    rw_in = refs[:n_rw_in]
    m_in = refs[n_rw_in:n_rw_in + N_MAMBA_IN]
    s5_in = refs[n_rw_in + N_MAMBA_IN:n_rw_in + N_MAMBA_IN + N_S5_IN]
    rest = refs[n_rw_in + N_MAMBA_IN + N_S5_IN:]
    (h_ref, u_ref, gs5_ref, wb_ref, wcr_ref, wci_ref, zr_ref, zi_ref, sr_ref, si_ref, car_ref, d_ref, tri5_ref,
     wglu_ref, bglu_ref, wo_ref, fg_ref) = s5_in
    if has_vmix:
        o_ref, s_ref, st_ref, s5st_ref, sre_ref, sim_ref, yrw_ref, ym2_ref, pb_ref, pf_ref = rest
        rw_refs = (*rw_in, yrw_ref, s_ref, pb_ref, pf_ref)
    else:
        o_ref, vout_ref, s_ref, st_ref, s5st_ref, sre_ref, sim_ref, yrw_ref, ym2_ref, pb_ref, pf_ref = rest
        rw_refs = (*rw_in, yrw_ref, vout_ref, s_ref, pb_ref, pf_ref)
    first = pl.program_id(0) == 0

    @pl.when(first)
    def _():
        s_ref[...] = jnp.zeros_like(s_ref)
        st_ref[...] = jnp.zeros_like(st_ref)
        s5st_ref[...] = jnp.zeros_like(s5st_ref)

    box = {}
    mamba = _mamba_stages(first, *m_in, ym2_ref, st_ref)
    s5 = _s5_stages(box, u_ref, wb_ref, wcr_ref, wci_ref, zr_ref, zi_ref, sr_ref, si_ref, car_ref, d_ref,
                    tri5_ref, s5st_ref, sre_ref, sim_ref)
    streams = [[mamba, M2_STAGES, 0], [s5, S5_STAGES, S5_PRE]]
    calls = [0]

    def tick():
        calls[0] += 1
        for st in streams:
            target = min(st[1], -(-calls[0] * st[1] // (RW_TICKS - 1)))
            while st[2] < target:
                next(st[0], None)
                st[2] += 1
        if calls[0] == RW_TICKS - 1:
            for st in streams:
                for _ in st[0]:
                    pass
            ys = _gelu_tanh(box['ys'])
            ys = ys * _sigmoid(_dot(ys.astype(BF16), wglu_ref[...]) + bglu_ref[...])
            ys = ys * _silu(gs5_ref[...])
            box['acc'] = (_dot(ys.astype(BF16), wo_ref[0:BW, :])
                          + _dot(ym2_ref[...].astype(BF16), wo_ref[2 * BW:3 * BW, :]))

    _rwkv_body(first, tick, lambda: next(s5), rw_refs, has_vmix)
    acc = box['acc'] + _dot(yrw_ref[...].astype(BF16), wo_ref[BW:2 * BW, :])
    hn = h_ref[...] + acc
    if final:
        ms = jnp.mean(hn * hn, axis=-1, keepdims=True)
        hn = hn * lax.rsqrt(ms + NORM_EPS) * fg_ref[...]
    o_ref[...] = hn


def _layer(h, proj, rp, mp, tb, d_row, w_glu, b_glu, w_out, final_g, v_first, final):
    L = proj.shape[0]
    TB = LAYER_TB
    NS = S5_GROUPS * S5_STATE
    has_vmix = v_first is not None
    hb = TB // SUBLANES

    def cur(width, col):
        return pl.BlockSpec((TB, width), lambda i: (i, col // width))

    def halo(width, col):
        return pl.BlockSpec((SUBLANES, width), lambda i: (jnp.maximum(i * hb - 1, 0), col // width))

    def full(a):
        return pl.BlockSpec(a.shape, (lambda i: (0, 0, 0)) if a.ndim == 3 else (lambda i: (0, 0)),
                            pipeline_mode=pl.Buffered(1))

    pos = np.arange(TB)
    same = (pos[:, None] // RW_T) == (pos[None, :] // RW_T)
    rw_tri = jnp.asarray(np.concatenate([same & (pos[None, :] <= pos[:, None]),
                                         same & (pos[None, :] > pos[:, None])], axis=0), BF16)
    args = [proj, proj, proj, proj]
    specs = [cur(4 * BW, COL_RW), halo(4 * BW, COL_RW), cur(LANES, COL_WA), halo(LANES, COL_WA)]
    if has_vmix:
        args += [proj, proj, v_first]
        specs += [cur(LANES, COL_MISC), halo(LANES, COL_MISC), pl.BlockSpec((TB, BW), lambda i: (i, 0))]
        names = ['mu_rw', 'mu_wa', 'mu_misc', 'w_up', 'w0', 'a_up', 'a0', 'k_k', 'k_a', 'r_k',
                 'ln_g', 'ln_b', 'v_up', 'v0']
    else:
        names = ['mu_rw', 'mu_wa', 'w_up', 'w0', 'a_up', 'a0', 'k_k', 'k_a', 'r_k', 'ln_g', 'ln_b']
    consts = [rp[n] for n in names] + [rw_tri]
    args += consts
    specs += [full(c) for c in consts]
    assert len(args) == N_RW_IN[has_vmix]

    m_tri = jnp.tril(jnp.ones((M2_T, M2_T), BF16))
    m_consts = [mp['conv_w'], mp['conv_b'], mp['dt_bias'], mp['a_log'], mp['d_full'], mp['norm_g'],
                mp['expand'], m_tri]
    args += [proj, proj, proj, proj] + m_consts
    specs += [cur(M2_XBC, COL_XBC), halo(M2_XBC, COL_XBC), cur(BW, COL_Z), cur(LANES, COL_MISC)]
    specs += [full(c) for c in m_consts]
    assert len(m_consts) + 4 == N_MAMBA_IN

    s5_tri = jnp.tril(jnp.ones((S5_CT, S5_CT), BF16))
    s5_consts = [tb['wb'], tb['wc_re'], tb['wc_im'], tb['zr'], tb['zi'], tb['sr'], tb['si'], tb['car'],
                 d_row, s5_tri, w_glu, b_glu, w_out, final_g]
    args += [h, proj, proj] + s5_consts
    specs += [pl.BlockSpec((TB, D_MODEL), lambda i: (i, 0)), cur(BW, COL_U), cur(BW, COL_GS5)]
    specs += [full(c) for c in s5_consts]
    assert len(s5_consts) + 3 == N_S5_IN

    tok = pl.BlockSpec((TB, BW), lambda i: (i, 0))
    out_specs = [pl.BlockSpec((TB, D_MODEL), lambda i: (i, 0))]
    out_shape = [jax.ShapeDtypeStruct((L, D_MODEL), F32)]
    if not has_vmix:
        out_specs.append(tok)
        out_shape.append(jax.ShapeDtypeStruct((L, BW), F32))
    res = pl.pallas_call(
        functools.partial(_layer_kernel, has_vmix=has_vmix, final=final),
        grid=(L // TB,),
        in_specs=specs,
        out_specs=out_specs,
        out_shape=out_shape,
        scratch_shapes=[pltpu.VMEM((N_PAIRS, LANES, LANES), F32), pltpu.VMEM((N_PAIRS, M2_STATE, LANES), F32),
                        pltpu.VMEM((SUBLANES, NS), F32), pltpu.VMEM((TB, NS), BF16), pltpu.VMEM((TB, NS), BF16),
                        pltpu.VMEM((TB, BW), F32), pltpu.VMEM((TB, BW), F32),
                        pltpu.VMEM((7, TB, BW), BF16), pltpu.VMEM((3, TB, BW), F32)],
        compiler_params=pltpu.CompilerParams(
            dimension_semantics=("arbitrary",), vmem_limit_bytes=VMEM_LIMIT),
    )(*args)
    return res[0], (v_first if has_vmix else res[1])


def _reorder_w_in(w, w_vmix):
    o = np.cumsum([0, BW, BW, BW, BW, BW, LORA_W, LORA_A, BW, M2_XBC, N_HEADS, BW])
    u, gs5, r, k, v, wd, ad, g, xbc, dt, z = (w[:, o[i]:o[i + 1]] for i in range(11))
    vm = w_vmix if w_vmix is not None else jnp.zeros((w.shape[0], LORA_V), w.dtype)
    pad = jnp.zeros((w.shape[0], LANES - N_HEADS - LORA_V), w.dtype)
    return jnp.concatenate([r, k, v, g, u, gs5, xbc, z, wd, ad, dt, vm, pad], axis=1).astype(BF16)


def _pad_rows(w, start, total=LANES):
    return jnp.pad(w.astype(F32), ((start, total - start - w.shape[0]), (0, 0)))


def _row(a):
    return a.astype(F32).reshape(1, -1)


def kernel(x, norm_g, w_in, w_in_vmix, s5_log_dt, s5_a_re, s5_a_im, s5_b_re, s5_b_im, s5_c_re, s5_c_im, s5_d, s5_w_glu, s5_b_glu, rwkv_mu, rwkv_w_up, rwkv_w0, rwkv_a_up, rwkv_a0, rwkv_k_k, rwkv_k_a, rwkv_r_k, rwkv_ln_g, rwkv_ln_b, rwkv_vmix_mu, rwkv_v_up, rwkv_v0, m2_conv_w, m2_conv_b, m2_dt_bias, m2_a_log, m2_d, m2_norm_g, w_out, final_norm_g):
    bsz, L, d = x.shape
    assert bsz == 1 and d == D_MODEL
    assert L % max(PROJ_TM, LAYER_TB) == 0 and LAYER_TB % M2_T == 0 and LAYER_TB % S5_CT == 0
    depth = w_in.shape[0]
    h = x.reshape(L, d).astype(F32)
    expand = jnp.pad(jnp.repeat(jnp.eye(N_HEADS, dtype=F32), HEAD, axis=1), ((0, LANES - N_HEADS), (0, 0)))
    v_first = None
    for i in range(depth):
        w_cat = _reorder_w_in(w_in[i], w_in_vmix[i - 1] if i > 0 else None)
        proj = _proj(h, _row(norm_g[i]), w_cat)

        mu = rwkv_mu[i].astype(F32)
        mo = np.cumsum([0, BW, BW, BW, LORA_W, LORA_A, BW])
        mr, mk, mv, mwd, mad, mg = (mu[mo[j]:mo[j + 1]] for j in range(6))
        rp = {
            'mu_rw': jnp.concatenate([mr, mk, mv, mg]).reshape(1, -1),
            'mu_wa': jnp.concatenate([mwd, mad]).reshape(1, -1),
            'w_up': _pad_rows(rwkv_w_up[i], 0).astype(BF16), 'w0': _row(rwkv_w0[i]),
            'a_up': _pad_rows(rwkv_a_up[i], LORA_W).astype(BF16), 'a0': _row(rwkv_a0[i]),
            'k_k': _row(rwkv_k_k[i]), 'k_a': _row(rwkv_k_a[i]), 'r_k': _row(rwkv_r_k[i]),
            'ln_g': _row(rwkv_ln_g[i]), 'ln_b': _row(rwkv_ln_b[i]),
        }
        if i > 0:
            rp['mu_misc'] = jnp.pad(rwkv_vmix_mu[i - 1].astype(F32),
                                    (N_HEADS, LANES - N_HEADS - LORA_V)).reshape(1, -1)
            rp['v_up'] = _pad_rows(rwkv_v_up[i - 1], N_HEADS).astype(BF16)
            rp['v0'] = _row(rwkv_v0[i - 1])
        mp = {
            'conv_w': m2_conv_w[i].astype(F32), 'conv_b': _row(m2_conv_b[i]),
            'dt_bias': jnp.pad(m2_dt_bias[i].astype(F32), (0, LANES - N_HEADS)).reshape(1, -1),
            'a_log': jnp.pad(m2_a_log[i].astype(F32), (0, LANES - N_HEADS)).reshape(1, -1),
            'd_full': jnp.repeat(m2_d[i].astype(F32), HEAD).reshape(1, -1),
            'norm_g': _row(m2_norm_g[i]), 'expand': expand.astype(BF16),
        }
        s5_tb = _s5_state_tables(s5_log_dt[i], s5_a_re[i], s5_a_im[i], s5_b_re[i], s5_b_im[i],
                                 s5_c_re[i], s5_c_im[i])
        h, v_first = _layer(h, proj, rp, mp, s5_tb, _row(s5_d[i]), s5_w_glu[i].astype(BF16),
                            _row(s5_b_glu[i]), w_out[i].astype(BF16), _row(final_norm_g), v_first,
                            final=(i == depth - 1))
    return h.reshape(bsz, L, d).astype(x.dtype)
```
